```python
import math
import jax, jax.numpy as jnp
from jax import lax
import numpy as np

D_MODEL = 1024
BATCH = 8
SEQ = 2048
DEPTH = 2

CTX_LEN = 256
GRID_W = 64
INNER = D_MODEL
HEAD_DIM = 64
A_WIDTH = INNER // 2
B_WIDTH = INNER - A_WIDTH
B_HEADS = B_WIDTH // HEAD_DIM
B_KV_HEADS = B_HEADS // 4
KV_WIDTH = B_KV_HEADS * HEAD_DIM
CONV_WIDTH = 31
Q_BLOCK = 128
ROPE_THETA = 10000.0
C_WIDTH = INNER // 2
C_GROUPS = 4
C_GROUP_DIM = C_WIDTH // C_GROUPS
D_WIDTH = INNER - C_WIDTH
D_HEADS = D_WIDTH // HEAD_DIM
NA_ROWS = 8
NA_COLS = 16
EPS = 1e-6
NEG_INF = -1e30
N_AB = (DEPTH + 1) // 2
N_CD = DEPTH // 2

AB_Q_OFF = 2 * A_WIDTH
AB_K_OFF = AB_Q_OFF + B_WIDTH
AB_V_OFF = AB_K_OFF + KV_WIDTH
AB_Z_OFF = AB_V_OFF + KV_WIDTH
AB_IN = AB_Z_OFF + INNER
CD_Q_OFF = C_WIDTH
CD_K_OFF = CD_Q_OFF + D_WIDTH
CD_V_OFF = CD_K_OFF + D_WIDTH
CD_Z_OFF = CD_V_OFF + D_WIDTH
CD_IN = CD_Z_OFF + INNER

kernel_name = "hybrid_prefix_dit_conv_gqa_fnet_natten"


def rms_norm(x, g):
    x32 = x.astype(jnp.float32)
    y = x32 * lax.rsqrt(jnp.mean(x32 * x32, axis=-1, keepdims=True) + EPS)
    return (y * g.astype(jnp.float32)).astype(x.dtype)


def layer_norm(x, g, b):
    x32 = x.astype(jnp.float32)
    mu = jnp.mean(x32, axis=-1, keepdims=True)
    xc = x32 - mu
    y = xc * lax.rsqrt(jnp.mean(xc * xc, axis=-1, keepdims=True) + EPS)
    return (y * g.astype(jnp.float32) + b.astype(jnp.float32)).astype(x.dtype)


def adaln(cond, w, b):
    m = jax.nn.silu(cond) @ w + b
    return jnp.split(m, 3, axis=-1)


def axial_rope(x):
    S = x.shape[1]
    t = jnp.arange(S)
    row = (t // GRID_W).astype(jnp.float32)
    col = (t % GRID_W).astype(jnp.float32)
    nf = HEAD_DIM // 4
    inv = ROPE_THETA ** (-jnp.arange(nf, dtype=jnp.float32) / nf)
    ang = jnp.concatenate([row[:, None] * inv, col[:, None] * inv], axis=-1)
    cos = jnp.cos(ang)[None, :, None, :]
    sin = jnp.sin(ang)[None, :, None, :]
    xp = x.astype(jnp.float32).reshape(x.shape[:-1] + (HEAD_DIM // 2, 2))
    a, b = xp[..., 0], xp[..., 1]
    out = jnp.stack([a * cos - b * sin, a * sin + b * cos], axis=-1)
    return out.reshape(x.shape).astype(x.dtype)


def dense_attn(q, k, v):
    Bq, Lq, Hq, dh = q.shape
    Hk = k.shape[2]
    qg = q.reshape(Bq, Lq, Hk, Hq // Hk, dh)
    s = jnp.einsum('bqkgd,bskd->bkgqs', qg, k).astype(jnp.float32) * (dh ** -0.5)
    p = jax.nn.softmax(s, axis=-1).astype(v.dtype)
    o = jnp.einsum('bkgqs,bskd->bqkgd', p, v)
    return o.reshape(Bq, Lq, Hq * dh)


def blocked_attn(q, k, v):
    Bq, S, Hq, dh = q.shape
    nb = S // Q_BLOCK
    qb = q.reshape(Bq, nb, Q_BLOCK, Hq, dh).transpose(1, 0, 2, 3, 4)
    ob = lax.map(lambda qi: dense_attn(qi, k, v), qb)
    return ob.transpose(1, 0, 2, 3).reshape(Bq, S, Hq * dh)


def dwconv(x, w, b):
    K = w.shape[0]
    y = lax.conv_general_dilated(x, w[:, None, :].astype(x.dtype), (1,), [(K // 2, K // 2)],
                                 dimension_numbers=('NWC', 'WIO', 'NWC'),
                                 feature_group_count=x.shape[-1])
    return y + b


def conformer_conv(a_val, a_gate, conv_w, conv_b, ln_g, ln_b):
    u = a_val * jax.nn.sigmoid(a_gate)
    u = dwconv(u, conv_w, conv_b)
    return jax.nn.silu(layer_norm(u, ln_g, ln_b))


def fourier_mix(f, w_c):
    Bf, L, _ = f.shape
    fg = f.reshape(Bf, L, C_GROUPS, C_GROUP_DIM).astype(jnp.float32)
    fr = jnp.fft.fft2(fg, axes=(1, 3), norm='ortho').real.astype(f.dtype)
    return jnp.einsum('blgc,gce->blge', fr, w_c).reshape(Bf, L, C_WIDTH)


def neighborhood_attn(q, k, v, kc, vc, rpb):
    Bq, S, H, dh = q.shape
    rows = S // GRID_W
    kr = min(NA_ROWS, rows)
    qg = q.reshape(Bq, rows, GRID_W, H, dh)
    kg = k.reshape(Bq, rows, GRID_W, H, dh)
    vg = v.reshape(Bq, rows, GRID_W, H, dh)
    r = jnp.arange(rows)
    rs = jnp.clip(r - kr // 2, 0, rows - kr)
    row_idx = rs[:, None] + jnp.arange(kr)[None, :]
    kb = kg[:, row_idx]
    vb = vg[:, row_idx]
    wq = jnp.arange(GRID_W)
    cs = jnp.clip(wq - NA_COLS // 2, 0, GRID_W - NA_COLS)
    col_ok = (wq[None, :] >= cs[:, None]) & (wq[None, :] < cs[:, None] + NA_COLS)
    dr_idx = row_idx - r[:, None] + (NA_ROWS - 1)
    dc_idx = jnp.clip(wq[None, :] - wq[:, None] + (NA_COLS - 1), 0, 2 * NA_COLS - 2)
    bias = rpb[:, dr_idx[:, None, :, None], dc_idx[None, :, None, :]]
    scale = dh ** -0.5
    s_win = jnp.einsum('brqhd,brkwhd->bhrqkw', qg, kb).astype(jnp.float32) * scale \
        + bias.astype(jnp.float32)[None]
    s_win = jnp.where(col_ok[:, None, :], s_win, NEG_INF).reshape(Bq, H, rows, GRID_W, kr * GRID_W)
    s_ctx = jnp.einsum('brqhd,bchd->bhrqc', qg, kc).astype(jnp.float32) * scale
    p = jax.nn.softmax(jnp.concatenate([s_win, s_ctx], axis=-1), axis=-1).astype(v.dtype)
    p_win = p[..., :kr * GRID_W].reshape(Bq, H, rows, GRID_W, kr, GRID_W)
    p_ctx = p[..., kr * GRID_W:]
    o = jnp.einsum('bhrqkw,brkwhd->brqhd', p_win, vb) + jnp.einsum('bhrqc,bchd->brqhd', p_ctx, vc)
    return o.reshape(Bq, S, H * dh)


def layer_ab(x, xc, c, c_ctx, w_ada, b_ada, norm_g, w_in, conv_w, conv_b, ln_g, ln_b,
             qn_g, kn_g, w_out, need_ctx):
    Bx, S, _ = x.shape
    L = xc.shape[1]
    shift, scale, gate = adaln(c[:, None, :], w_ada, b_ada)
    cshift, cscale, cgate = adaln(c_ctx, w_ada, b_ada)
    h = rms_norm(x, norm_g) * (1 + scale) + shift
    hc = rms_norm(xc, norm_g) * (1 + cscale) + cshift
    u = h @ w_in
    a_val, a_gate = u[..., :A_WIDTH], u[..., A_WIDTH:AB_Q_OFF]
    q = u[..., AB_Q_OFF:AB_K_OFF].reshape(Bx, S, B_HEADS, HEAD_DIM)
    k = u[..., AB_K_OFF:AB_V_OFF].reshape(Bx, S, B_KV_HEADS, HEAD_DIM)
    v = u[..., AB_V_OFF:AB_Z_OFF].reshape(Bx, S, B_KV_HEADS, HEAD_DIM)
    z = u[..., AB_Z_OFF:]
    uc = hc @ (w_in if need_ctx else w_in[:, AB_K_OFF:AB_Z_OFF])
    kv_off = AB_K_OFF if need_ctx else 0
    ck = uc[..., kv_off:kv_off + KV_WIDTH].reshape(Bx, L, B_KV_HEADS, HEAD_DIM)
    cv = uc[..., kv_off + KV_WIDTH:kv_off + 2 * KV_WIDTH].reshape(Bx, L, B_KV_HEADS, HEAD_DIM)
    ck = rms_norm(ck, kn_g)
    q = axial_rope(rms_norm(q, qn_g))
    k = axial_rope(rms_norm(k, kn_g))
    y_b = blocked_attn(q, jnp.concatenate([ck, k], axis=1), jnp.concatenate([cv, v], axis=1))
    y_a = conformer_conv(a_val, a_gate, conv_w, conv_b, ln_g, ln_b)
    y = jnp.concatenate([y_a, y_b], axis=-1) * jax.nn.silu(z)
    x = x + gate * (y @ w_out)
    if need_ctx:
        cq = rms_norm(uc[..., AB_Q_OFF:AB_K_OFF].reshape(Bx, L, B_HEADS, HEAD_DIM), qn_g)
        yc_b = dense_attn(cq, ck, cv)
        yc_a = conformer_conv(uc[..., :A_WIDTH], uc[..., A_WIDTH:AB_Q_OFF], conv_w, conv_b, ln_g, ln_b)
        yc = jnp.concatenate([yc_a, yc_b], axis=-1) * jax.nn.silu(uc[..., AB_Z_OFF:])
        xc = xc + cgate * (yc @ w_out)
    return x, xc


def layer_cd(x, xc, c, c_ctx, w_ada, b_ada, norm_g, w_in, w_fourier, rpb, w_out, need_ctx):
    Bx, S, _ = x.shape
    L = xc.shape[1]
    shift, scale, gate = adaln(c[:, None, :], w_ada, b_ada)
    cshift, cscale, cgate = adaln(c_ctx, w_ada, b_ada)
    h = rms_norm(x, norm_g) * (1 + scale) + shift
    hc = rms_norm(xc, norm_g) * (1 + cscale) + cshift
    u = h @ w_in
    f = u[..., :C_WIDTH]
    q = u[..., CD_Q_OFF:CD_K_OFF].reshape(Bx, S, D_HEADS, HEAD_DIM)
    k = u[..., CD_K_OFF:CD_V_OFF].reshape(Bx, S, D_HEADS, HEAD_DIM)
    v = u[..., CD_V_OFF:CD_Z_OFF].reshape(Bx, S, D_HEADS, HEAD_DIM)
    z = u[..., CD_Z_OFF:]
    uc = hc @ (w_in if need_ctx else w_in[:, CD_K_OFF:CD_Z_OFF])
    kv_off = CD_K_OFF if need_ctx else 0
    ck = uc[..., kv_off:kv_off + D_WIDTH].reshape(Bx, L, D_HEADS, HEAD_DIM)
    cv = uc[..., kv_off + D_WIDTH:kv_off + 2 * D_WIDTH].reshape(Bx, L, D_HEADS, HEAD_DIM)
    y_c = fourier_mix(f, w_fourier)
    y_d = neighborhood_attn(q, k, v, ck, cv, rpb)
    y = jnp.concatenate([y_c, y_d], axis=-1) * jax.nn.silu(z)
    x = x + gate * (y @ w_out)
    if need_ctx:
        cq = uc[..., CD_Q_OFF:CD_K_OFF].reshape(Bx, L, D_HEADS, HEAD_DIM)
        yc_d = dense_attn(cq, ck, cv)
        yc_c = fourier_mix(uc[..., :C_WIDTH], w_fourier)
        yc = jnp.concatenate([yc_c, yc_d], axis=-1) * jax.nn.silu(uc[..., CD_Z_OFF:])
        xc = xc + cgate * (yc @ w_out)
    return x, xc


def setup_inputs(seed: int = 0) -> dict:
    key = jax.random.key(seed)
    ks = iter(jax.random.split(key, 32))
    nrm = lambda shape, s: jax.random.normal(next(ks), shape, jnp.float32) * s
    gain = lambda shape: 1.0 + nrm(shape, 0.05)
    return {
        "x": nrm((BATCH, SEQ, D_MODEL), 1.0),
        "c": nrm((BATCH, D_MODEL), 1.0),
        "ctx": nrm((BATCH, CTX_LEN, D_MODEL), 1.0),
        "c_ctx": nrm((D_MODEL,), 1.0),
        "ab_w_ada": nrm((N_AB, D_MODEL, 3 * D_MODEL), 0.5 * D_MODEL ** -0.5),
        "ab_b_ada": nrm((N_AB, 3 * D_MODEL), 0.01),
        "ab_norm_g": gain((N_AB, D_MODEL)),
        "ab_w_in": nrm((N_AB, D_MODEL, AB_IN), D_MODEL ** -0.5),
        "ab_conv_w": nrm((N_AB, CONV_WIDTH, A_WIDTH), CONV_WIDTH ** -0.5),
        "ab_conv_b": nrm((N_AB, A_WIDTH), 0.01),
        "ab_ln_g": gain((N_AB, A_WIDTH)),
        "ab_ln_b": nrm((N_AB, A_WIDTH), 0.01),
        "ab_q_norm_g": gain((N_AB, HEAD_DIM)),
        "ab_k_norm_g": gain((N_AB, HEAD_DIM)),
        "ab_w_out": nrm((N_AB, INNER, D_MODEL), INNER ** -0.5),
        "cd_w_ada": nrm((N_CD, D_MODEL, 3 * D_MODEL), 0.5 * D_MODEL ** -0.5),
        "cd_b_ada": nrm((N_CD, 3 * D_MODEL), 0.01),
        "cd_norm_g": gain((N_CD, D_MODEL)),
        "cd_w_in": nrm((N_CD, D_MODEL, CD_IN), D_MODEL ** -0.5),
        "cd_w_fourier": nrm((N_CD, C_GROUPS, C_GROUP_DIM, C_GROUP_DIM), C_GROUP_DIM ** -0.5),
        "cd_rpb": nrm((N_CD, D_HEADS, 2 * NA_ROWS - 1, 2 * NA_COLS - 1), 0.1),
        "cd_w_out": nrm((N_CD, INNER, D_MODEL), INNER ** -0.5),
        "final_norm_g": gain((D_MODEL,)),
    }


def reference(x, c, ctx, c_ctx, ab_w_ada, ab_b_ada, ab_norm_g, ab_w_in, ab_conv_w, ab_conv_b,
              ab_ln_g, ab_ln_b, ab_q_norm_g, ab_k_norm_g, ab_w_out, cd_w_ada, cd_b_ada, cd_norm_g,
              cd_w_in, cd_w_fourier, cd_rpb, cd_w_out, final_norm_g):
    xc = ctx
    for layer in range(DEPTH):
        need_ctx = layer < DEPTH - 1
        i = layer // 2
        if layer % 2 == 0:
            x, xc = layer_ab(x, xc, c, c_ctx, ab_w_ada[i], ab_b_ada[i], ab_norm_g[i], ab_w_in[i],
                             ab_conv_w[i], ab_conv_b[i], ab_ln_g[i], ab_ln_b[i],
                             ab_q_norm_g[i], ab_k_norm_g[i], ab_w_out[i], need_ctx)
        else:
            x, xc = layer_cd(x, xc, c, c_ctx, cd_w_ada[i], cd_b_ada[i], cd_norm_g[i], cd_w_in[i],
                             cd_w_fourier[i], cd_rpb[i], cd_w_out[i], need_ctx)
    return rms_norm(x, final_norm_g)
```

```python
import functools
import math

import numpy as np
import jax
import jax.numpy as jnp
from jax import lax
from jax.experimental import pallas as pl
from jax.experimental.pallas import tpu as pltpu

F32 = jnp.float32
BF16 = jnp.bfloat16
HIGHEST = lax.Precision.HIGHEST

D_MODEL = 1024
HEAD_DIM = 64
GRID_W = 64
CONV_WIDTH = 31
NA_ROWS = 8
NA_COLS = 16
ROPE_THETA = 10000.0
EPS = 1e-6
NEG_INF = -1e30
LANES = 128
HALO = 16
VMEM_LIMIT = 48 * 1024 * 1024

NA_TILE_ROWS = 4
NA_WIN_ROWS = NA_TILE_ROWS + NA_ROWS

GQA_HEAD_ORDER = (0, 4, 1, 5, 2, 6, 3, 7)


def _params(sem):
    return pltpu.CompilerParams(dimension_semantics=sem, vmem_limit_bytes=VMEM_LIMIT)


def _silu(t):
    return t * jax.nn.sigmoid(t)


def _dot(a, b):
    return jnp.dot(a, b, preferred_element_type=F32)


def _dot_nt(a, b):
    return lax.dot_general(a, b, (((1,), (1,)), ((), ())), preferred_element_type=F32)


def _rope_tables(seq):
    t = np.arange(seq)
    row = (t // GRID_W).astype(np.float64)
    col = (t % GRID_W).astype(np.float64)
    nf = HEAD_DIM // 4
    inv = np.float32(ROPE_THETA) ** (-np.arange(nf, dtype=np.float32) / nf)
    ang = np.concatenate([row[:, None] * inv, col[:, None] * inv], axis=-1)
    cos = np.repeat(np.cos(ang), 2, axis=-1)
    sin = np.repeat(np.sin(ang), 2, axis=-1)
    sign = np.tile(np.array([-1.0, 1.0]), HEAD_DIM // 2)
    cos = np.tile(cos, (1, LANES // HEAD_DIM))
    sin = np.tile(sin * sign, (1, LANES // HEAD_DIM))
    return cos.astype(np.float32), sin.astype(np.float32)


def _dft_tables(n):
    k = np.arange(n)
    ang = 2.0 * np.pi * ((k[:, None] * k[None, :]) % n) / n
    s = 1.0 / math.sqrt(n)
    return (np.cos(ang) * s).astype(np.float32), (np.sin(ang) * s).astype(np.float32)


def _block_ones(width):
    h = np.arange(width) // HEAD_DIM
    return (h[:, None] == h[None, :]).astype(np.float32)


def _adaln_body(c_ref, w_ref, b_ref, o_ref):
    a = _silu(c_ref[...])
    o_ref[...] = jnp.dot(a, w_ref[...], preferred_element_type=F32, precision=HIGHEST) + b_ref[...]


def _adaln(cond, w, b):
    r, d = cond.shape
    n = w.shape[1]
    tn = 1024
    return pl.pallas_call(
        _adaln_body,
        grid=(n // tn,),
        in_specs=[pl.BlockSpec((r, d), lambda j: (0, 0)),
                  pl.BlockSpec((d, tn), lambda j: (0, j)),
                  pl.BlockSpec((1, tn), lambda j: (0, j))],
        out_specs=pl.BlockSpec((r, tn), lambda j: (0, j)),
        out_shape=jax.ShapeDtypeStruct((r, n), F32),
        compiler_params=_params(("arbitrary",)),
        name="adaln",
    )(cond, w, b.reshape(1, n))


def _head_rms(t, ones_ref, gain):
    t2 = t * t
    hi = t2.astype(BF16)
    lo = (t2 - hi.astype(F32)).astype(BF16)
    ms = (_dot(hi, ones_ref[...]) + _dot(lo, ones_ref[...])) * (1.0 / HEAD_DIM)
    return t * lax.rsqrt(ms + EPS) * gain


def _rope(t, cos, sin_signed):
    rows = t.shape[0]
    lane = lax.broadcasted_iota(jnp.int32, (rows, LANES), 1)
    even = (lane & 1) == 0
    outs = []
    for c in range(t.shape[1] // LANES):
        xc = t[:, c * LANES:(c + 1) * LANES]
        swapped = jnp.where(even, pltpu.roll(xc, LANES - 1, 1), pltpu.roll(xc, 1, 1))
        outs.append(xc * cos + swapped * sin_signed)
    return outs[0] if len(outs) == 1 else jnp.concatenate(outs, axis=1)


def _modulated_norm(x_ref, g_ref, s1p_ref, sh_ref):
    x = x_ref[0]
    y = x * lax.rsqrt(jnp.mean(x * x, axis=-1, keepdims=True) + EPS) * g_ref[...]
    return (y * s1p_ref[0] + sh_ref[0]).astype(BF16)


def _inproj_ab_body(rope, x_ref, g_ref, s1p_ref, sh_ref, w_ref, qg_ref, kg_ref, cos_ref, sin_ref, mq_ref, mk_ref,
                    glu_ref, q_ref, k_ref, v_ref, sz_ref):
    u = _dot(_modulated_norm(x_ref, g_ref, s1p_ref, sh_ref), w_ref[...])
    glu_ref[0] = u[:, 0:512] * jax.nn.sigmoid(u[:, 512:1024])
    q = _head_rms(u[:, 1024:1536], mq_ref, qg_ref[...])
    k = _head_rms(u[:, 1536:1664], mk_ref, kg_ref[...])
    if rope:
        q = _rope(q, cos_ref[...], sin_ref[...])
        k = _rope(k, cos_ref[...], sin_ref[...])
    q_ref[0] = (q * (HEAD_DIM ** -0.5)).astype(BF16)
    k_ref[0] = k.astype(BF16)
    v_ref[0] = u[:, 1664:1792].astype(BF16)
    sz_ref[0] = _silu(u[:, 1792:2816]).astype(BF16)


def _inproj_ab(x, g, s1p, sh, w, qg, kg, cos, sin, mq, mk, rope, tm):
    b, s, d = x.shape
    n = w.shape[1]
    row = lambda bi, i: (bi, i, 0)
    const = lambda bi, i: (0, 0)
    per_b = lambda bi, i: (bi, 0, 0)
    tab = lambda bi, i: (i, 0)
    return pl.pallas_call(
        functools.partial(_inproj_ab_body, rope),
        grid=(b, s // tm),
        in_specs=[pl.BlockSpec((1, tm, d), row),
                  pl.BlockSpec((1, d), const),
                  pl.BlockSpec((1, 1, d), per_b),
                  pl.BlockSpec((1, 1, d), per_b),
                  pl.BlockSpec((d, n), const),
                  pl.BlockSpec((1, 512), const),
                  pl.BlockSpec((1, 128), const),
                  pl.BlockSpec((tm, LANES), tab),
                  pl.BlockSpec((tm, LANES), tab),
                  pl.BlockSpec((512, 512), const),
                  pl.BlockSpec((128, 128), const)],
        out_specs=[pl.BlockSpec((1, tm, 512), row),
                   pl.BlockSpec((1, tm, 512), row),
                   pl.BlockSpec((1, tm, 128), row),
                   pl.BlockSpec((1, tm, 128), row),
                   pl.BlockSpec((1, tm, 1024), row)],
        out_shape=[jax.ShapeDtypeStruct((b, s, 512), F32),
                   jax.ShapeDtypeStruct((b, s, 512), BF16),
                   jax.ShapeDtypeStruct((b, s, 128), BF16),
                   jax.ShapeDtypeStruct((b, s, 128), BF16),
                   jax.ShapeDtypeStruct((b, s, 1024), BF16)],
        compiler_params=_params(("arbitrary", "arbitrary")),
        name="inproj_ab_rope" if rope else "inproj_ab_ctx",
    )(x, g, s1p, sh, w, qg, kg, cos, sin, mq, mk)


def _inproj_cd_body(x_ref, g_ref, s1p_ref, sh_ref, w_ref, f_ref, q_ref, k_ref, v_ref, sz_ref):
    u = _dot(_modulated_norm(x_ref, g_ref, s1p_ref, sh_ref), w_ref[...])
    f_ref[0] = u[:, 0:512].astype(BF16)
    q_ref[0] = (u[:, 512:1024] * (HEAD_DIM ** -0.5)).astype(BF16)
    k_ref[0] = u[:, 1024:1536].astype(BF16)
    v_ref[0] = u[:, 1536:2048].astype(BF16)
    sz_ref[0] = _silu(u[:, 2048:3072]).astype(BF16)


def _inproj_cd(x, g, s1p, sh, w, tm):
    b, s, d = x.shape
    n = w.shape[1]
    row = lambda bi, i: (bi, i, 0)
    const = lambda bi, i: (0, 0)
    per_b = lambda bi, i: (bi, 0, 0)
    return pl.pallas_call(
        _inproj_cd_body,
        grid=(b, s // tm),
        in_specs=[pl.BlockSpec((1, tm, d), row),
                  pl.BlockSpec((1, d), const),
                  pl.BlockSpec((1, 1, d), per_b),
                  pl.BlockSpec((1, 1, d), per_b),
                  pl.BlockSpec((d, n), const)],
        out_specs=[pl.BlockSpec((1, tm, 512), row)] * 4 + [pl.BlockSpec((1, tm, 1024), row)],
        out_shape=[jax.ShapeDtypeStruct((b, s, 512), BF16)] * 4 + [jax.ShapeDtypeStruct((b, s, 1024), BF16)],
        compiler_params=_params(("arbitrary", "arbitrary")),
        name="inproj_cd",
    )(x, g, s1p, sh, w)


def _inproj_kv_body(x_ref, g_ref, s1p_ref, sh_ref, w_ref, k_ref, v_ref):
    u = _dot(_modulated_norm(x_ref, g_ref, s1p_ref, sh_ref), w_ref[...])
    k_ref[0] = u[:, 0:512].astype(BF16)
    v_ref[0] = u[:, 512:1024].astype(BF16)


def _inproj_kv(x, g, s1p, sh, w):
    b, s, d = x.shape
    n = w.shape[1]
    row = lambda bi: (bi, 0, 0)
    const = lambda bi: (0, 0)
    return pl.pallas_call(
        _inproj_kv_body,
        grid=(b,),
        in_specs=[pl.BlockSpec((1, s, d), row),
                  pl.BlockSpec((1, d), const),
                  pl.BlockSpec((1, 1, d), row),
                  pl.BlockSpec((1, 1, d), row),
                  pl.BlockSpec((d, n), const)],
        out_specs=[pl.BlockSpec((1, s, 512), row)] * 2,
        out_shape=[jax.ShapeDtypeStruct((b, s, 512), BF16)] * 2,
        compiler_params=_params(("arbitrary",)),
        name="inproj_kv_ctx",
    )(x, g, s1p, sh, w)


def _split_heads(q):
    lane = lax.broadcasted_iota(jnp.int32, q.shape, 1)
    zero = jnp.zeros_like(q)
    return jnp.concatenate([jnp.where(lane < HEAD_DIM, q, zero), jnp.where(lane >= HEAD_DIM, q, zero)], axis=0)


def _merge_heads(o2, rows):
    lane = lax.broadcasted_iota(jnp.int32, (rows, LANES), 1)
    return jnp.where(lane < HEAD_DIM, o2[:rows], o2[rows:])


def _softmax_pv(scores, values):
    m = functools.reduce(jnp.maximum, [jnp.max(s, axis=-1, keepdims=True) for s in scores])
    ps = [jnp.exp(s - m) for s in scores]
    l = functools.reduce(jnp.add, [jnp.sum(p, axis=-1, keepdims=True) for p in ps])
    o = functools.reduce(jnp.add, [_dot(p.astype(BF16), v) for p, v in zip(ps, values)])
    return o / l


def _gqa_body(n_src, q_ref, *refs):
    k_refs, v_refs, o_ref = refs[:n_src], refs[n_src:2 * n_src], refs[2 * n_src]
    rows = q_ref.shape[1]
    q2 = _split_heads(q_ref[0])
    scores = [_dot_nt(q2, k[0]) for k in k_refs]
    o2 = _softmax_pv(scores, [v[0] for v in v_refs])
    o_ref[0] = _merge_heads(o2, rows).astype(o_ref.dtype)


def _gqa_attention(q, ks, vs, tq):
    b, lq, w = q.shape
    n_src = len(ks)
    kv_specs = [pl.BlockSpec((1, k.shape[1], LANES), lambda bi, i, j: (bi, 0, 0)) for k in ks]
    return pl.pallas_call(
        functools.partial(_gqa_body, n_src),
        grid=(b, lq // tq, w // LANES),
        in_specs=[pl.BlockSpec((1, tq, LANES), lambda bi, i, j: (bi, i, j))] + kv_specs + kv_specs,
        out_specs=pl.BlockSpec((1, tq, LANES), lambda bi, i, j: (bi, i, j)),
        out_shape=jax.ShapeDtypeStruct((b, lq, w), BF16),
        compiler_params=_params(("arbitrary", "arbitrary", "arbitrary")),
        name=f"gqa_attention_{n_src}src",
    )(q, *ks, *vs)


def _conv_body(rows, x_ref, prev_ref, next_ref, w_ref, cb_ref, lg_ref, lb_ref, o_ref, pad_ref):
    i = pl.program_id(1)
    last = pl.num_programs(1) - 1
    pad_ref[0:HALO, :] = jnp.where(i > 0, prev_ref[0], 0.0)
    pad_ref[HALO:HALO + rows, :] = x_ref[0]
    pad_ref[HALO + rows:HALO + rows + HALO, :] = jnp.where(i < last, next_ref[0], 0.0)
    sub = 32
    base = HALO - CONV_WIDTH // 2
    for r0 in range(0, rows, sub):
        acc = jnp.zeros((sub, x_ref.shape[2]), F32) + cb_ref[...]
        for k in range(CONV_WIDTH):
            acc = acc + pad_ref[base + r0 + k:base + r0 + k + sub, :] * w_ref[k:k + 1, :]
        mu = jnp.mean(acc, axis=-1, keepdims=True)
        xc = acc - mu
        y = xc * lax.rsqrt(jnp.mean(xc * xc, axis=-1, keepdims=True) + EPS) * lg_ref[...] + lb_ref[...]
        o_ref[0, r0:r0 + sub, :] = _silu(y).astype(o_ref.dtype)


def _conformer_conv(u, conv_w, conv_b, ln_g, ln_b, rows):
    b, l, c = u.shape
    hb = rows // HALO
    n_halo = l // HALO
    return pl.pallas_call(
        functools.partial(_conv_body, rows),
        grid=(b, l // rows),
        in_specs=[pl.BlockSpec((1, rows, c), lambda bi, i: (bi, i, 0)),
                  pl.BlockSpec((1, HALO, c), lambda bi, i: (bi, jnp.maximum(i * hb - 1, 0), 0)),
                  pl.BlockSpec((1, HALO, c), lambda bi, i: (bi, jnp.minimum((i + 1) * hb, n_halo - 1), 0)),
                  pl.BlockSpec((CONV_WIDTH, c), lambda bi, i: (0, 0)),
                  pl.BlockSpec((1, c), lambda bi, i: (0, 0)),
                  pl.BlockSpec((1, c), lambda bi, i: (0, 0)),
                  pl.BlockSpec((1, c), lambda bi, i: (0, 0))],
        out_specs=pl.BlockSpec((1, rows, c), lambda bi, i: (bi, i, 0)),
        out_shape=jax.ShapeDtypeStruct((b, l, c), BF16),
        scratch_shapes=[pltpu.VMEM((rows + 2 * HALO, c), F32)],
        compiler_params=_params(("arbitrary", "arbitrary")),
        name="conformer_conv",
    )(u, u, u, conv_w, conv_b.reshape(1, c), ln_g.reshape(1, c), ln_b.reshape(1, c))


def _outproj_body(final, ya_ref, yb_ref, sz_ref, w_ref, x_ref, gate_ref, *rest):
    y = jnp.concatenate([ya_ref[0], yb_ref[0]], axis=-1).astype(F32) * sz_ref[0].astype(F32)
    xn = x_ref[0] + gate_ref[0] * _dot(y.astype(BF16), w_ref[...])
    if final:
        fg_ref, o_ref = rest
        o_ref[0] = xn * lax.rsqrt(jnp.mean(xn * xn, axis=-1, keepdims=True) + EPS) * fg_ref[...]
    else:
        rest[0][0] = xn


def _outproj(ya, yb, sz, w, x, gate, final_g, tm):
    b, s, d = x.shape
    row = lambda bi, i: (bi, i, 0)
    const = lambda bi, i: (0, 0)
    per_b = lambda bi, i: (bi, 0, 0)
    in_specs = [pl.BlockSpec((1, tm, 512), row),
                pl.BlockSpec((1, tm, 512), row),
                pl.BlockSpec((1, tm, 1024), row),
                pl.BlockSpec((1024, d), const),
                pl.BlockSpec((1, tm, d), row),
                pl.BlockSpec((1, 1, d), per_b)]
    args = [ya, yb, sz, w, x, gate]
    if final_g is not None:
        in_specs.append(pl.BlockSpec((1, d), const))
        args.append(final_g)
    return pl.pallas_call(
        functools.partial(_outproj_body, final_g is not None),
        grid=(b, s // tm),
        in_specs=in_specs,
        out_specs=pl.BlockSpec((1, tm, d), row),
        out_shape=jax.ShapeDtypeStruct((b, s, d), F32),
        compiler_params=_params(("arbitrary", "arbitrary")),
        name="outproj_final" if final_g is not None else "outproj",
    )(*args)


def _fourier_w_body(cc_ref, sc_ref, w_ref, o_ref):
    for g in range(w_ref.shape[0]):
        wc = jnp.dot(cc_ref[...], w_ref[g], preferred_element_type=F32, precision=HIGHEST)
        ws = jnp.dot(sc_ref[...], w_ref[g], preferred_element_type=F32, precision=HIGHEST)
        o_ref[g] = jnp.concatenate([wc, ws], axis=-1).astype(o_ref.dtype)


def _fourier_weights(cc, sc, w):
    g, c, _ = w.shape
    return pl.pallas_call(
        _fourier_w_body,
        out_shape=jax.ShapeDtypeStruct((g, c, 2 * c), BF16),
        name="fourier_weights",
    )(cc, sc, w)


def _fourier_body(f_ref, w1_ref, cs_ref, o_ref, y_ref):
    groups, c, _ = w1_ref.shape
    seq = f_ref.shape[1]

    @pl.when(pl.program_id(1) == 0)
    def _():
        for g in range(groups):
            yg = _dot(f_ref[0, :, g * c:(g + 1) * c], w1_ref[g])
            y_ref[0:seq, g * c:(g + 1) * c] = yg[:, :c].astype(BF16)
            y_ref[seq:2 * seq, g * c:(g + 1) * c] = yg[:, c:].astype(BF16)

    o_ref[0] = _dot(cs_ref[...], y_ref[...]).astype(o_ref.dtype)


def _fourier_mix(f, w1, cs, tl):
    b, s, w = f.shape
    return pl.pallas_call(
        _fourier_body,
        grid=(b, s // tl),
        in_specs=[pl.BlockSpec((1, s, w), lambda bi, i: (bi, 0, 0)),
                  pl.BlockSpec(w1.shape, lambda bi, i: (0, 0, 0)),
                  pl.BlockSpec((tl, 2 * s), lambda bi, i: (i, 0))],
        out_specs=pl.BlockSpec((1, tl, w), lambda bi, i: (bi, i, 0)),
        out_shape=jax.ShapeDtypeStruct((b, s, w), BF16),
        scratch_shapes=[pltpu.VMEM((2 * s, w), BF16)],
        compiler_params=_params(("arbitrary", "arbitrary")),
        name="fourier_mix",
    )(f, w1, cs)


def _na_tile_geometry(rows_total):
    last_r0 = rows_total - NA_TILE_ROWS
    return ((0, 0), (2 * NA_TILE_ROWS, 2 * NA_TILE_ROWS - NA_ROWS // 2), (last_r0, rows_total - NA_WIN_ROWS))


def _na_bias_body(rows_total, rpb_ref, o_ref, t2_ref):
    h = pl.program_id(0)
    wq = lax.broadcasted_iota(jnp.int32, (GRID_W, LANES), 0)
    wk = lax.broadcasted_iota(jnp.int32, (GRID_W, LANES), 1) & (GRID_W - 1)
    cs = jnp.clip(wq - NA_COLS // 2, 0, GRID_W - NA_COLS)
    col_ok = (wk >= cs) & (wk < cs + NA_COLS)
    dc = wk - wq + (NA_COLS - 1)
    n_dr, n_dc = 2 * NA_ROWS - 1, 2 * NA_COLS - 1
    neg = jnp.full((GRID_W, LANES), NEG_INF, F32)
    for dr in range(n_dr):
        val = neg
        for c in range(n_dc):
            val = jnp.where(col_ok & (dc == c), rpb_ref[h * (n_dr * n_dc) + dr * n_dc + c], val)
        t2_ref[dr] = val
    lane = lax.broadcasted_iota(jnp.int32, (GRID_W, LANES), 1)
    left = lane < GRID_W
    for ty, (r0, ws) in enumerate(_na_tile_geometry(rows_total)):
        for i in range(NA_TILE_ROWS):
            r = r0 + i
            rs = min(max(r - NA_ROWS // 2, 0), rows_total - NA_ROWS)
            for jp in range(NA_WIN_ROWS // 2):
                halves = []
                for kr in (ws + 2 * jp, ws + 2 * jp + 1):
                    halves.append(t2_ref[kr - r + NA_ROWS - 1] if rs <= kr < rs + NA_ROWS else neg)
                o_ref[0, ty, i * GRID_W:(i + 1) * GRID_W, jp * LANES:(jp + 1) * LANES] = jnp.where(left, halves[0], halves[1])


def _na_bias_tables(rpb, rows_total):
    h = rpb.shape[0]
    q_rows, k_cols = NA_TILE_ROWS * GRID_W, NA_WIN_ROWS * GRID_W
    return pl.pallas_call(
        functools.partial(_na_bias_body, rows_total),
        grid=(h,),
        in_specs=[pl.BlockSpec(memory_space=pltpu.SMEM)],
        out_specs=pl.BlockSpec((1, 3, q_rows, k_cols), lambda hi: (hi, 0, 0, 0)),
        out_shape=jax.ShapeDtypeStruct((h, 3, q_rows, k_cols), F32),
        scratch_shapes=[pltpu.VMEM((2 * NA_ROWS - 1, GRID_W, LANES), F32)],
        compiler_params=_params(("arbitrary",)),
        name="na_bias_tables",
    )(rpb.reshape(-1))


def _na_body(rows_total, q_ref, k_ref, v_ref, kc_ref, vc_ref, bias_ref, o_ref):
    t = pl.program_id(2)
    n_t = pl.num_programs(2)
    ty = jnp.where(t == 0, 0, jnp.where(t == n_t - 1, 2, 1))
    ws = jnp.clip(t * NA_TILE_ROWS - NA_ROWS // 2, 0, rows_total - NA_WIN_ROWS)
    off = pl.multiple_of(ws * GRID_W, GRID_W)
    n_win = NA_WIN_ROWS * GRID_W
    rows = q_ref.shape[1]
    q2 = _split_heads(q_ref[0])
    bias = jnp.concatenate([bias_ref[0, ty], bias_ref[1, ty]], axis=0)
    s_win = _dot_nt(q2, k_ref[0, pl.ds(off, n_win), :]) + bias
    s_ctx = _dot_nt(q2, kc_ref[0])
    o2 = _softmax_pv([s_win, s_ctx], [v_ref[0, pl.ds(off, n_win), :], vc_ref[0]])
    o_ref[0] = _merge_heads(o2, rows).astype(o_ref.dtype)


def _neighborhood_attention(q, k, v, kc, vc, bias):
    b, s, w = q.shape
    l = kc.shape[1]
    rows_total = s // GRID_W
    tq = NA_TILE_ROWS * GRID_W
    full = lambda j, bi, t: (bi, 0, j)
    return pl.pallas_call(
        functools.partial(_na_body, rows_total),
        grid=(w // LANES, b, s // tq),
        in_specs=[pl.BlockSpec((1, tq, LANES), lambda j, bi, t: (bi, t, j)),
                  pl.BlockSpec((1, s, LANES), full),
                  pl.BlockSpec((1, s, LANES), full),
                  pl.BlockSpec((1, l, LANES), full),
                  pl.BlockSpec((1, l, LANES), full),
                  pl.BlockSpec((2,) + bias.shape[1:], lambda j, bi, t: (j, 0, 0, 0))],
        out_specs=pl.BlockSpec((1, tq, LANES), lambda j, bi, t: (bi, t, j)),
        out_shape=jax.ShapeDtypeStruct((b, s, w), BF16),
        compiler_params=_params(("arbitrary", "arbitrary", "arbitrary")),
        name="neighborhood_attention",
    )(q, k, v, kc, vc, bias)


def _modulation(m, rows, batch):
    d = m.shape[1] // 3
    pick = lambda lo: jnp.broadcast_to(m[rows, lo:lo + d].reshape(-1, 1, d), (batch, 1, d))
    return pick(0), 1.0 + pick(d), pick(2 * d)


def kernel(x, c, ctx, c_ctx, ab_w_ada, ab_b_ada, ab_norm_g, ab_w_in, ab_conv_w, ab_conv_b, ab_ln_g, ab_ln_b, ab_q_norm_g, ab_k_norm_g, ab_w_out, cd_w_ada, cd_b_ada, cd_norm_g, cd_w_in, cd_w_fourier, cd_rpb, cd_w_out, final_norm_g):
    b, s, d = x.shape
    l = ctx.shape[1]
    assert d == D_MODEL and s % (NA_TILE_ROWS * GRID_W) == 0 and ab_w_ada.shape[0] == 1 and cd_w_ada.shape[0] == 1

    cond = jnp.zeros((16, d), F32).at[:b].set(c).at[b].set(c_ctx)
    lat = slice(0, b)
    cx = slice(b, b + 1)

    m0 = _adaln(cond, ab_w_ada[0], ab_b_ada[0])
    sh, s1p, gate = _modulation(m0, lat, b)
    csh, cs1p, cgate = _modulation(m0, cx, b)

    order = np.asarray(GQA_HEAD_ORDER)
    head_cols = (order[:, None] * HEAD_DIM + np.arange(HEAD_DIM)[None, :]).reshape(-1)
    cols = np.concatenate([np.arange(1024), 1024 + head_cols, np.arange(1536, 1792), np.arange(1792, 2304),
                           2304 + head_cols])
    w_in0 = jnp.take(ab_w_in[0], cols, axis=1).astype(BF16)
    w_out0 = jnp.take(ab_w_out[0], np.concatenate([np.arange(512), 512 + head_cols]), axis=0).astype(BF16)

    cos, sin = _rope_tables(s)
    cos, sin = jnp.asarray(cos), jnp.asarray(sin)
    mq = jnp.asarray(_block_ones(512)).astype(BF16)
    mk = jnp.asarray(_block_ones(128)).astype(BF16)
    qg = jnp.tile(ab_q_norm_g[0], 512 // HEAD_DIM).reshape(1, 512)
    kg = jnp.tile(ab_k_norm_g[0], 128 // HEAD_DIM).reshape(1, 128)
    ng0 = ab_norm_g[0].reshape(1, d)

    glu, q, k, v, sz = _inproj_ab(x, ng0, s1p, sh, w_in0, qg, kg, cos, sin, mq, mk, True, 512)
    cglu, cq, ck, cv, csz = _inproj_ab(ctx, ng0, cs1p, csh, w_in0, qg, kg, cos[:l], sin[:l], mq, mk, False, l)

    y_b = _gqa_attention(q, [ck, k], [cv, v], 256)
    y_a = _conformer_conv(glu, ab_conv_w[0], ab_conv_b[0], ab_ln_g[0], ab_ln_b[0], 256)
    x1 = _outproj(y_a, y_b, sz, w_out0, x, gate, None, 512)

    yc_b = _gqa_attention(cq, [ck], [cv], l)
    yc_a = _conformer_conv(cglu, ab_conv_w[0], ab_conv_b[0], ab_ln_g[0], ab_ln_b[0], l)
    xc1 = _outproj(yc_a, yc_b, csz, w_out0, ctx, cgate, None, l)

    m1 = _adaln(cond, cd_w_ada[0], cd_b_ada[0])
    sh, s1p, gate = _modulation(m1, lat, b)
    csh, cs1p, _ = _modulation(m1, cx, b)
    ng1 = cd_norm_g[0].reshape(1, d)
    w_in1 = cd_w_in[0].astype(BF16)

    f, q, k, v, sz = _inproj_cd(x1, ng1, s1p, sh, w_in1, 512)
    kc, vc = _inproj_kv(xc1, ng1, cs1p, csh, w_in1[:, 1024:2048])

    c_l, s_l = _dft_tables(s)
    cs_tab = jnp.asarray(np.concatenate([c_l, -s_l], axis=1)).astype(BF16)
    c_c, s_c = _dft_tables(cd_w_fourier.shape[-1])
    w1 = _fourier_weights(jnp.asarray(c_c), jnp.asarray(s_c), cd_w_fourier[0])
    y_c = _fourier_mix(f, w1, cs_tab, 512)

    bias = _na_bias_tables(cd_rpb[0], s // GRID_W)
    y_d = _neighborhood_attention(q, k, v, kc, vc, bias)

    return _outproj(y_c, y_d, sz, cd_w_out[0].astype(BF16), x1, gate, final_norm_g.reshape(1, d), 512)
```

```python
import functools
import math

import numpy as np
import jax
import jax.numpy as jnp
from jax import lax
from jax.experimental import pallas as pl
from jax.experimental.pallas import tpu as pltpu

F32 = jnp.float32
BF16 = jnp.bfloat16
HIGHEST = lax.Precision.HIGHEST

D_MODEL = 1024
HEAD_DIM = 64
GRID_W = 64
CONV_WIDTH = 31
NA_ROWS = 8
NA_COLS = 16
ROPE_THETA = 10000.0
EPS = 1e-6
NEG_INF = -1e30
LOG2E = math.log2(math.e)
Q_SCALE = HEAD_DIM ** -0.5 * LOG2E
LANES = 128
SUBLANES = 8
HALO = 16
VMEM_LIMIT = 48 * 1024 * 1024

NA_TILE_ROWS = 4
NA_WIN_ROWS = NA_TILE_ROWS + NA_ROWS
NA_BATCH_PER_STEP = 4

GQA_HEAD_ORDER = (0, 4, 1, 5, 2, 6, 3, 7)


def _params(sem):
    return pltpu.CompilerParams(dimension_semantics=sem, vmem_limit_bytes=VMEM_LIMIT)


def _silu(t):
    return t * jax.nn.sigmoid(t)


def _dot(a, b):
    return jnp.dot(a, b, preferred_element_type=F32)


def _dot_nt(a, b):
    return lax.dot_general(a, b, (((1,), (1,)), ((), ())), preferred_element_type=F32)


def _rope_tables(seq):
    t = np.arange(seq)
    row = (t // GRID_W).astype(np.float64)
    col = (t % GRID_W).astype(np.float64)
    nf = HEAD_DIM // 4
    inv = np.float32(ROPE_THETA) ** (-np.arange(nf, dtype=np.float32) / nf)
    ang = np.concatenate([row[:, None] * inv, col[:, None] * inv], axis=-1)
    cos = np.repeat(np.cos(ang), 2, axis=-1)
    sin = np.repeat(np.sin(ang), 2, axis=-1)
    sign = np.tile(np.array([-1.0, 1.0]), HEAD_DIM // 2)
    cos = np.tile(cos, (1, LANES // HEAD_DIM))
    sin = np.tile(sin * sign, (1, LANES // HEAD_DIM))
    return cos.astype(np.float32), sin.astype(np.float32)


def _dft_tables(n):
    k = np.arange(n)
    ang = 2.0 * np.pi * ((k[:, None] * k[None, :]) % n) / n
    s = 1.0 / math.sqrt(n)
    return (np.cos(ang) * s).astype(np.float32), (np.sin(ang) * s).astype(np.float32)


def _block_ones(width):
    h = np.arange(width) // HEAD_DIM
    return (h[:, None] == h[None, :]).astype(np.float32)


def _adaln_body(c_ref, w_ref, b_ref, o_ref):
    a = _silu(c_ref[...])
    o_ref[...] = jnp.dot(a, w_ref[...], preferred_element_type=F32, precision=HIGHEST) + b_ref[...]


def _adaln(cond, w, b):
    r, d = cond.shape
    n = w.shape[1]
    tn = 1024
    return pl.pallas_call(
        _adaln_body,
        grid=(n // tn,),
        in_specs=[pl.BlockSpec((r, d), lambda j: (0, 0)),
                  pl.BlockSpec((d, tn), lambda j: (0, j)),
                  pl.BlockSpec((1, tn), lambda j: (0, j))],
        out_specs=pl.BlockSpec((r, tn), lambda j: (0, j)),
        out_shape=jax.ShapeDtypeStruct((r, n), F32),
        compiler_params=_params(("arbitrary",)),
        name="adaln",
    )(cond, w, b.reshape(1, n))


def _head_rms(t, ones_ref, gain):
    t2 = t * t
    hi = t2.astype(BF16)
    lo = (t2 - hi.astype(F32)).astype(BF16)
    ms = (_dot(hi, ones_ref[...]) + _dot(lo, ones_ref[...])) * (1.0 / HEAD_DIM)
    return t * lax.rsqrt(ms + EPS) * gain


def _rope(t, cos, sin_signed):
    rows = t.shape[0]
    lane = lax.broadcasted_iota(jnp.int32, (rows, LANES), 1)
    even = (lane & 1) == 0
    outs = []
    for c in range(t.shape[1] // LANES):
        xc = t[:, c * LANES:(c + 1) * LANES]
        swapped = jnp.where(even, pltpu.roll(xc, LANES - 1, 1), pltpu.roll(xc, 1, 1))
        outs.append(xc * cos + swapped * sin_signed)
    return outs[0] if len(outs) == 1 else jnp.concatenate(outs, axis=1)


def _modulated_norm(x_ref, g_ref, s1p_ref, sh_ref):
    x = x_ref[0]
    y = x * lax.rsqrt(jnp.mean(x * x, axis=-1, keepdims=True) + EPS) * g_ref[...]
    return (y * s1p_ref[0] + sh_ref[0]).astype(BF16)


def _inproj_ab_body(rope, x_ref, g_ref, s1p_ref, sh_ref, w_ref, qg_ref, kg_ref, cos_ref, sin_ref, mq_ref, mk_ref,
                    glu_ref, q_ref, k_ref, v_ref, sz_ref):
    u = _dot(_modulated_norm(x_ref, g_ref, s1p_ref, sh_ref), w_ref[...])
    glu_ref[0] = u[:, 0:512] * jax.nn.sigmoid(u[:, 512:1024])
    q = _head_rms(u[:, 1024:1536], mq_ref, qg_ref[...])
    k = _head_rms(u[:, 1536:1664], mk_ref, kg_ref[...])
    if rope:
        q = _rope(q, cos_ref[...], sin_ref[...])
        k = _rope(k, cos_ref[...], sin_ref[...])
    q_ref[0] = (q * Q_SCALE).astype(BF16)
    k_ref[0] = k.astype(BF16)
    v_ref[0] = _with_ones(u[:, 1664:1792]).astype(BF16)
    sz_ref[0] = _silu(u[:, 1792:2816]).astype(BF16)


def _inproj_ab(x, g, s1p, sh, w, qg, kg, cos, sin, mq, mk, rope, tm):
    b, s, d = x.shape
    n = w.shape[1]
    row = lambda bi, i: (bi, i, 0)
    const = lambda bi, i: (0, 0)
    per_b = lambda bi, i: (bi, 0, 0)
    tab = lambda bi, i: (i, 0)
    return pl.pallas_call(
        functools.partial(_inproj_ab_body, rope),
        grid=(b, s // tm),
        in_specs=[pl.BlockSpec((1, tm, d), row),
                  pl.BlockSpec((1, d), const),
                  pl.BlockSpec((1, 1, d), per_b),
                  pl.BlockSpec((1, 1, d), per_b),
                  pl.BlockSpec((d, n), const),
                  pl.BlockSpec((1, 512), const),
                  pl.BlockSpec((1, 128), const),
                  pl.BlockSpec((tm, LANES), tab),
                  pl.BlockSpec((tm, LANES), tab),
                  pl.BlockSpec((512, 512), const),
                  pl.BlockSpec((128, 128), const)],
        out_specs=[pl.BlockSpec((1, tm, 512), row),
                   pl.BlockSpec((1, tm, 512), row),
                   pl.BlockSpec((1, tm, 128), row),
                   pl.BlockSpec((1, tm, 256), row),
                   pl.BlockSpec((1, tm, 1024), row)],
        out_shape=[jax.ShapeDtypeStruct((b, s, 512), F32),
                   jax.ShapeDtypeStruct((b, s, 512), BF16),
                   jax.ShapeDtypeStruct((b, s, 128), BF16),
                   jax.ShapeDtypeStruct((b, s, 256), BF16),
                   jax.ShapeDtypeStruct((b, s, 1024), BF16)],
        compiler_params=_params(("arbitrary", "arbitrary")),
        name="inproj_ab_rope" if rope else "inproj_ab_ctx",
    )(x, g, s1p, sh, w, qg, kg, cos, sin, mq, mk)


def _inproj_cd_body(x_ref, g_ref, s1p_ref, sh_ref, w_ref, f_ref, q_ref, k_ref, v_ref, sz_ref):
    u = _dot(_modulated_norm(x_ref, g_ref, s1p_ref, sh_ref), w_ref[...])
    f_ref[0] = u[:, 0:512].astype(BF16)
    q_ref[0] = (u[:, 512:1024] * Q_SCALE).astype(BF16)
    k_ref[0] = u[:, 1024:1536].astype(BF16)
    v_ref[0] = _pairs_with_ones(u[:, 1536:2048]).astype(BF16)
    sz_ref[0] = _silu(u[:, 2048:3072]).astype(BF16)


def _inproj_cd(x, g, s1p, sh, w, tm):
    b, s, d = x.shape
    n = w.shape[1]
    row = lambda bi, i: (bi, i, 0)
    const = lambda bi, i: (0, 0)
    per_b = lambda bi, i: (bi, 0, 0)
    return pl.pallas_call(
        _inproj_cd_body,
        grid=(b, s // tm),
        in_specs=[pl.BlockSpec((1, tm, d), row),
                  pl.BlockSpec((1, d), const),
                  pl.BlockSpec((1, 1, d), per_b),
                  pl.BlockSpec((1, 1, d), per_b),
                  pl.BlockSpec((d, n), const)],
        out_specs=[pl.BlockSpec((1, tm, 512), row)] * 3 + [pl.BlockSpec((1, tm, 1024), row)] * 2,
        out_shape=[jax.ShapeDtypeStruct((b, s, 512), BF16)] * 3 + [jax.ShapeDtypeStruct((b, s, 1024), BF16)] * 2,
        compiler_params=_params(("arbitrary", "arbitrary")),
        name="inproj_cd",
    )(x, g, s1p, sh, w)


def _inproj_kv_body(x_ref, g_ref, s1p_ref, sh_ref, w_ref, k_ref, v_ref):
    u = _dot(_modulated_norm(x_ref, g_ref, s1p_ref, sh_ref), w_ref[...])
    k_ref[0] = u[:, 0:512].astype(BF16)
    v_ref[0] = _pairs_with_ones(u[:, 512:1024]).astype(BF16)


def _inproj_kv(x, g, s1p, sh, w):
    b, s, d = x.shape
    n = w.shape[1]
    row = lambda bi: (bi, 0, 0)
    const = lambda bi: (0, 0)
    return pl.pallas_call(
        _inproj_kv_body,
        grid=(b,),
        in_specs=[pl.BlockSpec((1, s, d), row),
                  pl.BlockSpec((1, d), const),
                  pl.BlockSpec((1, 1, d), row),
                  pl.BlockSpec((1, 1, d), row),
                  pl.BlockSpec((d, n), const)],
        out_specs=[pl.BlockSpec((1, s, 512), row), pl.BlockSpec((1, s, 1024), row)],
        out_shape=[jax.ShapeDtypeStruct((b, s, 512), BF16), jax.ShapeDtypeStruct((b, s, 1024), BF16)],
        compiler_params=_params(("arbitrary",)),
        name="inproj_kv_ctx",
    )(x, g, s1p, sh, w)


def _split_heads(q):
    lane = lax.broadcasted_iota(jnp.int32, q.shape, 1)
    zero = jnp.zeros_like(q)
    return jnp.concatenate([jnp.where(lane < HEAD_DIM, q, zero), jnp.where(lane >= HEAD_DIM, q, zero)], axis=0)


def _with_ones(v):
    lane = lax.broadcasted_iota(jnp.int32, v.shape, 1)
    one = jnp.ones_like(v)
    return jnp.concatenate([jnp.where(lane < HEAD_DIM, v, one), jnp.where(lane < HEAD_DIM, one, v)], axis=1)


def _pairs_with_ones(v):
    return jnp.concatenate([_with_ones(v[:, c:c + LANES]) for c in range(0, v.shape[1], LANES)], axis=1)


def _online_softmax_pv(q2, chunks):
    rows = q2.shape[0] // 2
    m = acc = None
    for k, v2, bias in chunks:
        s = _dot_nt(q2, k)
        if bias is not None:
            s = s + bias
        mc = jnp.max(s, axis=-1, keepdims=True)
        m_new = mc if m is None else jnp.maximum(m, mc)
        p = jnp.exp2(s - m_new).astype(BF16)
        pv = jnp.concatenate([_dot(p[:rows], v2[:, :LANES]), _dot(p[rows:], v2[:, LANES:])], axis=0)
        acc = pv if m is None else jnp.exp2(m - m_new) * acc + pv
        m = m_new
    left = lax.broadcasted_iota(jnp.int32, (rows, LANES), 1) < HEAD_DIM
    top, bot = acc[:rows], acc[rows:]
    return jnp.where(left, top, bot) / pltpu.roll(jnp.where(left, bot, top), HEAD_DIM, 1)


def _key_chunks(k_ref, v_ref, chunk):
    n = k_ref.shape[1]
    return [(k_ref[0, c:c + chunk, :], v_ref[0, c:c + chunk, :], None) for c in range(0, n, chunk)]


def _gqa_body(n_src, chunk, q_ref, *refs):
    k_refs, v_refs, o_ref = refs[:n_src], refs[n_src:2 * n_src], refs[2 * n_src]
    q2 = _split_heads(q_ref[0])
    chunks = [c for k, v in zip(k_refs, v_refs) for c in _key_chunks(k, v, min(chunk, k.shape[1]))]
    o_ref[0] = _online_softmax_pv(q2, chunks).astype(o_ref.dtype)


def _gqa_attention(q, ks, vs, tq, chunk=512):
    b, lq, w = q.shape
    n_src = len(ks)
    k_specs = [pl.BlockSpec((1, k.shape[1], LANES), lambda bi, i, j: (bi, 0, 0)) for k in ks]
    v_specs = [pl.BlockSpec((1, v.shape[1], 2 * LANES), lambda bi, i, j: (bi, 0, 0)) for v in vs]
    return pl.pallas_call(
        functools.partial(_gqa_body, n_src, chunk),
        grid=(b, lq // tq, w // LANES),
        in_specs=[pl.BlockSpec((1, tq, LANES), lambda bi, i, j: (bi, i, j))] + k_specs + v_specs,
        out_specs=pl.BlockSpec((1, tq, LANES), lambda bi, i, j: (bi, i, j)),
        out_shape=jax.ShapeDtypeStruct((b, lq, w), BF16),
        compiler_params=_params(("arbitrary", "arbitrary", "arbitrary")),
        name=f"gqa_attention_{n_src}src",
    )(q, *ks, *vs)


def _conv_body(rows, x_ref, prev_ref, next_ref, w_ref, cb_ref, lg_ref, lb_ref, o_ref, pad_ref, shift_ref):
    i = pl.program_id(1)
    last = pl.num_programs(1) - 1
    pad_ref[0:HALO, :] = jnp.where(i > 0, prev_ref[0], 0.0)
    pad_ref[HALO:HALO + rows, :] = x_ref[0]
    pad_ref[HALO + rows:HALO + rows + HALO, :] = jnp.where(i < last, next_ref[0], 0.0)
    span = shift_ref.shape[1]
    for r in range(1, SUBLANES):
        shift_ref[r] = pad_ref[r:r + span, :]
    sub = 32
    base = HALO - CONV_WIDTH // 2
    for r0 in range(0, rows, sub):
        acc = jnp.zeros((sub, x_ref.shape[2]), F32) + cb_ref[...]
        for k in range(CONV_WIDTH):
            a, r = divmod(base + k, SUBLANES)
            lo = a * SUBLANES + r0
            tap = pad_ref[lo:lo + sub, :] if r == 0 else shift_ref[r, lo:lo + sub, :]
            acc = acc + tap * jnp.concatenate([w_ref[k]] * (sub // SUBLANES), axis=0)
        mu = jnp.mean(acc, axis=-1, keepdims=True)
        xc = acc - mu
        y = xc * lax.rsqrt(jnp.mean(xc * xc, axis=-1, keepdims=True) + EPS) * lg_ref[...] + lb_ref[...]
        o_ref[0, r0:r0 + sub, :] = _silu(y).astype(o_ref.dtype)


def _conformer_conv(u, conv_w, conv_b, ln_g, ln_b, rows):
    b, l, c = u.shape
    hb = rows // HALO
    n_halo = l // HALO
    return pl.pallas_call(
        functools.partial(_conv_body, rows),
        grid=(b, l // rows),
        in_specs=[pl.BlockSpec((1, rows, c), lambda bi, i: (bi, i, 0)),
                  pl.BlockSpec((1, HALO, c), lambda bi, i: (bi, jnp.maximum(i * hb - 1, 0), 0)),
                  pl.BlockSpec((1, HALO, c), lambda bi, i: (bi, jnp.minimum((i + 1) * hb, n_halo - 1), 0)),
                  pl.BlockSpec((CONV_WIDTH, SUBLANES, c), lambda bi, i: (0, 0, 0)),
                  pl.BlockSpec((1, c), lambda bi, i: (0, 0)),
                  pl.BlockSpec((1, c), lambda bi, i: (0, 0)),
                  pl.BlockSpec((1, c), lambda bi, i: (0, 0))],
        out_specs=pl.BlockSpec((1, rows, c), lambda bi, i: (bi, i, 0)),
        out_shape=jax.ShapeDtypeStruct((b, l, c), BF16),
        scratch_shapes=[pltpu.VMEM((rows + 2 * HALO, c), F32),
                        pltpu.VMEM((SUBLANES, rows + 2 * HALO - SUBLANES, c), F32)],
        compiler_params=_params(("arbitrary", "arbitrary")),
        name="conformer_conv",
    )(u, u, u, jnp.broadcast_to(conv_w[:, None, :], (CONV_WIDTH, SUBLANES, c)),
      conv_b.reshape(1, c), ln_g.reshape(1, c), ln_b.reshape(1, c))


def _outproj_body(final, ya_ref, yb_ref, sz_ref, w_ref, x_ref, gate_ref, *rest):
    y = jnp.concatenate([ya_ref[0], yb_ref[0]], axis=-1).astype(F32) * sz_ref[0].astype(F32)
    xn = x_ref[0] + gate_ref[0] * _dot(y.astype(BF16), w_ref[...])
    if final:
        fg_ref, o_ref = rest
        o_ref[0] = xn * lax.rsqrt(jnp.mean(xn * xn, axis=-1, keepdims=True) + EPS) * fg_ref[...]
    else:
        rest[0][0] = xn


def _outproj(ya, yb, sz, w, x, gate, final_g, tm):
    b, s, d = x.shape
    row = lambda bi, i: (bi, i, 0)
    const = lambda bi, i: (0, 0)
    per_b = lambda bi, i: (bi, 0, 0)
    in_specs = [pl.BlockSpec((1, tm, 512), row),
                pl.BlockSpec((1, tm, 512), row),
                pl.BlockSpec((1, tm, 1024), row),
                pl.BlockSpec((1024, d), const),
                pl.BlockSpec((1, tm, d), row),
                pl.BlockSpec((1, 1, d), per_b)]
    args = [ya, yb, sz, w, x, gate]
    if final_g is not None:
        in_specs.append(pl.BlockSpec((1, d), const))
        args.append(final_g)
    return pl.pallas_call(
        functools.partial(_outproj_body, final_g is not None),
        grid=(b, s // tm),
        in_specs=in_specs,
        out_specs=pl.BlockSpec((1, tm, d), row),
        out_shape=jax.ShapeDtypeStruct((b, s, d), F32),
        compiler_params=_params(("arbitrary", "arbitrary")),
        name="outproj_final" if final_g is not None else "outproj",
    )(*args)


def _fourier_w_body(cc_ref, sc_ref, w_ref, o_ref):
    for g in range(w_ref.shape[0]):
        wc = jnp.dot(cc_ref[...], w_ref[g], preferred_element_type=F32, precision=HIGHEST)
        ws = jnp.dot(sc_ref[...], w_ref[g], preferred_element_type=F32, precision=HIGHEST)
        o_ref[g] = jnp.concatenate([wc, ws], axis=-1).astype(o_ref.dtype)


def _fourier_weights(cc, sc, w):
    g, c, _ = w.shape
    return pl.pallas_call(
        _fourier_w_body,
        out_shape=jax.ShapeDtypeStruct((g, c, 2 * c), BF16),
        name="fourier_weights",
    )(cc, sc, w)


def _fourier_body(f_ref, w1_ref, cs_ref, o_ref, y_ref):
    groups, c, _ = w1_ref.shape
    seq = f_ref.shape[1]

    @pl.when(pl.program_id(1) == 0)
    def _():
        for g in range(groups):
            yg = _dot(f_ref[0, :, g * c:(g + 1) * c], w1_ref[g])
            y_ref[0:seq, g * c:(g + 1) * c] = yg[:, :c].astype(BF16)
            y_ref[seq:2 * seq, g * c:(g + 1) * c] = yg[:, c:].astype(BF16)

    o_ref[0] = _dot(cs_ref[...], y_ref[...]).astype(o_ref.dtype)


def _fourier_mix(f, w1, cs, tl):
    b, s, w = f.shape
    return pl.pallas_call(
        _fourier_body,
        grid=(b, s // tl),
        in_specs=[pl.BlockSpec((1, s, w), lambda bi, i: (bi, 0, 0)),
                  pl.BlockSpec(w1.shape, lambda bi, i: (0, 0, 0)),
                  pl.BlockSpec((tl, 2 * s), lambda bi, i: (i, 0))],
        out_specs=pl.BlockSpec((1, tl, w), lambda bi, i: (bi, i, 0)),
        out_shape=jax.ShapeDtypeStruct((b, s, w), BF16),
        scratch_shapes=[pltpu.VMEM((2 * s, w), BF16)],
        compiler_params=_params(("arbitrary", "arbitrary")),
        name="fourier_mix",
    )(f, w1, cs)


def _na_tile_geometry(rows_total):
    last_r0 = rows_total - NA_TILE_ROWS
    return ((0, 0), (2 * NA_TILE_ROWS, 2 * NA_TILE_ROWS - NA_ROWS // 2), (last_r0, rows_total - NA_WIN_ROWS))


def _na_bias_body(rows_total, rpb_ref, o_ref, t2_ref):
    h = pl.program_id(0)
    wq = lax.broadcasted_iota(jnp.int32, (GRID_W, LANES), 0)
    wk = lax.broadcasted_iota(jnp.int32, (GRID_W, LANES), 1) & (GRID_W - 1)
    cs = jnp.clip(wq - NA_COLS // 2, 0, GRID_W - NA_COLS)
    col_ok = (wk >= cs) & (wk < cs + NA_COLS)
    dc = wk - wq + (NA_COLS - 1)
    n_dr, n_dc = 2 * NA_ROWS - 1, 2 * NA_COLS - 1
    neg = jnp.full((GRID_W, LANES), NEG_INF, F32)
    for dr in range(n_dr):
        val = neg
        for c in range(n_dc):
            val = jnp.where(col_ok & (dc == c), rpb_ref[h * (n_dr * n_dc) + dr * n_dc + c] * LOG2E, val)
        t2_ref[dr] = val
    lane = lax.broadcasted_iota(jnp.int32, (GRID_W, LANES), 1)
    left = lane < GRID_W
    for ty, (r0, ws) in enumerate(_na_tile_geometry(rows_total)):
        for i in range(NA_TILE_ROWS):
            r = r0 + i
            rs = min(max(r - NA_ROWS // 2, 0), rows_total - NA_ROWS)
            for jp in range(NA_WIN_ROWS // 2):
                halves = []
                for kr in (ws + 2 * jp, ws + 2 * jp + 1):
                    halves.append(t2_ref[kr - r + NA_ROWS - 1] if rs <= kr < rs + NA_ROWS else neg)
                o_ref[0, ty, i * GRID_W:(i + 1) * GRID_W, jp * LANES:(jp + 1) * LANES] = jnp.where(left, halves[0], halves[1])


def _na_bias_tables(rpb, rows_total):
    h = rpb.shape[0]
    q_rows, k_cols = NA_TILE_ROWS * GRID_W, NA_WIN_ROWS * GRID_W
    return pl.pallas_call(
        functools.partial(_na_bias_body, rows_total),
        grid=(h,),
        in_specs=[pl.BlockSpec(memory_space=pltpu.SMEM)],
        out_specs=pl.BlockSpec((1, 3, q_rows, k_cols), lambda hi: (hi, 0, 0, 0)),
        out_shape=jax.ShapeDtypeStruct((h, 3, q_rows, k_cols), F32),
        scratch_shapes=[pltpu.VMEM((2 * NA_ROWS - 1, GRID_W, LANES), F32)],
        compiler_params=_params(("arbitrary",)),
        name="na_bias_tables",
    )(rpb.reshape(-1))


def _na_body(rows_total, q_ref, k_ref, v_ref, kc_ref, vc_ref, bias_ref, o_ref):
    t = pl.program_id(2)
    n_t = pl.num_programs(2)
    ty = jnp.where(t == 0, 0, jnp.where(t == n_t - 1, 2, 1))
    ws = jnp.clip(t * NA_TILE_ROWS - NA_ROWS // 2, 0, rows_total - NA_WIN_ROWS)
    off = pl.multiple_of(ws * GRID_W, GRID_W)
    n_win = NA_WIN_ROWS * GRID_W
    bias = jnp.concatenate([bias_ref[0, ty], bias_ref[1, ty]], axis=0)
    for bb in range(q_ref.shape[0]):
        q2 = _split_heads(q_ref[bb])
        o = _online_softmax_pv(q2, [(kc_ref[bb], vc_ref[bb], None),
                                    (k_ref[bb, pl.ds(off, n_win), :], v_ref[bb, pl.ds(off, n_win), :], bias)])
        o_ref[bb] = o.astype(o_ref.dtype)


def _neighborhood_attention(q, k, v, kc, vc, bias):
    b, s, w = q.shape
    l = kc.shape[1]
    rows_total = s // GRID_W
    tq = NA_TILE_ROWS * GRID_W
    nb = NA_BATCH_PER_STEP
    full = lambda j, bi, t: (bi, 0, j)
    return pl.pallas_call(
        functools.partial(_na_body, rows_total),
        grid=(w // LANES, b // nb, s // tq),
        in_specs=[pl.BlockSpec((nb, tq, LANES), lambda j, bi, t: (bi, t, j)),
                  pl.BlockSpec((nb, s, LANES), full),
                  pl.BlockSpec((nb, s, 2 * LANES), full),
                  pl.BlockSpec((nb, l, LANES), full),
                  pl.BlockSpec((nb, l, 2 * LANES), full),
                  pl.BlockSpec((2,) + bias.shape[1:], lambda j, bi, t: (j, 0, 0, 0))],
        out_specs=pl.BlockSpec((nb, tq, LANES), lambda j, bi, t: (bi, t, j)),
        out_shape=jax.ShapeDtypeStruct((b, s, w), BF16),
        compiler_params=_params(("arbitrary", "arbitrary", "arbitrary")),
        name="neighborhood_attention",
    )(q, k, v, kc, vc, bias)


def _modulation(m, rows, batch):
    d = m.shape[1] // 3
    pick = lambda lo: jnp.broadcast_to(m[rows, lo:lo + d].reshape(-1, 1, d), (batch, 1, d))
    return pick(0), 1.0 + pick(d), pick(2 * d)


def kernel(x, c, ctx, c_ctx, ab_w_ada, ab_b_ada, ab_norm_g, ab_w_in, ab_conv_w, ab_conv_b, ab_ln_g, ab_ln_b, ab_q_norm_g, ab_k_norm_g, ab_w_out, cd_w_ada, cd_b_ada, cd_norm_g, cd_w_in, cd_w_fourier, cd_rpb, cd_w_out, final_norm_g):
    b, s, d = x.shape
    l = ctx.shape[1]
    assert d == D_MODEL and s % (NA_TILE_ROWS * GRID_W) == 0 and ab_w_ada.shape[0] == 1 and cd_w_ada.shape[0] == 1

    cond = jnp.zeros((16, d), F32).at[:b].set(c).at[b].set(c_ctx)
    lat = slice(0, b)
    cx = slice(b, b + 1)

    m0 = _adaln(cond, ab_w_ada[0], ab_b_ada[0])
    sh, s1p, gate = _modulation(m0, lat, b)
    csh, cs1p, cgate = _modulation(m0, cx, b)

    order = np.asarray(GQA_HEAD_ORDER)
    head_cols = (order[:, None] * HEAD_DIM + np.arange(HEAD_DIM)[None, :]).reshape(-1)
    cols = np.concatenate([np.arange(1024), 1024 + head_cols, np.arange(1536, 1792), np.arange(1792, 2304),
                           2304 + head_cols])
    w_in0 = jnp.take(ab_w_in[0], cols, axis=1).astype(BF16)
    w_out0 = jnp.take(ab_w_out[0], np.concatenate([np.arange(512), 512 + head_cols]), axis=0).astype(BF16)

    cos, sin = _rope_tables(s)
    cos, sin = jnp.asarray(cos), jnp.asarray(sin)
    mq = jnp.asarray(_block_ones(512)).astype(BF16)
    mk = jnp.asarray(_block_ones(128)).astype(BF16)
    qg = jnp.tile(ab_q_norm_g[0], 512 // HEAD_DIM).reshape(1, 512)
    kg = jnp.tile(ab_k_norm_g[0], 128 // HEAD_DIM).reshape(1, 128)
    ng0 = ab_norm_g[0].reshape(1, d)

    glu, q, k, v, sz = _inproj_ab(x, ng0, s1p, sh, w_in0, qg, kg, cos, sin, mq, mk, True, 512)
    cglu, cq, ck, cv, csz = _inproj_ab(ctx, ng0, cs1p, csh, w_in0, qg, kg, cos[:l], sin[:l], mq, mk, False, l)

    y_b = _gqa_attention(q, [ck, k], [cv, v], 1024)
    y_a = _conformer_conv(glu, ab_conv_w[0], ab_conv_b[0], ab_ln_g[0], ab_ln_b[0], 256)
    x1 = _outproj(y_a, y_b, sz, w_out0, x, gate, None, 512)

    yc_b = _gqa_attention(cq, [ck], [cv], l)
    yc_a = _conformer_conv(cglu, ab_conv_w[0], ab_conv_b[0], ab_ln_g[0], ab_ln_b[0], l)
    xc1 = _outproj(yc_a, yc_b, csz, w_out0, ctx, cgate, None, l)

    m1 = _adaln(cond, cd_w_ada[0], cd_b_ada[0])
    sh, s1p, gate = _modulation(m1, lat, b)
    csh, cs1p, _ = _modulation(m1, cx, b)
    ng1 = cd_norm_g[0].reshape(1, d)
    w_in1 = cd_w_in[0].astype(BF16)

    f, q, k, v, sz = _inproj_cd(x1, ng1, s1p, sh, w_in1, 512)
    kc, vc = _inproj_kv(xc1, ng1, cs1p, csh, w_in1[:, 1024:2048])

    c_l, s_l = _dft_tables(s)
    cs_tab = jnp.asarray(np.concatenate([c_l, -s_l], axis=1)).astype(BF16)
    c_c, s_c = _dft_tables(cd_w_fourier.shape[-1])
    w1 = _fourier_weights(jnp.asarray(c_c), jnp.asarray(s_c), cd_w_fourier[0])
    y_c = _fourier_mix(f, w1, cs_tab, 512)

    bias = _na_bias_tables(cd_rpb[0], s // GRID_W)
    y_d = _neighborhood_attention(q, k, v, kc, vc, bias)

    return _outproj(y_c, y_d, sz, cd_w_out[0].astype(BF16), x1, gate, final_norm_g.reshape(1, d), 512)
```

```python
import functools
import math

import numpy as np
import jax
import jax.numpy as jnp
from jax import lax
from jax.experimental import pallas as pl
from jax.experimental.pallas import tpu as pltpu

F32 = jnp.float32
BF16 = jnp.bfloat16
HIGHEST = lax.Precision.HIGHEST

D_MODEL = 1024
HEAD_DIM = 64
GRID_W = 64
CONV_WIDTH = 31
NA_ROWS = 8
NA_COLS = 16
ROPE_THETA = 10000.0
EPS = 1e-6
NEG_INF = -1e30
LOG2E = math.log2(math.e)
Q_SCALE = HEAD_DIM ** -0.5 * LOG2E
LANES = 128
SUBLANES = 8
HALO = 16
VMEM_LIMIT = 48 * 1024 * 1024

NA_TILE_ROWS = 4
NA_WIN_ROWS = NA_TILE_ROWS + NA_ROWS
NA_BATCH_PER_STEP = 8


def _params(sem):
    return pltpu.CompilerParams(dimension_semantics=sem, vmem_limit_bytes=VMEM_LIMIT)


def _silu(t):
    return t * jax.nn.sigmoid(t)


def _dot(a, b):
    return jnp.dot(a, b, preferred_element_type=F32)


def _dot_nt(a, b):
    return lax.dot_general(a, b, (((1,), (1,)), ((), ())), preferred_element_type=F32)


def _rope_tables(seq):
    t = np.arange(seq)
    row = (t // GRID_W).astype(np.float64)
    col = (t % GRID_W).astype(np.float64)
    nf = HEAD_DIM // 4
    inv = np.float32(ROPE_THETA) ** (-np.arange(nf, dtype=np.float32) / nf)
    ang = np.concatenate([row[:, None] * inv, col[:, None] * inv], axis=-1)
    cos = np.repeat(np.cos(ang), 2, axis=-1)
    sin = np.repeat(np.sin(ang), 2, axis=-1)
    sign = np.tile(np.array([-1.0, 1.0]), HEAD_DIM // 2)
    cos = np.tile(cos, (1, LANES // HEAD_DIM))
    sin = np.tile(sin * sign, (1, LANES // HEAD_DIM))
    return cos.astype(np.float32), sin.astype(np.float32)


def _dft_tables(n):
    k = np.arange(n)
    ang = 2.0 * np.pi * ((k[:, None] * k[None, :]) % n) / n
    s = 1.0 / math.sqrt(n)
    return (np.cos(ang) * s).astype(np.float32), (np.sin(ang) * s).astype(np.float32)


def _block_ones(width):
    h = np.arange(width) // HEAD_DIM
    return (h[:, None] == h[None, :]).astype(np.float32)


def _adaln_body(c_ref, w_ref, b_ref, o_ref):
    a = _silu(c_ref[...])
    o_ref[...] = jnp.dot(a, w_ref[...], preferred_element_type=F32, precision=HIGHEST) + b_ref[...]


def _adaln(cond, w, b):
    r, d = cond.shape
    n = w.shape[1]
    tn = 1024
    return pl.pallas_call(
        _adaln_body,
        grid=(n // tn,),
        in_specs=[pl.BlockSpec((r, d), lambda j: (0, 0)),
                  pl.BlockSpec((d, tn), lambda j: (0, j)),
                  pl.BlockSpec((1, tn), lambda j: (0, j))],
        out_specs=pl.BlockSpec((r, tn), lambda j: (0, j)),
        out_shape=jax.ShapeDtypeStruct((r, n), F32),
        compiler_params=_params(("arbitrary",)),
        name="adaln",
    )(cond, w, b.reshape(1, n))


def _head_rms(t, ones_ref, gain):
    ms = _dot((t * t).astype(BF16), ones_ref[...]) * (1.0 / HEAD_DIM)
    return t * lax.rsqrt(ms + EPS) * gain


def _rope(t, cos, sin_signed):
    rows = t.shape[0]
    lane = lax.broadcasted_iota(jnp.int32, (rows, LANES), 1)
    even = (lane & 1) == 0
    outs = []
    for c in range(t.shape[1] // LANES):
        xc = t[:, c * LANES:(c + 1) * LANES]
        swapped = jnp.where(even, pltpu.roll(xc, LANES - 1, 1), pltpu.roll(xc, 1, 1))
        outs.append(xc * cos + swapped * sin_signed)
    return outs[0] if len(outs) == 1 else jnp.concatenate(outs, axis=1)


def _modulated_norm(x_ref, g_ref, s1p_ref, sh_ref):
    x = x_ref[0]
    y = x * lax.rsqrt(jnp.mean(x * x, axis=-1, keepdims=True) + EPS) * g_ref[...]
    return (y * s1p_ref[0] + sh_ref[0]).astype(BF16)


def _inproj_ab_body(rope, x_ref, g_ref, s1p_ref, sh_ref, w_ref, qg_ref, kg_ref, cos_ref, sin_ref, mq_ref, mk_ref,
                    glu_ref, q_ref, k_ref, v_ref, sz_ref):
    u = _dot(_modulated_norm(x_ref, g_ref, s1p_ref, sh_ref), w_ref[...])
    glu_ref[0] = u[:, 0:512] * jax.nn.sigmoid(u[:, 512:1024])
    q = _head_rms(u[:, 1024:1536], mq_ref, qg_ref[...])
    k = _head_rms(u[:, 1536:1664], mk_ref, kg_ref[...])
    if rope:
        q = _rope(q, cos_ref[...], sin_ref[...])
        k = _rope(k, cos_ref[...], sin_ref[...])
    q_ref[0] = (q * Q_SCALE).astype(BF16)
    v = u[:, 1664:1792]
    k_sw, v_sw = pltpu.roll(k, HEAD_DIM, 1), pltpu.roll(v, HEAD_DIM, 1)
    left = lax.broadcasted_iota(jnp.int32, k.shape, 1) < HEAD_DIM
    one = jnp.ones_like(v)
    k_ref[0] = jnp.concatenate([jnp.where(left, k, k_sw), jnp.where(left, k_sw, k)], axis=1).astype(BF16)
    v_ref[0] = jnp.concatenate([jnp.where(left, v, one), jnp.where(left, one, v_sw),
                                jnp.where(left, v_sw, one), jnp.where(left, one, v)], axis=1).astype(BF16)
    sz_ref[0] = _silu(u[:, 1792:2816]).astype(BF16)


def _inproj_ab(x, g, s1p, sh, w, qg, kg, cos, sin, mq, mk, rope, tm):
    b, s, d = x.shape
    n = w.shape[1]
    row = lambda bi, i: (bi, i, 0)
    const = lambda bi, i: (0, 0)
    per_b = lambda bi, i: (bi, 0, 0)
    tab = lambda bi, i: (i, 0)
    return pl.pallas_call(
        functools.partial(_inproj_ab_body, rope),
        grid=(b, s // tm),
        in_specs=[pl.BlockSpec((1, tm, d), row),
                  pl.BlockSpec((1, d), const),
                  pl.BlockSpec((1, 1, d), per_b),
                  pl.BlockSpec((1, 1, d), per_b),
                  pl.BlockSpec((d, n), const),
                  pl.BlockSpec((1, 512), const),
                  pl.BlockSpec((1, 128), const),
                  pl.BlockSpec((tm, LANES), tab),
                  pl.BlockSpec((tm, LANES), tab),
                  pl.BlockSpec((512, 512), const),
                  pl.BlockSpec((128, 128), const)],
        out_specs=[pl.BlockSpec((1, tm, 512), row),
                   pl.BlockSpec((1, tm, 512), row),
                   pl.BlockSpec((1, tm, 256), row),
                   pl.BlockSpec((1, tm, 512), row),
                   pl.BlockSpec((1, tm, 1024), row)],
        out_shape=[jax.ShapeDtypeStruct((b, s, 512), F32),
                   jax.ShapeDtypeStruct((b, s, 512), BF16),
                   jax.ShapeDtypeStruct((b, s, 256), BF16),
                   jax.ShapeDtypeStruct((b, s, 512), BF16),
                   jax.ShapeDtypeStruct((b, s, 1024), BF16)],
        compiler_params=_params(("arbitrary", "arbitrary")),
        name="inproj_ab_rope" if rope else "inproj_ab_ctx",
    )(x, g, s1p, sh, w, qg, kg, cos, sin, mq, mk)


def _inproj_cd_body(x_ref, g_ref, s1p_ref, sh_ref, w_ref, f_ref, q_ref, k_ref, v_ref, sz_ref):
    u = _dot(_modulated_norm(x_ref, g_ref, s1p_ref, sh_ref), w_ref[...])
    f_ref[0] = u[:, 0:512].astype(BF16)
    q_ref[0] = (u[:, 512:1024] * Q_SCALE).astype(BF16)
    k_ref[0] = u[:, 1024:1536].astype(BF16)
    v_ref[0] = _pairs_with_ones(u[:, 1536:2048]).astype(BF16)
    sz_ref[0] = _silu(u[:, 2048:3072]).astype(BF16)


def _inproj_cd(x, g, s1p, sh, w, tm):
    b, s, d = x.shape
    n = w.shape[1]
    row = lambda bi, i: (bi, i, 0)
    const = lambda bi, i: (0, 0)
    per_b = lambda bi, i: (bi, 0, 0)
    return pl.pallas_call(
        _inproj_cd_body,
        grid=(b, s // tm),
        in_specs=[pl.BlockSpec((1, tm, d), row),
                  pl.BlockSpec((1, d), const),
                  pl.BlockSpec((1, 1, d), per_b),
                  pl.BlockSpec((1, 1, d), per_b),
                  pl.BlockSpec((d, n), const)],
        out_specs=[pl.BlockSpec((1, tm, 512), row)] * 3 + [pl.BlockSpec((1, tm, 1024), row)] * 2,
        out_shape=[jax.ShapeDtypeStruct((b, s, 512), BF16)] * 3 + [jax.ShapeDtypeStruct((b, s, 1024), BF16)] * 2,
        compiler_params=_params(("arbitrary", "arbitrary")),
        name="inproj_cd",
    )(x, g, s1p, sh, w)


def _inproj_kv_body(x_ref, g_ref, s1p_ref, sh_ref, w_ref, k_ref, v_ref):
    u = _dot(_modulated_norm(x_ref, g_ref, s1p_ref, sh_ref), w_ref[...])
    k_ref[0] = u[:, 0:512].astype(BF16)
    v_ref[0] = _pairs_with_ones(u[:, 512:1024]).astype(BF16)


def _inproj_kv(x, g, s1p, sh, w, col_block):
    b, s, d = x.shape
    n = 1024
    row = lambda bi: (bi, 0, 0)
    const = lambda bi: (0, 0)
    return pl.pallas_call(
        _inproj_kv_body,
        grid=(b,),
        in_specs=[pl.BlockSpec((1, s, d), row),
                  pl.BlockSpec((1, d), const),
                  pl.BlockSpec((1, 1, d), row),
                  pl.BlockSpec((1, 1, d), row),
                  pl.BlockSpec((d, n), lambda bi: (0, col_block))],
        out_specs=[pl.BlockSpec((1, s, 512), row), pl.BlockSpec((1, s, 1024), row)],
        out_shape=[jax.ShapeDtypeStruct((b, s, 512), BF16), jax.ShapeDtypeStruct((b, s, 1024), BF16)],
        compiler_params=_params(("arbitrary",)),
        name="inproj_kv_ctx",
    )(x, g, s1p, sh, w)


def _split_heads(q):
    lane = lax.broadcasted_iota(jnp.int32, q.shape, 1)
    zero = jnp.zeros_like(q)
    return jnp.concatenate([jnp.where(lane < HEAD_DIM, q, zero), jnp.where(lane >= HEAD_DIM, q, zero)], axis=0)


def _with_ones(v):
    lane = lax.broadcasted_iota(jnp.int32, v.shape, 1)
    one = jnp.ones_like(v)
    return jnp.concatenate([jnp.where(lane < HEAD_DIM, v, one), jnp.where(lane < HEAD_DIM, one, v)], axis=1)


def _pairs_with_ones(v):
    return jnp.concatenate([_with_ones(v[:, c:c + LANES]) for c in range(0, v.shape[1], LANES)], axis=1)


def _online_softmax_pv(q2, chunks):
    rows = q2.shape[0] // 2
    m = acc = None
    for k, v2, bias in chunks:
        s = _dot_nt(q2, k)
        if bias is not None:
            s = s + bias
        mc = jnp.max(s, axis=-1, keepdims=True)
        m_new = mc if m is None else jnp.maximum(m, mc)
        p = jnp.exp2(s - m_new).astype(BF16)
        pv = jnp.concatenate([_dot(p[:rows], v2[:, :LANES]), _dot(p[rows:], v2[:, LANES:])], axis=0)
        acc = pv if m is None else jnp.exp2(m - m_new) * acc + pv
        m = m_new
    left = lax.broadcasted_iota(jnp.int32, (rows, LANES), 1) < HEAD_DIM
    top, bot = acc[:rows], acc[rows:]
    return jnp.where(left, top, bot) / pltpu.roll(jnp.where(left, bot, top), HEAD_DIM, 1)


def _key_chunks(k_ref, v_ref, kv, chunk):
    n = k_ref.shape[1]
    kl, vl = slice(kv * LANES, (kv + 1) * LANES), slice(kv * 2 * LANES, (kv + 1) * 2 * LANES)
    return [(k_ref[0, c:c + chunk, kl], v_ref[0, c:c + chunk, vl], None) for c in range(0, n, chunk)]


def _gqa_body(n_src, chunk, q_ref, *refs):
    k_refs, v_refs, o_ref = refs[:n_src], refs[n_src:2 * n_src], refs[2 * n_src]
    pairs = q_ref.shape[2] // LANES
    kv_heads = k_refs[0].shape[2] // LANES
    for p in range(pairs):
        lanes = slice(p * LANES, (p + 1) * LANES)
        kv = p * kv_heads // pairs
        chunks = [c for k, v in zip(k_refs, v_refs) for c in _key_chunks(k, v, kv, min(chunk, k.shape[1]))]
        o_ref[0, :, lanes] = _online_softmax_pv(_split_heads(q_ref[0, :, lanes]), chunks).astype(o_ref.dtype)


def _gqa_attention(q, ks, vs, tq, pairs, chunk=512):
    b, lq, w = q.shape
    n_src = len(ks)
    n_pairs = w // LANES
    kv_heads = ks[0].shape[2] // LANES
    kv_blk = max(1, pairs * kv_heads // n_pairs)
    kv_idx = lambda bi, i, j: (bi, 0, j * pairs * kv_heads // n_pairs // kv_blk)
    k_specs = [pl.BlockSpec((1, k.shape[1], kv_blk * LANES), kv_idx) for k in ks]
    v_specs = [pl.BlockSpec((1, v.shape[1], kv_blk * 2 * LANES), kv_idx) for v in vs]
    return pl.pallas_call(
        functools.partial(_gqa_body, n_src, chunk),
        grid=(b, lq // tq, n_pairs // pairs),
        in_specs=[pl.BlockSpec((1, tq, pairs * LANES), lambda bi, i, j: (bi, i, j))] + k_specs + v_specs,
        out_specs=pl.BlockSpec((1, tq, pairs * LANES), lambda bi, i, j: (bi, i, j)),
        out_shape=jax.ShapeDtypeStruct((b, lq, w), BF16),
        compiler_params=_params(("arbitrary", "arbitrary", "arbitrary")),
        name=f"gqa_attention_{n_src}src",
    )(q, *ks, *vs)


def _conv_body(rows, x_ref, prev_ref, next_ref, w_ref, cb_ref, lg_ref, lb_ref, o_ref, pad_ref, shift_ref):
    i = pl.program_id(1)
    last = pl.num_programs(1) - 1
    pad_ref[0:HALO, :] = jnp.where(i > 0, prev_ref[0], 0.0)
    pad_ref[HALO:HALO + rows, :] = x_ref[0]
    pad_ref[HALO + rows:HALO + rows + HALO, :] = jnp.where(i < last, next_ref[0], 0.0)
    span = shift_ref.shape[1]
    for r in range(1, SUBLANES):
        shift_ref[r] = pad_ref[r:r + span, :]
    sub = 32
    base = HALO - CONV_WIDTH // 2
    for r0 in range(0, rows, sub):
        acc = jnp.zeros((sub, x_ref.shape[2]), F32) + cb_ref[...]
        for k in range(CONV_WIDTH):
            a, r = divmod(base + k, SUBLANES)
            lo = a * SUBLANES + r0
            tap = pad_ref[lo:lo + sub, :] if r == 0 else shift_ref[r, lo:lo + sub, :]
            acc = acc + tap * jnp.concatenate([w_ref[k]] * (sub // SUBLANES), axis=0)
        mu = jnp.mean(acc, axis=-1, keepdims=True)
        xc = acc - mu
        y = xc * lax.rsqrt(jnp.mean(xc * xc, axis=-1, keepdims=True) + EPS) * lg_ref[...] + lb_ref[...]
        o_ref[0, r0:r0 + sub, :] = _silu(y).astype(o_ref.dtype)


def _conformer_conv(u, conv_w, conv_b, ln_g, ln_b, rows):
    b, l, c = u.shape
    hb = rows // HALO
    n_halo = l // HALO
    return pl.pallas_call(
        functools.partial(_conv_body, rows),
        grid=(b, l // rows),
        in_specs=[pl.BlockSpec((1, rows, c), lambda bi, i: (bi, i, 0)),
                  pl.BlockSpec((1, HALO, c), lambda bi, i: (bi, jnp.maximum(i * hb - 1, 0), 0)),
                  pl.BlockSpec((1, HALO, c), lambda bi, i: (bi, jnp.minimum((i + 1) * hb, n_halo - 1), 0)),
                  pl.BlockSpec((CONV_WIDTH, SUBLANES, c), lambda bi, i: (0, 0, 0)),
                  pl.BlockSpec((1, c), lambda bi, i: (0, 0)),
                  pl.BlockSpec((1, c), lambda bi, i: (0, 0)),
                  pl.BlockSpec((1, c), lambda bi, i: (0, 0))],
        out_specs=pl.BlockSpec((1, rows, c), lambda bi, i: (bi, i, 0)),
        out_shape=jax.ShapeDtypeStruct((b, l, c), BF16),
        scratch_shapes=[pltpu.VMEM((rows + 2 * HALO, c), F32),
                        pltpu.VMEM((SUBLANES, rows + 2 * HALO - SUBLANES, c), F32)],
        compiler_params=_params(("arbitrary", "arbitrary")),
        name="conformer_conv",
    )(u, u, u, jnp.broadcast_to(conv_w[:, None, :], (CONV_WIDTH, SUBLANES, c)),
      conv_b.reshape(1, c), ln_g.reshape(1, c), ln_b.reshape(1, c))


def _outproj_body(final, ya_ref, yb_ref, sz_ref, w_ref, x_ref, gate_ref, *rest):
    y = jnp.concatenate([ya_ref[0], yb_ref[0]], axis=-1).astype(F32) * sz_ref[0].astype(F32)
    xn = x_ref[0] + gate_ref[0] * _dot(y.astype(BF16), w_ref[...])
    if final:
        fg_ref, o_ref = rest
        o_ref[0] = xn * lax.rsqrt(jnp.mean(xn * xn, axis=-1, keepdims=True) + EPS) * fg_ref[...]
    else:
        rest[0][0] = xn


def _outproj(ya, yb, sz, w, x, gate, final_g, tm):
    b, s, d = x.shape
    row = lambda bi, i: (bi, i, 0)
    const = lambda bi, i: (0, 0)
    per_b = lambda bi, i: (bi, 0, 0)
    in_specs = [pl.BlockSpec((1, tm, 512), row),
                pl.BlockSpec((1, tm, 512), row),
                pl.BlockSpec((1, tm, 1024), row),
                pl.BlockSpec((1024, d), const),
                pl.BlockSpec((1, tm, d), row),
                pl.BlockSpec((1, 1, d), per_b)]
    args = [ya, yb, sz, w, x, gate]
    if final_g is not None:
        in_specs.append(pl.BlockSpec((1, d), const))
        args.append(final_g)
    return pl.pallas_call(
        functools.partial(_outproj_body, final_g is not None),
        grid=(b, s // tm),
        in_specs=in_specs,
        out_specs=pl.BlockSpec((1, tm, d), row),
        out_shape=jax.ShapeDtypeStruct((b, s, d), F32),
        compiler_params=_params(("arbitrary", "arbitrary")),
        name="outproj_final" if final_g is not None else "outproj",
    )(*args)


def _fourier_w_body(cc_ref, sc_ref, w_ref, o_ref):
    for g in range(w_ref.shape[0]):
        wc = jnp.dot(cc_ref[...], w_ref[g], preferred_element_type=F32, precision=HIGHEST)
        ws = jnp.dot(sc_ref[...], w_ref[g], preferred_element_type=F32, precision=HIGHEST)
        o_ref[g] = jnp.concatenate([wc, ws], axis=-1).astype(o_ref.dtype)


def _fourier_weights(cc, sc, w):
    g, c, _ = w.shape
    return pl.pallas_call(
        _fourier_w_body,
        out_shape=jax.ShapeDtypeStruct((g, c, 2 * c), BF16),
        name="fourier_weights",
    )(cc, sc, w)


def _fourier_body(f_ref, w1_ref, cs_ref, o_ref, y_ref):
    groups, c, _ = w1_ref.shape
    seq = f_ref.shape[1]

    @pl.when(pl.program_id(1) == 0)
    def _():
        for g in range(groups):
            yg = _dot(f_ref[0, :, g * c:(g + 1) * c], w1_ref[g])
            y_ref[0:seq, g * c:(g + 1) * c] = yg[:, :c].astype(BF16)
            y_ref[seq:2 * seq, g * c:(g + 1) * c] = yg[:, c:].astype(BF16)

    o_ref[0] = _dot(cs_ref[...], y_ref[...]).astype(o_ref.dtype)


def _fourier_mix(f, w1, cs, tl):
    b, s, w = f.shape
    return pl.pallas_call(
        _fourier_body,
        grid=(b, s // tl),
        in_specs=[pl.BlockSpec((1, s, w), lambda bi, i: (bi, 0, 0)),
                  pl.BlockSpec(w1.shape, lambda bi, i: (0, 0, 0)),
                  pl.BlockSpec((tl, 2 * s), lambda bi, i: (i, 0))],
        out_specs=pl.BlockSpec((1, tl, w), lambda bi, i: (bi, i, 0)),
        out_shape=jax.ShapeDtypeStruct((b, s, w), BF16),
        scratch_shapes=[pltpu.VMEM((2 * s, w), BF16)],
        compiler_params=_params(("arbitrary", "arbitrary")),
        name="fourier_mix",
    )(f, w1, cs)


def _na_tile_geometry(rows_total):
    last_r0 = rows_total - NA_TILE_ROWS
    return ((0, 0), (2 * NA_TILE_ROWS, 2 * NA_TILE_ROWS - NA_ROWS // 2), (last_r0, rows_total - NA_WIN_ROWS))


def _na_bias_body(rows_total, rpb_ref, o_ref, t2_ref):
    h = pl.program_id(0)
    wq = lax.broadcasted_iota(jnp.int32, (GRID_W, LANES), 0)
    wk = lax.broadcasted_iota(jnp.int32, (GRID_W, LANES), 1) & (GRID_W - 1)
    cs = jnp.clip(wq - NA_COLS // 2, 0, GRID_W - NA_COLS)
    col_ok = (wk >= cs) & (wk < cs + NA_COLS)
    dc = wk - wq + (NA_COLS - 1)
    n_dr, n_dc = 2 * NA_ROWS - 1, 2 * NA_COLS - 1
    neg = jnp.full((GRID_W, LANES), NEG_INF, F32)
    for dr in range(n_dr):
        val = neg
        for c in range(n_dc):
            val = jnp.where(col_ok & (dc == c), rpb_ref[h * (n_dr * n_dc) + dr * n_dc + c] * LOG2E, val)
        t2_ref[dr] = val
    lane = lax.broadcasted_iota(jnp.int32, (GRID_W, LANES), 1)
    left = lane < GRID_W
    for ty, (r0, ws) in enumerate(_na_tile_geometry(rows_total)):
        for i in range(NA_TILE_ROWS):
            r = r0 + i
            rs = min(max(r - NA_ROWS // 2, 0), rows_total - NA_ROWS)
            for jp in range(NA_WIN_ROWS // 2):
                halves = []
                for kr in (ws + 2 * jp, ws + 2 * jp + 1):
                    halves.append(t2_ref[kr - r + NA_ROWS - 1] if rs <= kr < rs + NA_ROWS else neg)
                o_ref[0, ty, i * GRID_W:(i + 1) * GRID_W, jp * LANES:(jp + 1) * LANES] = jnp.where(left, halves[0], halves[1])


def _na_bias_tables(rpb, rows_total):
    h = rpb.shape[0]
    q_rows, k_cols = NA_TILE_ROWS * GRID_W, NA_WIN_ROWS * GRID_W
    return pl.pallas_call(
        functools.partial(_na_bias_body, rows_total),
        grid=(h,),
        in_specs=[pl.BlockSpec(memory_space=pltpu.SMEM)],
        out_specs=pl.BlockSpec((1, 3, q_rows, k_cols), lambda hi: (hi, 0, 0, 0)),
        out_shape=jax.ShapeDtypeStruct((h, 3, q_rows, k_cols), F32),
        scratch_shapes=[pltpu.VMEM((2 * NA_ROWS - 1, GRID_W, LANES), F32)],
        compiler_params=_params(("arbitrary",)),
        name="na_bias_tables",
    )(rpb.reshape(-1))


def _na_body(rows_total, q_ref, k_ref, v_ref, kc_ref, vc_ref, bias_ref, o_ref):
    t = pl.program_id(2)
    n_t = pl.num_programs(2)
    ty = jnp.where(t == 0, 0, jnp.where(t == n_t - 1, 2, 1))
    ws = jnp.clip(t * NA_TILE_ROWS - NA_ROWS // 2, 0, rows_total - NA_WIN_ROWS)
    off = pl.multiple_of(ws * GRID_W, GRID_W)
    n_win = NA_WIN_ROWS * GRID_W
    bias = jnp.concatenate([bias_ref[0, ty], bias_ref[1, ty]], axis=0)
    for bb in range(q_ref.shape[0]):
        q2 = _split_heads(q_ref[bb])
        o = _online_softmax_pv(q2, [(kc_ref[bb], vc_ref[bb], None),
                                    (k_ref[bb, pl.ds(off, n_win), :], v_ref[bb, pl.ds(off, n_win), :], bias)])
        o_ref[bb] = o.astype(o_ref.dtype)


def _neighborhood_attention(q, k, v, kc, vc, bias):
    b, s, w = q.shape
    l = kc.shape[1]
    rows_total = s // GRID_W
    tq = NA_TILE_ROWS * GRID_W
    nb = NA_BATCH_PER_STEP
    full = lambda j, bi, t: (bi, 0, j)
    return pl.pallas_call(
        functools.partial(_na_body, rows_total),
        grid=(w // LANES, b // nb, s // tq),
        in_specs=[pl.BlockSpec((nb, tq, LANES), lambda j, bi, t: (bi, t, j)),
                  pl.BlockSpec((nb, s, LANES), full),
                  pl.BlockSpec((nb, s, 2 * LANES), full),
                  pl.BlockSpec((nb, l, LANES), full),
                  pl.BlockSpec((nb, l, 2 * LANES), full),
                  pl.BlockSpec((2,) + bias.shape[1:], lambda j, bi, t: (j, 0, 0, 0))],
        out_specs=pl.BlockSpec((nb, tq, LANES), lambda j, bi, t: (bi, t, j)),
        out_shape=jax.ShapeDtypeStruct((b, s, w), BF16),
        compiler_params=_params(("arbitrary", "arbitrary", "arbitrary")),
        name="neighborhood_attention",
    )(q, k, v, kc, vc, bias)


def _modulation(m, rows, batch):
    d = m.shape[1] // 3
    pick = lambda lo: jnp.broadcast_to(m[rows, lo:lo + d].reshape(-1, 1, d), (batch, 1, d))
    return pick(0), 1.0 + pick(d), pick(2 * d)


def kernel(x, c, ctx, c_ctx, ab_w_ada, ab_b_ada, ab_norm_g, ab_w_in, ab_conv_w, ab_conv_b, ab_ln_g, ab_ln_b, ab_q_norm_g, ab_k_norm_g, ab_w_out, cd_w_ada, cd_b_ada, cd_norm_g, cd_w_in, cd_w_fourier, cd_rpb, cd_w_out, final_norm_g):
    b, s, d = x.shape
    l = ctx.shape[1]
    assert d == D_MODEL and s % (NA_TILE_ROWS * GRID_W) == 0 and ab_w_ada.shape[0] == 1 and cd_w_ada.shape[0] == 1

    cond = jnp.zeros((16, d), F32).at[:b].set(c).at[b].set(c_ctx)
    lat = slice(0, b)
    cx = slice(b, b + 1)

    m0 = _adaln(cond, ab_w_ada[0], ab_b_ada[0])
    sh, s1p, gate = _modulation(m0, lat, b)
    csh, cs1p, cgate = _modulation(m0, cx, b)

    w_in0 = ab_w_in[0].astype(BF16)
    w_out0 = ab_w_out[0].astype(BF16)

    cos, sin = _rope_tables(s)
    cos, sin = jnp.asarray(cos), jnp.asarray(sin)
    mq = jnp.asarray(_block_ones(512)).astype(BF16)
    mk = jnp.asarray(_block_ones(128)).astype(BF16)
    qg = jnp.tile(ab_q_norm_g[0], 512 // HEAD_DIM).reshape(1, 512)
    kg = jnp.tile(ab_k_norm_g[0], 128 // HEAD_DIM).reshape(1, 128)
    ng0 = ab_norm_g[0].reshape(1, d)

    glu, q, k, v, sz = _inproj_ab(x, ng0, s1p, sh, w_in0, qg, kg, cos, sin, mq, mk, True, 512)
    cglu, cq, ck, cv, csz = _inproj_ab(ctx, ng0, cs1p, csh, w_in0, qg, kg, cos[:l], sin[:l], mq, mk, False, l)

    y_b = _gqa_attention(q, [ck, k], [cv, v], 1024, 1)
    y_a = _conformer_conv(glu, ab_conv_w[0], ab_conv_b[0], ab_ln_g[0], ab_ln_b[0], 256)
    x1 = _outproj(y_a, y_b, sz, w_out0, x, gate, None, 512)

    yc_b = _gqa_attention(cq, [ck], [cv], l, 4)
    yc_a = _conformer_conv(cglu, ab_conv_w[0], ab_conv_b[0], ab_ln_g[0], ab_ln_b[0], l)
    xc1 = _outproj(yc_a, yc_b, csz, w_out0, ctx, cgate, None, l)

    m1 = _adaln(cond, cd_w_ada[0], cd_b_ada[0])
    sh, s1p, gate = _modulation(m1, lat, b)
    csh, cs1p, _ = _modulation(m1, cx, b)
    ng1 = cd_norm_g[0].reshape(1, d)
    w_in1 = cd_w_in[0].astype(BF16)

    f, q, k, v, sz = _inproj_cd(x1, ng1, s1p, sh, w_in1, 512)
    kc, vc = _inproj_kv(xc1, ng1, cs1p, csh, w_in1, 1)

    c_l, s_l = _dft_tables(s)
    cs_tab = jnp.asarray(np.concatenate([c_l, -s_l], axis=1)).astype(BF16)
    c_c, s_c = _dft_tables(cd_w_fourier.shape[-1])
    w1 = _fourier_weights(jnp.asarray(c_c), jnp.asarray(s_c), cd_w_fourier[0])
    y_c = _fourier_mix(f, w1, cs_tab, 512)

    bias = _na_bias_tables(cd_rpb[0], s // GRID_W)
    y_d = _neighborhood_attention(q, k, v, kc, vc, bias)

    return _outproj(y_c, y_d, sz, cd_w_out[0].astype(BF16), x1, gate, final_norm_g.reshape(1, d), 512)
```

```python
import functools
import math

import numpy as np
import jax
import jax.numpy as jnp
from jax import lax
from jax.experimental import pallas as pl
from jax.experimental.pallas import tpu as pltpu

F32 = jnp.float32
BF16 = jnp.bfloat16
HIGHEST = lax.Precision.HIGHEST

D_MODEL = 1024
HEAD_DIM = 64
GRID_W = 64
CONV_WIDTH = 31
NA_ROWS = 8
NA_COLS = 16
ROPE_THETA = 10000.0
EPS = 1e-6
NEG_INF = -1e30
LOG2E = math.log2(math.e)
Q_SCALE = HEAD_DIM ** -0.5 * LOG2E
LANES = 128
SUBLANES = 8
HALO = 16
VMEM_LIMIT = 48 * 1024 * 1024

NA_TILE_ROWS = 4
NA_WIN_ROWS = NA_TILE_ROWS + NA_ROWS
NA_BATCH_PER_STEP = 8
IN_PROJ_SUBTILE = 256
DFT_RADIX = 4


def _params(sem, flags=None):
    return pltpu.CompilerParams(dimension_semantics=sem, vmem_limit_bytes=VMEM_LIMIT, flags=flags)


def _silu(t):
    return t * jax.nn.sigmoid(t)


def _dot(a, b):
    return jnp.dot(a, b, preferred_element_type=F32)


def _dot_nt(a, b):
    return lax.dot_general(a, b, (((1,), (1,)), ((), ())), preferred_element_type=F32)


def _rope_tables(seq):
    t = np.arange(seq)
    row = (t // GRID_W).astype(np.float64)
    col = (t % GRID_W).astype(np.float64)
    nf = HEAD_DIM // 4
    inv = np.float32(ROPE_THETA) ** (-np.arange(nf, dtype=np.float32) / nf)
    ang = np.concatenate([row[:, None] * inv, col[:, None] * inv], axis=-1)
    cos = np.repeat(np.cos(ang), 2, axis=-1)
    sin = np.repeat(np.sin(ang), 2, axis=-1)
    sign = np.tile(np.array([-1.0, 1.0]), HEAD_DIM // 2)
    cos = np.tile(cos, (1, LANES // HEAD_DIM))
    sin = np.tile(sin * sign, (1, LANES // HEAD_DIM))
    return cos.astype(np.float32), sin.astype(np.float32)


def _dft_tables(n):
    k = np.arange(n)
    ang = 2.0 * np.pi * ((k[:, None] * k[None, :]) % n) / n
    s = 1.0 / math.sqrt(n)
    return (np.cos(ang) * s).astype(np.float32), (np.sin(ang) * s).astype(np.float32)


def _block_ones(width):
    h = np.arange(width) // HEAD_DIM
    return (h[:, None] == h[None, :]).astype(np.float32)


def _adaln_body(c_ref, w_ref, b_ref, o_ref):
    a = _silu(c_ref[...])
    o_ref[...] = jnp.dot(a, w_ref[...], preferred_element_type=F32, precision=HIGHEST) + b_ref[...]


def _adaln(cond, w, b):
    r, d = cond.shape
    n = w.shape[1]
    tn = 1024
    return pl.pallas_call(
        _adaln_body,
        grid=(n // tn,),
        in_specs=[pl.BlockSpec((r, d), lambda j: (0, 0)),
                  pl.BlockSpec((d, tn), lambda j: (0, j)),
                  pl.BlockSpec((1, tn), lambda j: (0, j))],
        out_specs=pl.BlockSpec((r, tn), lambda j: (0, j)),
        out_shape=jax.ShapeDtypeStruct((r, n), F32),
        compiler_params=_params(("arbitrary",)),
        name="adaln",
    )(cond, w, b.reshape(1, n))


def _head_rms(t, ones_ref, gain):
    ms = _dot((t * t).astype(BF16), ones_ref[...]) * (1.0 / HEAD_DIM)
    return t * lax.rsqrt(ms + EPS) * gain


def _rope(t, cos, sin_signed):
    rows = t.shape[0]
    lane = lax.broadcasted_iota(jnp.int32, (rows, LANES), 1)
    even = (lane & 1) == 0
    outs = []
    for c in range(t.shape[1] // LANES):
        xc = t[:, c * LANES:(c + 1) * LANES]
        swapped = jnp.where(even, pltpu.roll(xc, LANES - 1, 1), pltpu.roll(xc, 1, 1))
        outs.append(xc * cos + swapped * sin_signed)
    return outs[0] if len(outs) == 1 else jnp.concatenate(outs, axis=1)


def _modulated_norm(x_ref, g_ref, s1p_ref, sh_ref, rows):
    x = x_ref[0, rows, :]
    y = x * lax.rsqrt(jnp.mean(x * x, axis=-1, keepdims=True) + EPS) * g_ref[...]
    return (y * s1p_ref[0] + sh_ref[0]).astype(BF16)


def _row_subtiles(tm):
    sub = min(tm, IN_PROJ_SUBTILE)
    return [slice(r, r + sub) for r in range(0, tm, sub)]


def _inproj_ab_body(rope, x_ref, g_ref, s1p_ref, sh_ref, w_ref, qg_ref, kg_ref, cos_ref, sin_ref, mq_ref, mk_ref,
                    glu_ref, q_ref, k_ref, v_ref, sz_ref):
    for rows in _row_subtiles(x_ref.shape[1]):
        u = _dot(_modulated_norm(x_ref, g_ref, s1p_ref, sh_ref, rows), w_ref[...])
        glu_ref[0, rows, :] = u[:, 0:512] * jax.nn.sigmoid(u[:, 512:1024])
        q = _head_rms(u[:, 1024:1536], mq_ref, qg_ref[...])
        k = _head_rms(u[:, 1536:1664], mk_ref, kg_ref[...])
        if rope:
            q = _rope(q, cos_ref[rows, :], sin_ref[rows, :])
            k = _rope(k, cos_ref[rows, :], sin_ref[rows, :])
        q_ref[0, rows, :] = (q * Q_SCALE).astype(BF16)
        v = u[:, 1664:1792]
        k_sw, v_sw = pltpu.roll(k, HEAD_DIM, 1), pltpu.roll(v, HEAD_DIM, 1)
        left = lax.broadcasted_iota(jnp.int32, k.shape, 1) < HEAD_DIM
        one = jnp.ones_like(v)
        k_ref[0, rows, :] = jnp.concatenate([jnp.where(left, k, k_sw), jnp.where(left, k_sw, k)], axis=1).astype(BF16)
        v_ref[0, rows, :] = jnp.concatenate([jnp.where(left, v, one), jnp.where(left, one, v_sw),
                                             jnp.where(left, v_sw, one), jnp.where(left, one, v)],
                                            axis=1).astype(BF16)
        sz_ref[0, rows, :] = _silu(u[:, 1792:2816]).astype(BF16)


def _inproj_ab(x, g, s1p, sh, w, qg, kg, cos, sin, mq, mk, rope, tm):
    b, s, d = x.shape
    n = w.shape[1]
    row = lambda bi, i: (bi, i, 0)
    const = lambda bi, i: (0, 0)
    per_b = lambda bi, i: (bi, 0, 0)
    tab = lambda bi, i: (i, 0)
    return pl.pallas_call(
        functools.partial(_inproj_ab_body, rope),
        grid=(b, s // tm),
        in_specs=[pl.BlockSpec((1, tm, d), row),
                  pl.BlockSpec((1, d), const),
                  pl.BlockSpec((1, 1, d), per_b),
                  pl.BlockSpec((1, 1, d), per_b),
                  pl.BlockSpec((d, n), const),
                  pl.BlockSpec((1, 512), const),
                  pl.BlockSpec((1, 128), const),
                  pl.BlockSpec((tm, LANES), tab),
                  pl.BlockSpec((tm, LANES), tab),
                  pl.BlockSpec((512, 512), const),
                  pl.BlockSpec((128, 128), const)],
        out_specs=[pl.BlockSpec((1, tm, 512), row),
                   pl.BlockSpec((1, tm, 512), row),
                   pl.BlockSpec((1, tm, 256), row),
                   pl.BlockSpec((1, tm, 512), row),
                   pl.BlockSpec((1, tm, 1024), row)],
        out_shape=[jax.ShapeDtypeStruct((b, s, 512), F32),
                   jax.ShapeDtypeStruct((b, s, 512), BF16),
                   jax.ShapeDtypeStruct((b, s, 256), BF16),
                   jax.ShapeDtypeStruct((b, s, 512), BF16),
                   jax.ShapeDtypeStruct((b, s, 1024), BF16)],
        compiler_params=_params(("arbitrary", "arbitrary")),
        name="inproj_ab_rope" if rope else "inproj_ab_ctx",
    )(x, g, s1p, sh, w, qg, kg, cos, sin, mq, mk)


def _inproj_cd_body(x_ref, g_ref, s1p_ref, sh_ref, w_ref, f_ref, q_ref, k_ref, v_ref, sz_ref):
    for rows in _row_subtiles(x_ref.shape[1]):
        u = _dot(_modulated_norm(x_ref, g_ref, s1p_ref, sh_ref, rows), w_ref[...])
        f_ref[0, rows, :] = u[:, 0:512].astype(BF16)
        q_ref[0, rows, :] = (u[:, 512:1024] * Q_SCALE).astype(BF16)
        k_ref[0, rows, :] = u[:, 1024:1536].astype(BF16)
        v_ref[0, rows, :] = _pairs_with_ones(u[:, 1536:2048]).astype(BF16)
        sz_ref[0, rows, :] = _silu(u[:, 2048:3072]).astype(BF16)


def _inproj_cd(x, g, s1p, sh, w, tm):
    b, s, d = x.shape
    n = w.shape[1]
    row = lambda bi, i: (bi, i, 0)
    const = lambda bi, i: (0, 0)
    per_b = lambda bi, i: (bi, 0, 0)
    return pl.pallas_call(
        _inproj_cd_body,
        grid=(b, s // tm),
        in_specs=[pl.BlockSpec((1, tm, d), row),
                  pl.BlockSpec((1, d), const),
                  pl.BlockSpec((1, 1, d), per_b),
                  pl.BlockSpec((1, 1, d), per_b),
                  pl.BlockSpec((d, n), const)],
        out_specs=[pl.BlockSpec((1, tm, 512), row)] * 3 + [pl.BlockSpec((1, tm, 1024), row)] * 2,
        out_shape=[jax.ShapeDtypeStruct((b, s, 512), BF16)] * 3 + [jax.ShapeDtypeStruct((b, s, 1024), BF16)] * 2,
        compiler_params=_params(("arbitrary", "arbitrary")),
        name="inproj_cd",
    )(x, g, s1p, sh, w)


def _inproj_kv_body(x_ref, g_ref, s1p_ref, sh_ref, w_ref, k_ref, v_ref):
    u = _dot(_modulated_norm(x_ref, g_ref, s1p_ref, sh_ref, slice(None)), w_ref[...])
    k_ref[0] = u[:, 0:512].astype(BF16)
    v_ref[0] = _pairs_with_ones(u[:, 512:1024]).astype(BF16)


def _inproj_kv(x, g, s1p, sh, w, col_block):
    b, s, d = x.shape
    n = 1024
    row = lambda bi: (bi, 0, 0)
    const = lambda bi: (0, 0)
    return pl.pallas_call(
        _inproj_kv_body,
        grid=(b,),
        in_specs=[pl.BlockSpec((1, s, d), row),
                  pl.BlockSpec((1, d), const),
                  pl.BlockSpec((1, 1, d), row),
                  pl.BlockSpec((1, 1, d), row),
                  pl.BlockSpec((d, n), lambda bi: (0, col_block))],
        out_specs=[pl.BlockSpec((1, s, 512), row), pl.BlockSpec((1, s, 1024), row)],
        out_shape=[jax.ShapeDtypeStruct((b, s, 512), BF16), jax.ShapeDtypeStruct((b, s, 1024), BF16)],
        compiler_params=_params(("arbitrary",)),
        name="inproj_kv_ctx",
    )(x, g, s1p, sh, w)


def _split_heads(q):
    lane = lax.broadcasted_iota(jnp.int32, q.shape, 1)
    zero = jnp.zeros_like(q)
    return jnp.concatenate([jnp.where(lane < HEAD_DIM, q, zero), jnp.where(lane >= HEAD_DIM, q, zero)], axis=0)


def _with_ones(v):
    lane = lax.broadcasted_iota(jnp.int32, v.shape, 1)
    one = jnp.ones_like(v)
    return jnp.concatenate([jnp.where(lane < HEAD_DIM, v, one), jnp.where(lane < HEAD_DIM, one, v)], axis=1)


def _pairs_with_ones(v):
    return jnp.concatenate([_with_ones(v[:, c:c + LANES]) for c in range(0, v.shape[1], LANES)], axis=1)


def _online_softmax_pv(q2, chunks):
    rows = q2.shape[0] // 2
    m = acc = None
    for k, v2, bias in chunks:
        s = _dot_nt(q2, k)
        if bias is not None:
            s = s + bias
        mc = jnp.max(s, axis=-1, keepdims=True)
        m_new = mc if m is None else jnp.maximum(m, mc)
        p = jnp.exp2(s - m_new).astype(BF16)
        pv = jnp.concatenate([_dot(p[:rows], v2[:, :LANES]), _dot(p[rows:], v2[:, LANES:])], axis=0)
        acc = pv if m is None else jnp.exp2(m - m_new) * acc + pv
        m = m_new
    left = lax.broadcasted_iota(jnp.int32, (rows, LANES), 1) < HEAD_DIM
    top, bot = acc[:rows], acc[rows:]
    return jnp.where(left, top, bot) / pltpu.roll(jnp.where(left, bot, top), HEAD_DIM, 1)


def _key_chunks(k_ref, v_ref, kv, chunk):
    n = k_ref.shape[1]
    kl, vl = slice(kv * LANES, (kv + 1) * LANES), slice(kv * 2 * LANES, (kv + 1) * 2 * LANES)
    return [(k_ref[0, c:c + chunk, kl], v_ref[0, c:c + chunk, vl], None) for c in range(0, n, chunk)]


def _gqa_body(n_src, chunk, q_ref, *refs):
    k_refs, v_refs, o_ref = refs[:n_src], refs[n_src:2 * n_src], refs[2 * n_src]
    pairs = q_ref.shape[2] // LANES
    kv_heads = k_refs[0].shape[2] // LANES
    for p in range(pairs):
        lanes = slice(p * LANES, (p + 1) * LANES)
        kv = p * kv_heads // pairs
        chunks = [c for k, v in zip(k_refs, v_refs) for c in _key_chunks(k, v, kv, min(chunk, k.shape[1]))]
        o_ref[0, :, lanes] = _online_softmax_pv(_split_heads(q_ref[0, :, lanes]), chunks).astype(o_ref.dtype)


def _gqa_attention(q, ks, vs, tq, pairs, chunk=512):
    b, lq, w = q.shape
    n_src = len(ks)
    n_pairs = w // LANES
    kv_heads = ks[0].shape[2] // LANES
    kv_blk = max(1, pairs * kv_heads // n_pairs)
    kv_idx = lambda bi, i, j: (bi, 0, j * pairs * kv_heads // n_pairs // kv_blk)
    k_specs = [pl.BlockSpec((1, k.shape[1], kv_blk * LANES), kv_idx) for k in ks]
    v_specs = [pl.BlockSpec((1, v.shape[1], kv_blk * 2 * LANES), kv_idx) for v in vs]
    return pl.pallas_call(
        functools.partial(_gqa_body, n_src, chunk),
        grid=(b, lq // tq, n_pairs // pairs),
        in_specs=[pl.BlockSpec((1, tq, pairs * LANES), lambda bi, i, j: (bi, i, j))] + k_specs + v_specs,
        out_specs=pl.BlockSpec((1, tq, pairs * LANES), lambda bi, i, j: (bi, i, j)),
        out_shape=jax.ShapeDtypeStruct((b, lq, w), BF16),
        compiler_params=_params(("arbitrary", "arbitrary", "arbitrary")),
        name=f"gqa_attention_{n_src}src",
    )(q, *ks, *vs)


def _conv_body(rows, x_ref, prev_ref, next_ref, w_ref, cb_ref, lg_ref, lb_ref, o_ref, pad_ref, shift_ref):
    i = pl.program_id(1)
    last = pl.num_programs(1) - 1
    pad_ref[0:HALO, :] = jnp.where(i > 0, prev_ref[0], 0.0)
    pad_ref[HALO:HALO + rows, :] = x_ref[0]
    pad_ref[HALO + rows:HALO + rows + HALO, :] = jnp.where(i < last, next_ref[0], 0.0)
    span = shift_ref.shape[1]
    for r in range(1, SUBLANES):
        shift_ref[r] = pad_ref[r:r + span, :]
    sub = 32
    base = HALO - CONV_WIDTH // 2
    for r0 in range(0, rows, sub):
        acc = jnp.zeros((sub, x_ref.shape[2]), F32) + cb_ref[...]
        for k in range(CONV_WIDTH):
            a, r = divmod(base + k, SUBLANES)
            lo = a * SUBLANES + r0
            tap = pad_ref[lo:lo + sub, :] if r == 0 else shift_ref[r, lo:lo + sub, :]
            acc = acc + tap * jnp.concatenate([w_ref[k]] * (sub // SUBLANES), axis=0)
        mu = jnp.mean(acc, axis=-1, keepdims=True)
        xc = acc - mu
        y = xc * lax.rsqrt(jnp.mean(xc * xc, axis=-1, keepdims=True) + EPS) * lg_ref[...] + lb_ref[...]
        o_ref[0, r0:r0 + sub, :] = _silu(y).astype(o_ref.dtype)


def _conformer_conv(u, conv_w, conv_b, ln_g, ln_b, rows):
    b, l, c = u.shape
    hb = rows // HALO
    n_halo = l // HALO
    return pl.pallas_call(
        functools.partial(_conv_body, rows),
        grid=(b, l // rows),
        in_specs=[pl.BlockSpec((1, rows, c), lambda bi, i: (bi, i, 0)),
                  pl.BlockSpec((1, HALO, c), lambda bi, i: (bi, jnp.maximum(i * hb - 1, 0), 0)),
                  pl.BlockSpec((1, HALO, c), lambda bi, i: (bi, jnp.minimum((i + 1) * hb, n_halo - 1), 0)),
                  pl.BlockSpec((CONV_WIDTH, SUBLANES, c), lambda bi, i: (0, 0, 0)),
                  pl.BlockSpec((1, c), lambda bi, i: (0, 0)),
                  pl.BlockSpec((1, c), lambda bi, i: (0, 0)),
                  pl.BlockSpec((1, c), lambda bi, i: (0, 0))],
        out_specs=pl.BlockSpec((1, rows, c), lambda bi, i: (bi, i, 0)),
        out_shape=jax.ShapeDtypeStruct((b, l, c), BF16),
        scratch_shapes=[pltpu.VMEM((rows + 2 * HALO, c), F32),
                        pltpu.VMEM((SUBLANES, rows + 2 * HALO - SUBLANES, c), F32)],
        compiler_params=_params(("arbitrary", "arbitrary")),
        name="conformer_conv",
    )(u, u, u, jnp.broadcast_to(conv_w[:, None, :], (CONV_WIDTH, SUBLANES, c)),
      conv_b.reshape(1, c), ln_g.reshape(1, c), ln_b.reshape(1, c))


def _outproj_body(final, ya_ref, yb_ref, sz_ref, w_ref, x_ref, gate_ref, *rest):
    if final:
        perm_ref, fg_ref, o_ref = rest
        stacked = jnp.concatenate([ya_ref[0, j] for j in range(ya_ref.shape[1])], axis=0)
        ya = _dot(perm_ref[...], stacked)
    else:
        ya = ya_ref[0].astype(F32)
    y = jnp.concatenate([ya, yb_ref[0].astype(F32)], axis=-1) * sz_ref[0].astype(F32)
    xn = x_ref[0] + gate_ref[0] * _dot(y.astype(BF16), w_ref[...])
    if final:
        o_ref[0] = xn * lax.rsqrt(jnp.mean(xn * xn, axis=-1, keepdims=True) + EPS) * fg_ref[...]
    else:
        rest[0][0] = xn


def _outproj(ya, yb, sz, w, x, gate, final_g, tm):
    b, s, d = x.shape
    row = lambda bi, i: (bi, i, 0)
    const = lambda bi, i: (0, 0)
    per_b = lambda bi, i: (bi, 0, 0)
    final = final_g is not None
    ya_spec = (pl.BlockSpec((1, DFT_RADIX, tm // DFT_RADIX, 512), lambda bi, i: (bi, 0, i, 0)) if final
               else pl.BlockSpec((1, tm, 512), row))
    in_specs = [ya_spec,
                pl.BlockSpec((1, tm, 512), row),
                pl.BlockSpec((1, tm, 1024), row),
                pl.BlockSpec((1024, d), const),
                pl.BlockSpec((1, tm, d), row),
                pl.BlockSpec((1, 1, d), per_b)]
    args = [ya, yb, sz, w, x, gate]
    if final:
        in_specs += [pl.BlockSpec((tm, tm), const), pl.BlockSpec((1, d), const)]
        args += [jnp.asarray(_class_row_permutation(tm)).astype(BF16), final_g]
    return pl.pallas_call(
        functools.partial(_outproj_body, final),
        grid=(b, s // tm),
        in_specs=in_specs,
        out_specs=pl.BlockSpec((1, tm, d), row),
        out_shape=jax.ShapeDtypeStruct((b, s, d), F32),
        compiler_params=_params(("arbitrary", "arbitrary")),
        name="outproj_final" if final_g is not None else "outproj",
    )(*args)


def _fourier_w_body(cc_ref, sc_ref, w_ref, o_ref):
    for g in range(w_ref.shape[0]):
        wc = jnp.dot(cc_ref[...], w_ref[g], preferred_element_type=F32, precision=HIGHEST)
        ws = jnp.dot(sc_ref[...], w_ref[g], preferred_element_type=F32, precision=HIGHEST)
        o_ref[g] = jnp.concatenate([wc, ws], axis=-1).astype(o_ref.dtype)


def _fourier_weights(cc, sc, w):
    g, c, _ = w.shape
    return pl.pallas_call(
        _fourier_w_body,
        out_shape=jax.ShapeDtypeStruct((g, c, 2 * c), BF16),
        name="fourier_weights",
    )(cc, sc, w)


def _fourier_tables(seq):
    quarter = seq // DFT_RADIX
    m = np.arange(quarter)
    tabs = []
    for j in range(DFT_RADIX):
        ang = 2.0 * np.pi * (((DFT_RADIX * m + j)[:, None] * m[None, :]) % seq) / seq
        tabs.append(np.concatenate([np.cos(ang), -np.sin(ang)], axis=1) / math.sqrt(seq))
    return np.stack(tabs).astype(np.float32)


def _class_row_permutation(rows):
    per = rows // DFT_RADIX
    p = np.zeros((rows, rows), np.float32)
    for j in range(DFT_RADIX):
        p[DFT_RADIX * np.arange(per) + j, j * per + np.arange(per)] = 1.0
    return p


def _fourier_body(f_ref, w1_ref, tab_ref, o_ref, yc_ref, ys_ref):
    groups, c, _ = w1_ref.shape
    quarter = f_ref.shape[1] // DFT_RADIX
    for g in range(groups):
        yg = _dot(f_ref[0, :, g * c:(g + 1) * c], w1_ref[g])
        yc_ref[:, g * c:(g + 1) * c] = yg[:, :c]
        ys_ref[:, g * c:(g + 1) * c] = yg[:, c:]
    for j in range(DFT_RADIX):
        pc, ps = [], []
        for q in range(DFT_RADIX):
            rows = slice(q * quarter, (q + 1) * quarter)
            a = (j * q) % 4
            if a == 0:
                pc.append((1, yc_ref, rows)), ps.append((1, ys_ref, rows))
            elif a == 1:
                pc.append((-1, ys_ref, rows)), ps.append((1, yc_ref, rows))
            elif a == 2:
                pc.append((-1, yc_ref, rows)), ps.append((-1, ys_ref, rows))
            else:
                pc.append((1, ys_ref, rows)), ps.append((-1, yc_ref, rows))
        folded = []
        for terms in (pc, ps):
            terms = sorted(terms, key=lambda t: -t[0])
            acc = terms[0][1][terms[0][2], :]
            for sign, ref, rows in terms[1:]:
                acc = acc + ref[rows, :] if sign > 0 else acc - ref[rows, :]
            folded.append(acc.astype(BF16))
        o_ref[0, j] = _dot(tab_ref[j], jnp.concatenate(folded, axis=0)).astype(o_ref.dtype)


def _fourier_mix(f, w1, tab):
    b, s, w = f.shape
    quarter = s // DFT_RADIX
    return pl.pallas_call(
        _fourier_body,
        grid=(b,),
        in_specs=[pl.BlockSpec((1, s, w), lambda bi: (bi, 0, 0)),
                  pl.BlockSpec(w1.shape, lambda bi: (0, 0, 0)),
                  pl.BlockSpec(tab.shape, lambda bi: (0, 0, 0))],
        out_specs=pl.BlockSpec((1, DFT_RADIX, quarter, w), lambda bi: (bi, 0, 0, 0)),
        out_shape=jax.ShapeDtypeStruct((b, DFT_RADIX, quarter, w), BF16),
        scratch_shapes=[pltpu.VMEM((s, w), F32), pltpu.VMEM((s, w), F32)],
        compiler_params=_params(("arbitrary",)),
        name="fourier_mix",
    )(f, w1, tab)


def _na_tile_geometry(rows_total):
    last_r0 = rows_total - NA_TILE_ROWS
    return ((0, 0), (2 * NA_TILE_ROWS, 2 * NA_TILE_ROWS - NA_ROWS // 2), (last_r0, rows_total - NA_WIN_ROWS))


def _na_bias_body(rows_total, rpb_ref, o_ref, t2_ref):
    h = pl.program_id(0)
    wq = lax.broadcasted_iota(jnp.int32, (GRID_W, LANES), 0)
    wk = lax.broadcasted_iota(jnp.int32, (GRID_W, LANES), 1) & (GRID_W - 1)
    cs = jnp.clip(wq - NA_COLS // 2, 0, GRID_W - NA_COLS)
    col_ok = (wk >= cs) & (wk < cs + NA_COLS)
    dc = wk - wq + (NA_COLS - 1)
    n_dr, n_dc = 2 * NA_ROWS - 1, 2 * NA_COLS - 1
    neg = jnp.full((GRID_W, LANES), NEG_INF, F32)
    for dr in range(n_dr):
        val = neg
        for c in range(n_dc):
            val = jnp.where(col_ok & (dc == c), rpb_ref[h * (n_dr * n_dc) + dr * n_dc + c] * LOG2E, val)
        t2_ref[dr] = val
    lane = lax.broadcasted_iota(jnp.int32, (GRID_W, LANES), 1)
    left = lane < GRID_W
    for ty, (r0, ws) in enumerate(_na_tile_geometry(rows_total)):
        for i in range(NA_TILE_ROWS):
            r = r0 + i
            rs = min(max(r - NA_ROWS // 2, 0), rows_total - NA_ROWS)
            for jp in range(NA_WIN_ROWS // 2):
                halves = []
                for kr in (ws + 2 * jp, ws + 2 * jp + 1):
                    halves.append(t2_ref[kr - r + NA_ROWS - 1] if rs <= kr < rs + NA_ROWS else neg)
                o_ref[0, ty, i * GRID_W:(i + 1) * GRID_W, jp * LANES:(jp + 1) * LANES] = jnp.where(left, halves[0], halves[1])


def _na_bias_tables(rpb, rows_total):
    h = rpb.shape[0]
    q_rows, k_cols = NA_TILE_ROWS * GRID_W, NA_WIN_ROWS * GRID_W
    return pl.pallas_call(
        functools.partial(_na_bias_body, rows_total),
        grid=(h,),
        in_specs=[pl.BlockSpec(memory_space=pltpu.SMEM)],
        out_specs=pl.BlockSpec((1, 3, q_rows, k_cols), lambda hi: (hi, 0, 0, 0)),
        out_shape=jax.ShapeDtypeStruct((h, 3, q_rows, k_cols), F32),
        scratch_shapes=[pltpu.VMEM((2 * NA_ROWS - 1, GRID_W, LANES), F32)],
        compiler_params=_params(("arbitrary",)),
        name="na_bias_tables",
    )(rpb.reshape(-1))


def _na_body(rows_total, q_ref, k_ref, v_ref, kc_ref, vc_ref, bias_ref, o_ref):
    t = pl.program_id(2)
    n_t = pl.num_programs(2)
    ty = jnp.where(t == 0, 0, jnp.where(t == n_t - 1, 2, 1))
    ws = jnp.clip(t * NA_TILE_ROWS - NA_ROWS // 2, 0, rows_total - NA_WIN_ROWS)
    off = pl.multiple_of(ws * GRID_W, GRID_W)
    n_win = NA_WIN_ROWS * GRID_W
    bias = jnp.concatenate([bias_ref[0, ty], bias_ref[1, ty]], axis=0)
    for bb in range(q_ref.shape[0]):
        q2 = _split_heads(q_ref[bb])
        o = _online_softmax_pv(q2, [(kc_ref[bb], vc_ref[bb], None),
                                    (k_ref[bb, pl.ds(off, n_win), :], v_ref[bb, pl.ds(off, n_win), :], bias)])
        o_ref[bb] = o.astype(o_ref.dtype)


def _neighborhood_attention(q, k, v, kc, vc, bias):
    b, s, w = q.shape
    l = kc.shape[1]
    rows_total = s // GRID_W
    tq = NA_TILE_ROWS * GRID_W
    nb = NA_BATCH_PER_STEP
    full = lambda j, bi, t: (bi, 0, j)
    return pl.pallas_call(
        functools.partial(_na_body, rows_total),
        grid=(w // LANES, b // nb, s // tq),
        in_specs=[pl.BlockSpec((nb, tq, LANES), lambda j, bi, t: (bi, t, j)),
                  pl.BlockSpec((nb, s, LANES), full),
                  pl.BlockSpec((nb, s, 2 * LANES), full),
                  pl.BlockSpec((nb, l, LANES), full),
                  pl.BlockSpec((nb, l, 2 * LANES), full),
                  pl.BlockSpec((2,) + bias.shape[1:], lambda j, bi, t: (j, 0, 0, 0))],
        out_specs=pl.BlockSpec((nb, tq, LANES), lambda j, bi, t: (bi, t, j)),
        out_shape=jax.ShapeDtypeStruct((b, s, w), BF16),
        compiler_params=_params(("arbitrary", "arbitrary", "arbitrary")),
        name="neighborhood_attention",
    )(q, k, v, kc, vc, bias)


def _modulation(m, rows, batch):
    d = m.shape[1] // 3
    pick = lambda lo: jnp.broadcast_to(m[rows, lo:lo + d].reshape(-1, 1, d), (batch, 1, d))
    return pick(0), 1.0 + pick(d), pick(2 * d)


def kernel(x, c, ctx, c_ctx, ab_w_ada, ab_b_ada, ab_norm_g, ab_w_in, ab_conv_w, ab_conv_b, ab_ln_g, ab_ln_b, ab_q_norm_g, ab_k_norm_g, ab_w_out, cd_w_ada, cd_b_ada, cd_norm_g, cd_w_in, cd_w_fourier, cd_rpb, cd_w_out, final_norm_g):
    b, s, d = x.shape
    l = ctx.shape[1]
    assert d == D_MODEL and s % (NA_TILE_ROWS * GRID_W) == 0 and ab_w_ada.shape[0] == 1 and cd_w_ada.shape[0] == 1

    cond = jnp.zeros((16, d), F32).at[:b].set(c).at[b].set(c_ctx)
    lat = slice(0, b)
    cx = slice(b, b + 1)

    m0 = _adaln(cond, ab_w_ada[0], ab_b_ada[0])
    sh, s1p, gate = _modulation(m0, lat, b)
    csh, cs1p, cgate = _modulation(m0, cx, b)

    w_in0 = ab_w_in[0].astype(BF16)
    w_out0 = ab_w_out[0].astype(BF16)

    cos, sin = _rope_tables(s)
    cos, sin = jnp.asarray(cos), jnp.asarray(sin)
    mq = jnp.asarray(_block_ones(512)).astype(BF16)
    mk = jnp.asarray(_block_ones(128)).astype(BF16)
    qg = jnp.tile(ab_q_norm_g[0], 512 // HEAD_DIM).reshape(1, 512)
    kg = jnp.tile(ab_k_norm_g[0], 128 // HEAD_DIM).reshape(1, 128)
    ng0 = ab_norm_g[0].reshape(1, d)

    glu, q, k, v, sz = _inproj_ab(x, ng0, s1p, sh, w_in0, qg, kg, cos, sin, mq, mk, True, 1024)
    cglu, cq, ck, cv, csz = _inproj_ab(ctx, ng0, cs1p, csh, w_in0, qg, kg, cos[:l], sin[:l], mq, mk, False, l)

    y_b = _gqa_attention(q, [ck, k], [cv, v], 1024, 1)
    y_a = _conformer_conv(glu, ab_conv_w[0], ab_conv_b[0], ab_ln_g[0], ab_ln_b[0], 256)
    x1 = _outproj(y_a, y_b, sz, w_out0, x, gate, None, 512)

    yc_b = _gqa_attention(cq, [ck], [cv], l, 4)
    yc_a = _conformer_conv(cglu, ab_conv_w[0], ab_conv_b[0], ab_ln_g[0], ab_ln_b[0], l)
    xc1 = _outproj(yc_a, yc_b, csz, w_out0, ctx, cgate, None, l)

    m1 = _adaln(cond, cd_w_ada[0], cd_b_ada[0])
    sh, s1p, gate = _modulation(m1, lat, b)
    csh, cs1p, _ = _modulation(m1, cx, b)
    ng1 = cd_norm_g[0].reshape(1, d)
    w_in1 = cd_w_in[0].astype(BF16)

    f, q, k, v, sz = _inproj_cd(x1, ng1, s1p, sh, w_in1, 1024)
    kc, vc = _inproj_kv(xc1, ng1, cs1p, csh, w_in1, 1)

    c_c, s_c = _dft_tables(cd_w_fourier.shape[-1])
    w1 = _fourier_weights(jnp.asarray(c_c), jnp.asarray(s_c), cd_w_fourier[0])
    y_c = _fourier_mix(f, w1, jnp.asarray(_fourier_tables(s)).astype(BF16))

    bias = _na_bias_tables(cd_rpb[0], s // GRID_W)
    y_d = _neighborhood_attention(q, k, v, kc, vc, bias)

    return _outproj(y_c, y_d, sz, cd_w_out[0].astype(BF16), x1, gate, final_norm_g.reshape(1, d), 512)
```

```python
import functools
import math

import numpy as np
import jax
import jax.numpy as jnp
from jax import lax
from jax.experimental import pallas as pl
from jax.experimental.pallas import tpu as pltpu

F32 = jnp.float32
BF16 = jnp.bfloat16
HIGHEST = lax.Precision.HIGHEST

D_MODEL = 1024
HEAD_DIM = 64
GRID_W = 64
CONV_WIDTH = 31
NA_ROWS = 8
NA_COLS = 16
ROPE_THETA = 10000.0
EPS = 1e-6
NEG_INF = -1e30
LOG2E = math.log2(math.e)
Q_SCALE = HEAD_DIM ** -0.5 * LOG2E
LANES = 128
SUBLANES = 8
HALO = 16
VMEM_LIMIT = 48 * 1024 * 1024

NA_TILE_ROWS = 4
NA_WIN_ROWS = NA_TILE_ROWS + NA_ROWS
NA_BATCH_PER_STEP = 8
IN_PROJ_SUBTILE = 256
DFT_RADIX = 4


def _params(sem, flags=None):
    return pltpu.CompilerParams(dimension_semantics=sem, vmem_limit_bytes=VMEM_LIMIT, flags=flags)


def _silu(t):
    return t * jax.nn.sigmoid(t)


def _dot(a, b):
    return jnp.dot(a, b, preferred_element_type=F32)


def _dot_nt(a, b):
    return lax.dot_general(a, b, (((1,), (1,)), ((), ())), preferred_element_type=F32)


def _rope_tables(seq):
    t = np.arange(seq)
    row = (t // GRID_W).astype(np.float64)
    col = (t % GRID_W).astype(np.float64)
    nf = HEAD_DIM // 4
    inv = np.float32(ROPE_THETA) ** (-np.arange(nf, dtype=np.float32) / nf)
    ang = np.concatenate([row[:, None] * inv, col[:, None] * inv], axis=-1)
    cos = np.repeat(np.cos(ang), 2, axis=-1)
    sin = np.repeat(np.sin(ang), 2, axis=-1)
    sign = np.tile(np.array([-1.0, 1.0]), HEAD_DIM // 2)
    cos = np.tile(cos, (1, LANES // HEAD_DIM))
    sin = np.tile(sin * sign, (1, LANES // HEAD_DIM))
    return cos.astype(np.float32), sin.astype(np.float32)


def _dft_tables(n):
    k = np.arange(n)
    ang = 2.0 * np.pi * ((k[:, None] * k[None, :]) % n) / n
    s = 1.0 / math.sqrt(n)
    return (np.cos(ang) * s).astype(np.float32), (np.sin(ang) * s).astype(np.float32)


def _block_ones(width):
    h = np.arange(width) // HEAD_DIM
    return (h[:, None] == h[None, :]).astype(np.float32)


def _adaln_body(c_ref, w_ref, b_ref, o_ref):
    a = _silu(c_ref[...])
    o_ref[...] = jnp.dot(a, w_ref[...], preferred_element_type=F32, precision=HIGHEST) + b_ref[...]


def _adaln(cond, w, b):
    r, d = cond.shape
    n = w.shape[1]
    tn = 1024
    return pl.pallas_call(
        _adaln_body,
        grid=(n // tn,),
        in_specs=[pl.BlockSpec((r, d), lambda j: (0, 0)),
                  pl.BlockSpec((d, tn), lambda j: (0, j)),
                  pl.BlockSpec((1, tn), lambda j: (0, j))],
        out_specs=pl.BlockSpec((r, tn), lambda j: (0, j)),
        out_shape=jax.ShapeDtypeStruct((r, n), F32),
        compiler_params=_params(("arbitrary",)),
        name="adaln",
    )(cond, w, b.reshape(1, n))


def _head_rms(t, ones_ref, gain):
    ms = _dot((t * t).astype(BF16), ones_ref[...]) * (1.0 / HEAD_DIM)
    return t * lax.rsqrt(ms + EPS) * gain


def _rope(t, cos, sin_signed):
    rows = t.shape[0]
    lane = lax.broadcasted_iota(jnp.int32, (rows, LANES), 1)
    even = (lane & 1) == 0
    outs = []
    for c in range(t.shape[1] // LANES):
        xc = t[:, c * LANES:(c + 1) * LANES]
        swapped = jnp.where(even, pltpu.roll(xc, LANES - 1, 1), pltpu.roll(xc, 1, 1))
        outs.append(xc * cos + swapped * sin_signed)
    return outs[0] if len(outs) == 1 else jnp.concatenate(outs, axis=1)


def _modulate(x, g_ref, s1p_ref, sh_ref):
    y = x * lax.rsqrt(jnp.mean(x * x, axis=-1, keepdims=True) + EPS) * g_ref[...]
    return (y * s1p_ref[0] + sh_ref[0]).astype(BF16)


def _modulated_norm(x_ref, g_ref, s1p_ref, sh_ref, rows):
    return _modulate(x_ref[0, rows, :], g_ref, s1p_ref, sh_ref)


def _row_subtiles(tm):
    sub = min(tm, IN_PROJ_SUBTILE)
    return [slice(r, r + sub) for r in range(0, tm, sub)]


def _inproj_ab_body(rope, x_ref, g_ref, s1p_ref, sh_ref, w_ref, qg_ref, kg_ref, cos_ref, sin_ref, mq_ref, mk_ref,
                    glu_ref, q_ref, k_ref, v_ref, sz_ref):
    for rows in _row_subtiles(x_ref.shape[1]):
        u = _dot(_modulated_norm(x_ref, g_ref, s1p_ref, sh_ref, rows), w_ref[...])
        glu_ref[0, rows, :] = u[:, 0:512] * jax.nn.sigmoid(u[:, 512:1024])
        q = _head_rms(u[:, 1024:1536], mq_ref, qg_ref[...])
        k = _head_rms(u[:, 1536:1664], mk_ref, kg_ref[...])
        if rope:
            q = _rope(q, cos_ref[rows, :], sin_ref[rows, :])
            k = _rope(k, cos_ref[rows, :], sin_ref[rows, :])
        q_ref[0, rows, :] = (q * Q_SCALE).astype(BF16)
        v = u[:, 1664:1792]
        k_sw, v_sw = pltpu.roll(k, HEAD_DIM, 1), pltpu.roll(v, HEAD_DIM, 1)
        left = lax.broadcasted_iota(jnp.int32, k.shape, 1) < HEAD_DIM
        one = jnp.ones_like(v)
        k_ref[0, rows, :] = jnp.concatenate([jnp.where(left, k, k_sw), jnp.where(left, k_sw, k)], axis=1).astype(BF16)
        v_ref[0, rows, :] = jnp.concatenate([jnp.where(left, v, one), jnp.where(left, one, v_sw),
                                             jnp.where(left, v_sw, one), jnp.where(left, one, v)],
                                            axis=1).astype(BF16)
        sz_ref[0, rows, :] = _silu(u[:, 1792:2816]).astype(BF16)


def _inproj_ab(x, g, s1p, sh, w, qg, kg, cos, sin, mq, mk, rope, tm):
    b, s, d = x.shape
    n = w.shape[1]
    row = lambda bi, i: (bi, i, 0)
    const = lambda bi, i: (0, 0)
    per_b = lambda bi, i: (bi, 0, 0)
    tab = lambda bi, i: (i, 0)
    return pl.pallas_call(
        functools.partial(_inproj_ab_body, rope),
        grid=(b, s // tm),
        in_specs=[pl.BlockSpec((1, tm, d), row),
                  pl.BlockSpec((1, d), const),
                  pl.BlockSpec((1, 1, d), per_b),
                  pl.BlockSpec((1, 1, d), per_b),
                  pl.BlockSpec((d, n), const),
                  pl.BlockSpec((1, 512), const),
                  pl.BlockSpec((1, 128), const),
                  pl.BlockSpec((tm, LANES), tab),
                  pl.BlockSpec((tm, LANES), tab),
                  pl.BlockSpec((512, 512), const),
                  pl.BlockSpec((128, 128), const)],
        out_specs=[pl.BlockSpec((1, tm, 512), row),
                   pl.BlockSpec((1, tm, 512), row),
                   pl.BlockSpec((1, tm, 256), row),
                   pl.BlockSpec((1, tm, 512), row),
                   pl.BlockSpec((1, tm, 1024), row)],
        out_shape=[jax.ShapeDtypeStruct((b, s, 512), F32),
                   jax.ShapeDtypeStruct((b, s, 512), BF16),
                   jax.ShapeDtypeStruct((b, s, 256), BF16),
                   jax.ShapeDtypeStruct((b, s, 512), BF16),
                   jax.ShapeDtypeStruct((b, s, 1024), BF16)],
        compiler_params=_params(("arbitrary", "arbitrary")),
        name="inproj_ab_rope" if rope else "inproj_ab_ctx",
    )(x, g, s1p, sh, w, qg, kg, cos, sin, mq, mk)


def _outproj_inproj_cd_body(ya_ref, yb_ref, sz0_ref, wo_ref, x_ref, gate_ref, g_ref, s1p_ref, sh_ref, wi_ref,
                            x1_ref, f_ref, q_ref, k_ref, v_ref, sz_ref):
    for rows in _row_subtiles(x_ref.shape[1]):
        y = (jnp.concatenate([ya_ref[0, rows, :], yb_ref[0, rows, :]], axis=-1).astype(F32)
             * sz0_ref[0, rows, :].astype(F32))
        xn = x_ref[0, rows, :] + gate_ref[0] * _dot(y.astype(BF16), wo_ref[...])
        x1_ref[0, rows, :] = xn
        u = _dot(_modulate(xn, g_ref, s1p_ref, sh_ref), wi_ref[...])
        f_ref[0, rows, :] = u[:, 0:512].astype(BF16)
        q_ref[0, rows, :] = (u[:, 512:1024] * Q_SCALE).astype(BF16)
        k_ref[0, rows, :] = u[:, 1024:1536].astype(BF16)
        v_ref[0, rows, :] = _pairs_with_ones(u[:, 1536:2048]).astype(BF16)
        sz_ref[0, rows, :] = _silu(u[:, 2048:3072]).astype(BF16)


def _outproj_inproj_cd(ya, yb, sz0, w_out, x, gate, g, s1p, sh, w_in, tm):
    b, s, d = x.shape
    n = w_in.shape[1]
    row = lambda bi, i: (bi, i, 0)
    const = lambda bi, i: (0, 0)
    per_b = lambda bi, i: (bi, 0, 0)
    return pl.pallas_call(
        _outproj_inproj_cd_body,
        grid=(b, s // tm),
        in_specs=[pl.BlockSpec((1, tm, 512), row),
                  pl.BlockSpec((1, tm, 512), row),
                  pl.BlockSpec((1, tm, 1024), row),
                  pl.BlockSpec((1024, d), const),
                  pl.BlockSpec((1, tm, d), row),
                  pl.BlockSpec((1, 1, d), per_b),
                  pl.BlockSpec((1, d), const),
                  pl.BlockSpec((1, 1, d), per_b),
                  pl.BlockSpec((1, 1, d), per_b),
                  pl.BlockSpec((d, n), const)],
        out_specs=[pl.BlockSpec((1, tm, d), row)] + [pl.BlockSpec((1, tm, 512), row)] * 3
        + [pl.BlockSpec((1, tm, 1024), row)] * 2,
        out_shape=[jax.ShapeDtypeStruct((b, s, d), F32)] + [jax.ShapeDtypeStruct((b, s, 512), BF16)] * 3
        + [jax.ShapeDtypeStruct((b, s, 1024), BF16)] * 2,
        compiler_params=_params(("arbitrary", "arbitrary")),
        name="outproj_inproj_cd",
    )(ya, yb, sz0, w_out, x, gate, g, s1p, sh, w_in)


def _inproj_kv_body(x_ref, g_ref, s1p_ref, sh_ref, w_ref, k_ref, v_ref):
    u = _dot(_modulated_norm(x_ref, g_ref, s1p_ref, sh_ref, slice(None)), w_ref[...])
    k_ref[0] = u[:, 0:512].astype(BF16)
    v_ref[0] = _pairs_with_ones(u[:, 512:1024]).astype(BF16)


def _inproj_kv(x, g, s1p, sh, w, col_block):
    b, s, d = x.shape
    n = 1024
    row = lambda bi: (bi, 0, 0)
    const = lambda bi: (0, 0)
    return pl.pallas_call(
        _inproj_kv_body,
        grid=(b,),
        in_specs=[pl.BlockSpec((1, s, d), row),
                  pl.BlockSpec((1, d), const),
                  pl.BlockSpec((1, 1, d), row),
                  pl.BlockSpec((1, 1, d), row),
                  pl.BlockSpec((d, n), lambda bi: (0, col_block))],
        out_specs=[pl.BlockSpec((1, s, 512), row), pl.BlockSpec((1, s, 1024), row)],
        out_shape=[jax.ShapeDtypeStruct((b, s, 512), BF16), jax.ShapeDtypeStruct((b, s, 1024), BF16)],
        compiler_params=_params(("arbitrary",)),
        name="inproj_kv_ctx",
    )(x, g, s1p, sh, w)


def _split_heads(q):
    lane = lax.broadcasted_iota(jnp.int32, q.shape, 1)
    zero = jnp.zeros_like(q)
    return jnp.concatenate([jnp.where(lane < HEAD_DIM, q, zero), jnp.where(lane >= HEAD_DIM, q, zero)], axis=0)


def _with_ones(v):
    lane = lax.broadcasted_iota(jnp.int32, v.shape, 1)
    one = jnp.ones_like(v)
    return jnp.concatenate([jnp.where(lane < HEAD_DIM, v, one), jnp.where(lane < HEAD_DIM, one, v)], axis=1)


def _pairs_with_ones(v):
    return jnp.concatenate([_with_ones(v[:, c:c + LANES]) for c in range(0, v.shape[1], LANES)], axis=1)


def _online_softmax_pv(q2, chunks):
    rows = q2.shape[0] // 2
    m = acc = None
    for k, v2, bias in chunks:
        s = _dot_nt(q2, k)
        if bias is not None:
            s = s + bias
        mc = jnp.max(s, axis=-1, keepdims=True)
        m_new = mc if m is None else jnp.maximum(m, mc)
        p = jnp.exp2(s - m_new).astype(BF16)
        pv = jnp.concatenate([_dot(p[:rows], v2[:, :LANES]), _dot(p[rows:], v2[:, LANES:])], axis=0)
        acc = pv if m is None else jnp.exp2(m - m_new) * acc + pv
        m = m_new
    left = lax.broadcasted_iota(jnp.int32, (rows, LANES), 1) < HEAD_DIM
    top, bot = acc[:rows], acc[rows:]
    return jnp.where(left, top, bot) / pltpu.roll(jnp.where(left, bot, top), HEAD_DIM, 1)


def _key_chunks(k_ref, v_ref, kv, chunk):
    n = k_ref.shape[1]
    kl, vl = slice(kv * LANES, (kv + 1) * LANES), slice(kv * 2 * LANES, (kv + 1) * 2 * LANES)
    return [(k_ref[0, c:c + chunk, kl], v_ref[0, c:c + chunk, vl], None) for c in range(0, n, chunk)]


def _gqa_body(n_src, chunk, q_ref, *refs):
    k_refs, v_refs, o_ref = refs[:n_src], refs[n_src:2 * n_src], refs[2 * n_src]
    pairs = q_ref.shape[2] // LANES
    kv_heads = k_refs[0].shape[2] // LANES
    for p in range(pairs):
        lanes = slice(p * LANES, (p + 1) * LANES)
        kv = p * kv_heads // pairs
        chunks = [c for k, v in zip(k_refs, v_refs) for c in _key_chunks(k, v, kv, min(chunk, k.shape[1]))]
        o_ref[0, :, lanes] = _online_softmax_pv(_split_heads(q_ref[0, :, lanes]), chunks).astype(o_ref.dtype)


def _gqa_attention(q, ks, vs, tq, pairs, chunk=512):
    b, lq, w = q.shape
    n_src = len(ks)
    n_pairs = w // LANES
    kv_heads = ks[0].shape[2] // LANES
    kv_blk = max(1, pairs * kv_heads // n_pairs)
    kv_idx = lambda bi, i, j: (bi, 0, j * pairs * kv_heads // n_pairs // kv_blk)
    k_specs = [pl.BlockSpec((1, k.shape[1], kv_blk * LANES), kv_idx) for k in ks]
    v_specs = [pl.BlockSpec((1, v.shape[1], kv_blk * 2 * LANES), kv_idx) for v in vs]
    return pl.pallas_call(
        functools.partial(_gqa_body, n_src, chunk),
        grid=(b, lq // tq, n_pairs // pairs),
        in_specs=[pl.BlockSpec((1, tq, pairs * LANES), lambda bi, i, j: (bi, i, j))] + k_specs + v_specs,
        out_specs=pl.BlockSpec((1, tq, pairs * LANES), lambda bi, i, j: (bi, i, j)),
        out_shape=jax.ShapeDtypeStruct((b, lq, w), BF16),
        compiler_params=_params(("arbitrary", "arbitrary", "arbitrary")),
        name=f"gqa_attention_{n_src}src",
    )(q, *ks, *vs)


def _conv_body(rows, x_ref, prev_ref, next_ref, w_ref, cb_ref, lg_ref, lb_ref, o_ref, pad_ref, shift_ref):
    i = pl.program_id(1)
    last = pl.num_programs(1) - 1
    pad_ref[0:HALO, :] = jnp.where(i > 0, prev_ref[0], 0.0)
    pad_ref[HALO:HALO + rows, :] = x_ref[0]
    pad_ref[HALO + rows:HALO + rows + HALO, :] = jnp.where(i < last, next_ref[0], 0.0)
    span = shift_ref.shape[1]
    for r in range(1, SUBLANES):
        shift_ref[r] = pad_ref[r:r + span, :]
    sub = 32
    base = HALO - CONV_WIDTH // 2
    for r0 in range(0, rows, sub):
        acc = jnp.zeros((sub, x_ref.shape[2]), F32) + cb_ref[...]
        for k in range(CONV_WIDTH):
            a, r = divmod(base + k, SUBLANES)
            lo = a * SUBLANES + r0
            tap = pad_ref[lo:lo + sub, :] if r == 0 else shift_ref[r, lo:lo + sub, :]
            acc = acc + tap * jnp.concatenate([w_ref[k]] * (sub // SUBLANES), axis=0)
        mu = jnp.mean(acc, axis=-1, keepdims=True)
        xc = acc - mu
        y = xc * lax.rsqrt(jnp.mean(xc * xc, axis=-1, keepdims=True) + EPS) * lg_ref[...] + lb_ref[...]
        o_ref[0, r0:r0 + sub, :] = _silu(y).astype(o_ref.dtype)


def _conformer_conv(u, conv_w, conv_b, ln_g, ln_b, rows):
    b, l, c = u.shape
    hb = rows // HALO
    n_halo = l // HALO
    return pl.pallas_call(
        functools.partial(_conv_body, rows),
        grid=(b, l // rows),
        in_specs=[pl.BlockSpec((1, rows, c), lambda bi, i: (bi, i, 0)),
                  pl.BlockSpec((1, HALO, c), lambda bi, i: (bi, jnp.maximum(i * hb - 1, 0), 0)),
                  pl.BlockSpec((1, HALO, c), lambda bi, i: (bi, jnp.minimum((i + 1) * hb, n_halo - 1), 0)),
                  pl.BlockSpec((CONV_WIDTH, SUBLANES, c), lambda bi, i: (0, 0, 0)),
                  pl.BlockSpec((1, c), lambda bi, i: (0, 0)),
                  pl.BlockSpec((1, c), lambda bi, i: (0, 0)),
                  pl.BlockSpec((1, c), lambda bi, i: (0, 0))],
        out_specs=pl.BlockSpec((1, rows, c), lambda bi, i: (bi, i, 0)),
        out_shape=jax.ShapeDtypeStruct((b, l, c), BF16),
        scratch_shapes=[pltpu.VMEM((rows + 2 * HALO, c), F32),
                        pltpu.VMEM((SUBLANES, rows + 2 * HALO - SUBLANES, c), F32)],
        compiler_params=_params(("arbitrary", "arbitrary")),
        name="conformer_conv",
    )(u, u, u, jnp.broadcast_to(conv_w[:, None, :], (CONV_WIDTH, SUBLANES, c)),
      conv_b.reshape(1, c), ln_g.reshape(1, c), ln_b.reshape(1, c))


def _outproj_body(final, ya_ref, yb_ref, sz_ref, w_ref, x_ref, gate_ref, *rest):
    if final:
        perm_ref, fg_ref, o_ref = rest
        stacked = jnp.concatenate([ya_ref[0, j] for j in range(ya_ref.shape[1])], axis=0)
        ya = _dot(perm_ref[...], stacked)
    else:
        ya = ya_ref[0].astype(F32)
    y = jnp.concatenate([ya, yb_ref[0].astype(F32)], axis=-1) * sz_ref[0].astype(F32)
    xn = x_ref[0] + gate_ref[0] * _dot(y.astype(BF16), w_ref[...])
    if final:
        o_ref[0] = xn * lax.rsqrt(jnp.mean(xn * xn, axis=-1, keepdims=True) + EPS) * fg_ref[...]
    else:
        rest[0][0] = xn


def _outproj(ya, yb, sz, w, x, gate, final_g, tm):
    b, s, d = x.shape
    row = lambda bi, i: (bi, i, 0)
    const = lambda bi, i: (0, 0)
    per_b = lambda bi, i: (bi, 0, 0)
    final = final_g is not None
    ya_spec = (pl.BlockSpec((1, DFT_RADIX, tm // DFT_RADIX, 512), lambda bi, i: (bi, 0, i, 0)) if final
               else pl.BlockSpec((1, tm, 512), row))
    in_specs = [ya_spec,
                pl.BlockSpec((1, tm, 512), row),
                pl.BlockSpec((1, tm, 1024), row),
                pl.BlockSpec((1024, d), const),
                pl.BlockSpec((1, tm, d), row),
                pl.BlockSpec((1, 1, d), per_b)]
    args = [ya, yb, sz, w, x, gate]
    if final:
        in_specs += [pl.BlockSpec((tm, tm), const), pl.BlockSpec((1, d), const)]
        args += [jnp.asarray(_class_row_permutation(tm)).astype(BF16), final_g]
    return pl.pallas_call(
        functools.partial(_outproj_body, final),
        grid=(b, s // tm),
        in_specs=in_specs,
        out_specs=pl.BlockSpec((1, tm, d), row),
        out_shape=jax.ShapeDtypeStruct((b, s, d), F32),
        compiler_params=_params(("arbitrary", "arbitrary")),
        name="outproj_final" if final_g is not None else "outproj",
    )(*args)


def _fourier_w_body(cc_ref, sc_ref, w_ref, o_ref):
    for g in range(w_ref.shape[0]):
        wc = jnp.dot(cc_ref[...], w_ref[g], preferred_element_type=F32, precision=HIGHEST)
        ws = jnp.dot(sc_ref[...], w_ref[g], preferred_element_type=F32, precision=HIGHEST)
        o_ref[g] = jnp.concatenate([wc, ws], axis=-1).astype(o_ref.dtype)


def _fourier_weights(cc, sc, w):
    g, c, _ = w.shape
    return pl.pallas_call(
        _fourier_w_body,
        out_shape=jax.ShapeDtypeStruct((g, c, 2 * c), BF16),
        name="fourier_weights",
    )(cc, sc, w)


def _fourier_tables(seq):
    quarter = seq // DFT_RADIX
    m = np.arange(quarter)
    tabs = []
    for j in range(DFT_RADIX):
        ang = 2.0 * np.pi * (((DFT_RADIX * m + j)[:, None] * m[None, :]) % seq) / seq
        tabs.append(np.concatenate([np.cos(ang), -np.sin(ang)], axis=1) / math.sqrt(seq))
    return np.stack(tabs).astype(np.float32)


def _class_row_permutation(rows):
    per = rows // DFT_RADIX
    p = np.zeros((rows, rows), np.float32)
    for j in range(DFT_RADIX):
        p[DFT_RADIX * np.arange(per) + j, j * per + np.arange(per)] = 1.0
    return p


def _fourier_body(f_ref, w1_ref, tab_ref, o_ref, yc_ref, ys_ref):
    groups, c, _ = w1_ref.shape
    quarter = f_ref.shape[1] // DFT_RADIX
    for g in range(groups):
        yg = _dot(f_ref[0, :, g * c:(g + 1) * c], w1_ref[g])
        yc_ref[:, g * c:(g + 1) * c] = yg[:, :c]
        ys_ref[:, g * c:(g + 1) * c] = yg[:, c:]
    for j in range(DFT_RADIX):
        pc, ps = [], []
        for q in range(DFT_RADIX):
            rows = slice(q * quarter, (q + 1) * quarter)
            a = (j * q) % 4
            if a == 0:
                pc.append((1, yc_ref, rows)), ps.append((1, ys_ref, rows))
            elif a == 1:
                pc.append((-1, ys_ref, rows)), ps.append((1, yc_ref, rows))
            elif a == 2:
                pc.append((-1, yc_ref, rows)), ps.append((-1, ys_ref, rows))
            else:
                pc.append((1, ys_ref, rows)), ps.append((-1, yc_ref, rows))
        folded = []
        for terms in (pc, ps):
            terms = sorted(terms, key=lambda t: -t[0])
            acc = terms[0][1][terms[0][2], :]
            for sign, ref, rows in terms[1:]:
                acc = acc + ref[rows, :] if sign > 0 else acc - ref[rows, :]
            folded.append(acc.astype(BF16))
        o_ref[0, j] = _dot(tab_ref[j], jnp.concatenate(folded, axis=0)).astype(o_ref.dtype)


def _fourier_mix(f, w1, tab):
    b, s, w = f.shape
    quarter = s // DFT_RADIX
    return pl.pallas_call(
        _fourier_body,
        grid=(b,),
        in_specs=[pl.BlockSpec((1, s, w), lambda bi: (bi, 0, 0)),
                  pl.BlockSpec(w1.shape, lambda bi: (0, 0, 0)),
                  pl.BlockSpec(tab.shape, lambda bi: (0, 0, 0))],
        out_specs=pl.BlockSpec((1, DFT_RADIX, quarter, w), lambda bi: (bi, 0, 0, 0)),
        out_shape=jax.ShapeDtypeStruct((b, DFT_RADIX, quarter, w), BF16),
        scratch_shapes=[pltpu.VMEM((s, w), F32), pltpu.VMEM((s, w), F32)],
        compiler_params=_params(("arbitrary",)),
        name="fourier_mix",
    )(f, w1, tab)


def _na_tile_geometry(rows_total):
    last_r0 = rows_total - NA_TILE_ROWS
    return ((0, 0), (2 * NA_TILE_ROWS, 2 * NA_TILE_ROWS - NA_ROWS // 2), (last_r0, rows_total - NA_WIN_ROWS))


def _na_bias_body(rows_total, rpb_ref, o_ref, t2_ref):
    h = pl.program_id(0)
    wq = lax.broadcasted_iota(jnp.int32, (GRID_W, LANES), 0)
    wk = lax.broadcasted_iota(jnp.int32, (GRID_W, LANES), 1) & (GRID_W - 1)
    cs = jnp.clip(wq - NA_COLS // 2, 0, GRID_W - NA_COLS)
    col_ok = (wk >= cs) & (wk < cs + NA_COLS)
    dc = wk - wq + (NA_COLS - 1)
    n_dr, n_dc = 2 * NA_ROWS - 1, 2 * NA_COLS - 1
    neg = jnp.full((GRID_W, LANES), NEG_INF, F32)
    for dr in range(n_dr):
        val = neg
        for c in range(n_dc):
            val = jnp.where(col_ok & (dc == c), rpb_ref[h * (n_dr * n_dc) + dr * n_dc + c] * LOG2E, val)
        t2_ref[dr] = val
    lane = lax.broadcasted_iota(jnp.int32, (GRID_W, LANES), 1)
    left = lane < GRID_W
    for ty, (r0, ws) in enumerate(_na_tile_geometry(rows_total)):
        for i in range(NA_TILE_ROWS):
            r = r0 + i
            rs = min(max(r - NA_ROWS // 2, 0), rows_total - NA_ROWS)
            for jp in range(NA_WIN_ROWS // 2):
                halves = []
                for kr in (ws + 2 * jp, ws + 2 * jp + 1):
                    halves.append(t2_ref[kr - r + NA_ROWS - 1] if rs <= kr < rs + NA_ROWS else neg)
                o_ref[0, ty, i * GRID_W:(i + 1) * GRID_W, jp * LANES:(jp + 1) * LANES] = jnp.where(left, halves[0], halves[1])


def _na_bias_tables(rpb, rows_total):
    h = rpb.shape[0]
    q_rows, k_cols = NA_TILE_ROWS * GRID_W, NA_WIN_ROWS * GRID_W
    return pl.pallas_call(
        functools.partial(_na_bias_body, rows_total),
        grid=(h,),
        in_specs=[pl.BlockSpec(memory_space=pltpu.SMEM)],
        out_specs=pl.BlockSpec((1, 3, q_rows, k_cols), lambda hi: (hi, 0, 0, 0)),
        out_shape=jax.ShapeDtypeStruct((h, 3, q_rows, k_cols), F32),
        scratch_shapes=[pltpu.VMEM((2 * NA_ROWS - 1, GRID_W, LANES), F32)],
        compiler_params=_params(("arbitrary",)),
        name="na_bias_tables",
    )(rpb.reshape(-1))


def _na_body(rows_total, q_ref, k_ref, v_ref, kc_ref, vc_ref, bias_ref, o_ref):
    t = pl.program_id(2)
    n_t = pl.num_programs(2)
    ty = jnp.where(t == 0, 0, jnp.where(t == n_t - 1, 2, 1))
    ws = jnp.clip(t * NA_TILE_ROWS - NA_ROWS // 2, 0, rows_total - NA_WIN_ROWS)
    off = pl.multiple_of(ws * GRID_W, GRID_W)
    n_win = NA_WIN_ROWS * GRID_W
    bias = jnp.concatenate([bias_ref[0, ty], bias_ref[1, ty]], axis=0)
    for bb in range(q_ref.shape[0]):
        q2 = _split_heads(q_ref[bb])
        o = _online_softmax_pv(q2, [(kc_ref[bb], vc_ref[bb], None),
                                    (k_ref[bb, pl.ds(off, n_win), :], v_ref[bb, pl.ds(off, n_win), :], bias)])
        o_ref[bb] = o.astype(o_ref.dtype)


def _neighborhood_attention(q, k, v, kc, vc, bias):
    b, s, w = q.shape
    l = kc.shape[1]
    rows_total = s // GRID_W
    tq = NA_TILE_ROWS * GRID_W
    nb = NA_BATCH_PER_STEP
    full = lambda j, bi, t: (bi, 0, j)
    return pl.pallas_call(
        functools.partial(_na_body, rows_total),
        grid=(w // LANES, b // nb, s // tq),
        in_specs=[pl.BlockSpec((nb, tq, LANES), lambda j, bi, t: (bi, t, j)),
                  pl.BlockSpec((nb, s, LANES), full),
                  pl.BlockSpec((nb, s, 2 * LANES), full),
                  pl.BlockSpec((nb, l, LANES), full),
                  pl.BlockSpec((nb, l, 2 * LANES), full),
                  pl.BlockSpec((2,) + bias.shape[1:], lambda j, bi, t: (j, 0, 0, 0))],
        out_specs=pl.BlockSpec((nb, tq, LANES), lambda j, bi, t: (bi, t, j)),
        out_shape=jax.ShapeDtypeStruct((b, s, w), BF16),
        compiler_params=_params(("arbitrary", "arbitrary", "arbitrary")),
        name="neighborhood_attention",
    )(q, k, v, kc, vc, bias)


def _modulation(m, rows, batch):
    d = m.shape[1] // 3
    pick = lambda lo: jnp.broadcast_to(m[rows, lo:lo + d].reshape(-1, 1, d), (batch, 1, d))
    return pick(0), 1.0 + pick(d), pick(2 * d)


def kernel(x, c, ctx, c_ctx, ab_w_ada, ab_b_ada, ab_norm_g, ab_w_in, ab_conv_w, ab_conv_b, ab_ln_g, ab_ln_b, ab_q_norm_g, ab_k_norm_g, ab_w_out, cd_w_ada, cd_b_ada, cd_norm_g, cd_w_in, cd_w_fourier, cd_rpb, cd_w_out, final_norm_g):
    b, s, d = x.shape
    l = ctx.shape[1]
    assert d == D_MODEL and s % (NA_TILE_ROWS * GRID_W) == 0 and ab_w_ada.shape[0] == 1 and cd_w_ada.shape[0] == 1

    cond = jnp.zeros((16, d), F32).at[:b].set(c).at[b].set(c_ctx)
    lat = slice(0, b)
    cx = slice(b, b + 1)

    m0 = _adaln(cond, ab_w_ada[0], ab_b_ada[0])
    sh, s1p, gate = _modulation(m0, lat, b)
    csh, cs1p, cgate = _modulation(m0, cx, b)

    w_in0 = ab_w_in[0].astype(BF16)
    w_out0 = ab_w_out[0].astype(BF16)

    cos, sin = _rope_tables(s)
    cos, sin = jnp.asarray(cos), jnp.asarray(sin)
    mq = jnp.asarray(_block_ones(512)).astype(BF16)
    mk = jnp.asarray(_block_ones(128)).astype(BF16)
    qg = jnp.tile(ab_q_norm_g[0], 512 // HEAD_DIM).reshape(1, 512)
    kg = jnp.tile(ab_k_norm_g[0], 128 // HEAD_DIM).reshape(1, 128)
    ng0 = ab_norm_g[0].reshape(1, d)

    glu, q, k, v, sz = _inproj_ab(x, ng0, s1p, sh, w_in0, qg, kg, cos, sin, mq, mk, True, 1024)
    cglu, cq, ck, cv, csz = _inproj_ab(ctx, ng0, cs1p, csh, w_in0, qg, kg, cos[:l], sin[:l], mq, mk, False, l)

    y_b = _gqa_attention(q, [ck, k], [cv, v], 1024, 1)
    y_a = _conformer_conv(glu, ab_conv_w[0], ab_conv_b[0], ab_ln_g[0], ab_ln_b[0], 256)

    yc_b = _gqa_attention(cq, [ck], [cv], l, 4)
    yc_a = _conformer_conv(cglu, ab_conv_w[0], ab_conv_b[0], ab_ln_g[0], ab_ln_b[0], l)
    xc1 = _outproj(yc_a, yc_b, csz, w_out0, ctx, cgate, None, l)

    m1 = _adaln(cond, cd_w_ada[0], cd_b_ada[0])
    sh1, s1p1, gate1 = _modulation(m1, lat, b)
    csh, cs1p, _ = _modulation(m1, cx, b)
    ng1 = cd_norm_g[0].reshape(1, d)
    w_in1 = cd_w_in[0].astype(BF16)

    x1, f, q, k, v, sz = _outproj_inproj_cd(y_a, y_b, sz, w_out0, x, gate, ng1, s1p1, sh1, w_in1, 512)
    gate = gate1
    kc, vc = _inproj_kv(xc1, ng1, cs1p, csh, w_in1, 1)

    c_c, s_c = _dft_tables(cd_w_fourier.shape[-1])
    w1 = _fourier_weights(jnp.asarray(c_c), jnp.asarray(s_c), cd_w_fourier[0])
    y_c = _fourier_mix(f, w1, jnp.asarray(_fourier_tables(s)).astype(BF16))

    bias = _na_bias_tables(cd_rpb[0], s // GRID_W)
    y_d = _neighborhood_attention(q, k, v, kc, vc, bias)

    return _outproj(y_c, y_d, sz, cd_w_out[0].astype(BF16), x1, gate, final_norm_g.reshape(1, d), 512)
```

```python
import functools
import math

import numpy as np
import jax
import jax.numpy as jnp
from jax import lax
from jax.experimental import pallas as pl
from jax.experimental.pallas import tpu as pltpu

F32 = jnp.float32
BF16 = jnp.bfloat16
HIGHEST = lax.Precision.HIGHEST

D_MODEL = 1024
HEAD_DIM = 64
GRID_W = 64
CONV_WIDTH = 31
NA_ROWS = 8
NA_COLS = 16
ROPE_THETA = 10000.0
EPS = 1e-6
NEG_INF = -1e30
LOG2E = math.log2(math.e)
Q_SCALE = HEAD_DIM ** -0.5 * LOG2E
LANES = 128
SUBLANES = 8
HALO = 16
VMEM_LIMIT = 48 * 1024 * 1024

NA_TILE_ROWS = 4
NA_WIN_ROWS = NA_TILE_ROWS + NA_ROWS
NA_BATCH_PER_STEP = 8
IN_PROJ_SUBTILE = 256
DFT_RADIX = 4

def _params(sem, flags=None):
    return pltpu.CompilerParams(dimension_semantics=sem, vmem_limit_bytes=VMEM_LIMIT, flags=flags)


def _silu(t):
    return t * jax.nn.sigmoid(t)


def _dot(a, b):
    return jnp.dot(a, b, preferred_element_type=F32)


def _dot_nt(a, b):
    return lax.dot_general(a, b, (((1,), (1,)), ((), ())), preferred_element_type=F32)


def _rope_tables(seq):
    t = np.arange(seq)
    row = (t // GRID_W).astype(np.float64)
    col = (t % GRID_W).astype(np.float64)
    nf = HEAD_DIM // 4
    inv = np.float32(ROPE_THETA) ** (-np.arange(nf, dtype=np.float32) / nf)
    ang = np.concatenate([row[:, None] * inv, col[:, None] * inv], axis=-1)
    cos = np.repeat(np.cos(ang), 2, axis=-1)
    sin = np.repeat(np.sin(ang), 2, axis=-1)
    sign = np.tile(np.array([-1.0, 1.0]), HEAD_DIM // 2)
    cos = np.tile(cos, (1, LANES // HEAD_DIM))
    sin = np.tile(sin * sign, (1, LANES // HEAD_DIM))
    return cos.astype(np.float32), sin.astype(np.float32)


def _dft_tables(n):
    k = np.arange(n)
    ang = 2.0 * np.pi * ((k[:, None] * k[None, :]) % n) / n
    s = 1.0 / math.sqrt(n)
    return (np.cos(ang) * s).astype(np.float32), (np.sin(ang) * s).astype(np.float32)


def _block_ones(width):
    h = np.arange(width) // HEAD_DIM
    return (h[:, None] == h[None, :]).astype(np.float32)


def _adaln_body(c_ref, w_ref, b_ref, o_ref):
    a = _silu(c_ref[...])
    o_ref[...] = jnp.dot(a, w_ref[...], preferred_element_type=F32, precision=HIGHEST) + b_ref[...]


def _adaln(cond, w, b):
    r, d = cond.shape
    n = w.shape[1]
    tn = 1024
    return pl.pallas_call(
        _adaln_body,
        grid=(n // tn,),
        in_specs=[pl.BlockSpec((r, d), lambda j: (0, 0)),
                  pl.BlockSpec((d, tn), lambda j: (0, j)),
                  pl.BlockSpec((1, tn), lambda j: (0, j))],
        out_specs=pl.BlockSpec((r, tn), lambda j: (0, j)),
        out_shape=jax.ShapeDtypeStruct((r, n), F32),
        compiler_params=_params(("arbitrary",)),
        name="adaln",
    )(cond, w, b.reshape(1, n))


def _head_rms(t, ones_ref, gain):
    ms = _dot((t * t).astype(BF16), ones_ref[...]) * (1.0 / HEAD_DIM)
    return t * lax.rsqrt(ms + EPS) * gain


def _rope(t, cos, sin_signed):
    rows = t.shape[0]
    lane = lax.broadcasted_iota(jnp.int32, (rows, LANES), 1)
    even = (lane & 1) == 0
    outs = []
    for c in range(t.shape[1] // LANES):
        xc = t[:, c * LANES:(c + 1) * LANES]
        swapped = jnp.where(even, pltpu.roll(xc, LANES - 1, 1), pltpu.roll(xc, 1, 1))
        outs.append(xc * cos + swapped * sin_signed)
    return outs[0] if len(outs) == 1 else jnp.concatenate(outs, axis=1)


def _modulate(x, g_ref, s1p_ref, sh_ref):
    y = x * lax.rsqrt(jnp.mean(x * x, axis=-1, keepdims=True) + EPS) * g_ref[...]
    return (y * s1p_ref[0] + sh_ref[0]).astype(BF16)


def _modulated_norm(x_ref, g_ref, s1p_ref, sh_ref, rows):
    return _modulate(x_ref[0, rows, :], g_ref, s1p_ref, sh_ref)


def _row_subtiles(tm):
    sub = min(tm, IN_PROJ_SUBTILE)
    return [slice(r, r + sub) for r in range(0, tm, sub)]


def _inproj_ab_body(rope, x_ref, g_ref, s1p_ref, sh_ref, w_ref, qg_ref, kg_ref, cos_ref, sin_ref, mq_ref, mk_ref,
                    glu_ref, q_ref, k_ref, v_ref, sz_ref):
    for rows in _row_subtiles(x_ref.shape[1]):
        u = _dot(_modulated_norm(x_ref, g_ref, s1p_ref, sh_ref, rows), w_ref[...])
        tabs = (cos_ref[rows, :], sin_ref[rows, :]) if rope else None
        glu, q, k, v, sz = _ab_epilogue(u, mq_ref, mk_ref, qg_ref, kg_ref, tabs)
        glu_ref[0, rows, :] = glu
        q_ref[0, rows, :] = q
        k_ref[0, rows, :] = k
        v_ref[0, rows, :] = v
        sz_ref[0, rows, :] = sz


def _ab_epilogue(u, mq_ref, mk_ref, qg_ref, kg_ref, rope_tabs):
    glu = u[:, 0:512] * jax.nn.sigmoid(u[:, 512:1024])
    q = _head_rms(u[:, 1024:1536], mq_ref, qg_ref[...])
    k = _head_rms(u[:, 1536:1664], mk_ref, kg_ref[...])
    if rope_tabs is not None:
        q = _rope(q, *rope_tabs)
        k = _rope(k, *rope_tabs)
    v = u[:, 1664:1792]
    k_sw, v_sw = pltpu.roll(k, HEAD_DIM, 1), pltpu.roll(v, HEAD_DIM, 1)
    left = lax.broadcasted_iota(jnp.int32, k.shape, 1) < HEAD_DIM
    one = jnp.ones_like(v)
    k_dup = jnp.concatenate([jnp.where(left, k, k_sw), jnp.where(left, k_sw, k)], axis=1)
    v_dup = jnp.concatenate([jnp.where(left, v, one), jnp.where(left, one, v_sw),
                             jnp.where(left, v_sw, one), jnp.where(left, one, v)], axis=1)
    return (glu, (q * Q_SCALE).astype(BF16), k_dup.astype(BF16), v_dup.astype(BF16),
            _silu(u[:, 1792:2816]).astype(BF16))


def _inproj_ab(x, g, s1p, sh, w, qg, kg, cos, sin, mq, mk, rope, tm):
    b, s, d = x.shape
    n = w.shape[1]
    row = lambda bi, i: (bi, i, 0)
    const = lambda bi, i: (0, 0)
    per_b = lambda bi, i: (bi, 0, 0)
    tab = lambda bi, i: (i, 0)
    return pl.pallas_call(
        functools.partial(_inproj_ab_body, rope),
        grid=(b, s // tm),
        in_specs=[pl.BlockSpec((1, tm, d), row),
                  pl.BlockSpec((1, d), const),
                  pl.BlockSpec((1, 1, d), per_b),
                  pl.BlockSpec((1, 1, d), per_b),
                  pl.BlockSpec((d, n), const),
                  pl.BlockSpec((1, 512), const),
                  pl.BlockSpec((1, 128), const),
                  pl.BlockSpec((tm, LANES), tab),
                  pl.BlockSpec((tm, LANES), tab),
                  pl.BlockSpec((512, 512), const),
                  pl.BlockSpec((128, 128), const)],
        out_specs=[pl.BlockSpec((1, tm, 512), row),
                   pl.BlockSpec((1, tm, 512), row),
                   pl.BlockSpec((1, tm, 256), row),
                   pl.BlockSpec((1, tm, 512), row),
                   pl.BlockSpec((1, tm, 1024), row)],
        out_shape=[jax.ShapeDtypeStruct((b, s, 512), F32),
                   jax.ShapeDtypeStruct((b, s, 512), BF16),
                   jax.ShapeDtypeStruct((b, s, 256), BF16),
                   jax.ShapeDtypeStruct((b, s, 512), BF16),
                   jax.ShapeDtypeStruct((b, s, 1024), BF16)],
        compiler_params=_params(("arbitrary", "arbitrary")),
        name="inproj_ab_rope" if rope else "inproj_ab_ctx",
    )(x, g, s1p, sh, w, qg, kg, cos, sin, mq, mk)


def _outproj_inproj_cd_body(ya_ref, yb_ref, sz0_ref, wo_ref, x_ref, gate_ref, g_ref, s1p_ref, sh_ref, wi_ref,
                            x1_ref, f_ref, q_ref, k_ref, v_ref, sz_ref):
    for rows in _row_subtiles(x_ref.shape[1]):
        y = (jnp.concatenate([ya_ref[0, rows, :], yb_ref[0, rows, :]], axis=-1).astype(F32)
             * sz0_ref[0, rows, :].astype(F32))
        xn = x_ref[0, rows, :] + gate_ref[0] * _dot(y.astype(BF16), wo_ref[...])
        x1_ref[0, rows, :] = xn
        u = _dot(_modulate(xn, g_ref, s1p_ref, sh_ref), wi_ref[...])
        f_ref[0, rows, :] = u[:, 0:512].astype(BF16)
        q_ref[0, rows, :] = (u[:, 512:1024] * Q_SCALE).astype(BF16)
        k_ref[0, rows, :] = u[:, 1024:1536].astype(BF16)
        v_ref[0, rows, :] = _pairs_with_ones(u[:, 1536:2048]).astype(BF16)
        sz_ref[0, rows, :] = _silu(u[:, 2048:3072]).astype(BF16)


def _outproj_inproj_cd(ya, yb, sz0, w_out, x, gate, g, s1p, sh, w_in, tm):
    b, s, d = x.shape
    n = w_in.shape[1]
    row = lambda bi, i: (bi, i, 0)
    const = lambda bi, i: (0, 0)
    per_b = lambda bi, i: (bi, 0, 0)
    return pl.pallas_call(
        _outproj_inproj_cd_body,
        grid=(b, s // tm),
        in_specs=[pl.BlockSpec((1, tm, 512), row),
                  pl.BlockSpec((1, tm, 512), row),
                  pl.BlockSpec((1, tm, 1024), row),
                  pl.BlockSpec((1024, d), const),
                  pl.BlockSpec((1, tm, d), row),
                  pl.BlockSpec((1, 1, d), per_b),
                  pl.BlockSpec((1, d), const),
                  pl.BlockSpec((1, 1, d), per_b),
                  pl.BlockSpec((1, 1, d), per_b),
                  pl.BlockSpec((d, n), const)],
        out_specs=[pl.BlockSpec((1, tm, d), row)] + [pl.BlockSpec((1, tm, 512), row)] * 3
        + [pl.BlockSpec((1, tm, 1024), row)] * 2,
        out_shape=[jax.ShapeDtypeStruct((b, s, d), F32)] + [jax.ShapeDtypeStruct((b, s, 512), BF16)] * 3
        + [jax.ShapeDtypeStruct((b, s, 1024), BF16)] * 2,
        compiler_params=_params(("arbitrary", "arbitrary")),
        name="outproj_inproj_cd",
    )(ya, yb, sz0, w_out, x, gate, g, s1p, sh, w_in)


def _inproj_kv_body(x_ref, g_ref, s1p_ref, sh_ref, w_ref, k_ref, v_ref):
    u = _dot(_modulated_norm(x_ref, g_ref, s1p_ref, sh_ref, slice(None)), w_ref[...])
    k_ref[0] = u[:, 0:512].astype(BF16)
    v_ref[0] = _pairs_with_ones(u[:, 512:1024]).astype(BF16)


def _inproj_kv(x, g, s1p, sh, w, col_block):
    b, s, d = x.shape
    n = 1024
    row = lambda bi: (bi, 0, 0)
    const = lambda bi: (0, 0)
    return pl.pallas_call(
        _inproj_kv_body,
        grid=(b,),
        in_specs=[pl.BlockSpec((1, s, d), row),
                  pl.BlockSpec((1, d), const),
                  pl.BlockSpec((1, 1, d), row),
                  pl.BlockSpec((1, 1, d), row),
                  pl.BlockSpec((d, n), lambda bi: (0, col_block))],
        out_specs=[pl.BlockSpec((1, s, 512), row), pl.BlockSpec((1, s, 1024), row)],
        out_shape=[jax.ShapeDtypeStruct((b, s, 512), BF16), jax.ShapeDtypeStruct((b, s, 1024), BF16)],
        compiler_params=_params(("arbitrary",)),
        name="inproj_kv_ctx",
    )(x, g, s1p, sh, w)


def _split_heads(q):
    lane = lax.broadcasted_iota(jnp.int32, q.shape, 1)
    zero = jnp.zeros_like(q)
    return jnp.concatenate([jnp.where(lane < HEAD_DIM, q, zero), jnp.where(lane >= HEAD_DIM, q, zero)], axis=0)


def _with_ones(v):
    lane = lax.broadcasted_iota(jnp.int32, v.shape, 1)
    one = jnp.ones_like(v)
    return jnp.concatenate([jnp.where(lane < HEAD_DIM, v, one), jnp.where(lane < HEAD_DIM, one, v)], axis=1)


def _pairs_with_ones(v):
    return jnp.concatenate([_with_ones(v[:, c:c + LANES]) for c in range(0, v.shape[1], LANES)], axis=1)


def _pv(p, v2, rows):
    pv = _dot(p.astype(BF16), v2)
    return jnp.concatenate([pv[:rows, :LANES], pv[rows:, LANES:]], axis=0)


def _online_softmax_pv(q2, chunks, online=True):
    rows = q2.shape[0] // 2
    scores = [_dot_nt(q2, k) if bias is None else _dot_nt(q2, k) + bias for k, _, bias in chunks]
    if online:
        m = acc = None
        for s, (_, v2, _) in zip(scores, chunks):
            mc = jnp.max(s, axis=-1, keepdims=True)
            m_new = mc if m is None else jnp.maximum(m, mc)
            pv = _pv(jnp.exp2(s - m_new), v2, rows)
            acc = pv if m is None else jnp.exp2(m - m_new) * acc + pv
            m = m_new
    else:
        m = functools.reduce(jnp.maximum, [jnp.max(s, axis=-1, keepdims=True) for s in scores])
        acc = functools.reduce(jnp.add, [_pv(jnp.exp2(s - m), v2, rows) for s, (_, v2, _) in zip(scores, chunks)])
    left = lax.broadcasted_iota(jnp.int32, (rows, LANES), 1) < HEAD_DIM
    top, bot = acc[:rows], acc[rows:]
    return jnp.where(left, top, bot) / pltpu.roll(jnp.where(left, bot, top), HEAD_DIM, 1)


def _key_chunks(k_ref, v_ref, kv, chunk):
    n = k_ref.shape[1]
    kl, vl = slice(kv * LANES, (kv + 1) * LANES), slice(kv * 2 * LANES, (kv + 1) * 2 * LANES)
    return [(k_ref[0, c:c + chunk, kl], v_ref[0, c:c + chunk, vl], None) for c in range(0, n, chunk)]


def _gqa_body(n_src, chunk, q_ref, *refs):
    k_refs, v_refs, o_ref = refs[:n_src], refs[n_src:2 * n_src], refs[2 * n_src]
    pairs = q_ref.shape[2] // LANES
    kv_heads = k_refs[0].shape[2] // LANES
    for p in range(pairs):
        lanes = slice(p * LANES, (p + 1) * LANES)
        kv = p * kv_heads // pairs
        chunks = [c for k, v in zip(k_refs, v_refs) for c in _key_chunks(k, v, kv, min(chunk, k.shape[1]))]
        o_ref[0, :, lanes] = _online_softmax_pv(_split_heads(q_ref[0, :, lanes]), chunks).astype(o_ref.dtype)


def _gqa_attention(q, ks, vs, tq, pairs, chunk=512):
    b, lq, w = q.shape
    n_src = len(ks)
    n_pairs = w // LANES
    kv_heads = ks[0].shape[2] // LANES
    kv_blk = max(1, pairs * kv_heads // n_pairs)
    kv_idx = lambda bi, i, j: (bi, 0, j * pairs * kv_heads // n_pairs // kv_blk)
    k_specs = [pl.BlockSpec((1, k.shape[1], kv_blk * LANES), kv_idx) for k in ks]
    v_specs = [pl.BlockSpec((1, v.shape[1], kv_blk * 2 * LANES), kv_idx) for v in vs]
    return pl.pallas_call(
        functools.partial(_gqa_body, n_src, chunk),
        grid=(b, lq // tq, n_pairs // pairs),
        in_specs=[pl.BlockSpec((1, tq, pairs * LANES), lambda bi, i, j: (bi, i, j))] + k_specs + v_specs,
        out_specs=pl.BlockSpec((1, tq, pairs * LANES), lambda bi, i, j: (bi, i, j)),
        out_shape=jax.ShapeDtypeStruct((b, lq, w), BF16),
        compiler_params=_params(("arbitrary", "arbitrary", "arbitrary")),
        name=f"gqa_attention_{n_src}src",
    )(q, *ks, *vs)


def _conv_core(rows, pad_ref, shift_ref, w_ref, cb_ref, lg_ref, lb_ref, store):
    span = shift_ref.shape[1]
    for r in range(1, SUBLANES):
        shift_ref[r] = pad_ref[r:r + span, :]
    sub = 32
    base = HALO - CONV_WIDTH // 2
    for r0 in range(0, rows, sub):
        acc = jnp.zeros((sub, pad_ref.shape[1]), F32) + cb_ref[...]
        for k in range(CONV_WIDTH):
            a, r = divmod(base + k, SUBLANES)
            lo = a * SUBLANES + r0
            tap = pad_ref[lo:lo + sub, :] if r == 0 else shift_ref[r, lo:lo + sub, :]
            acc = acc + tap * jnp.concatenate([w_ref[k]] * (sub // SUBLANES), axis=0)
        mu = jnp.mean(acc, axis=-1, keepdims=True)
        xc = acc - mu
        y = xc * lax.rsqrt(jnp.mean(xc * xc, axis=-1, keepdims=True) + EPS) * lg_ref[...] + lb_ref[...]
        store(r0, sub, _silu(y))


def _conv_body(rows, x_ref, prev_ref, next_ref, w_ref, cb_ref, lg_ref, lb_ref, o_ref, pad_ref, shift_ref):
    i = pl.program_id(1)
    last = pl.num_programs(1) - 1
    pad_ref[0:HALO, :] = jnp.where(i > 0, prev_ref[0], 0.0)
    pad_ref[HALO:HALO + rows, :] = x_ref[0]
    pad_ref[HALO + rows:HALO + rows + HALO, :] = jnp.where(i < last, next_ref[0], 0.0)

    def store(r0, sub, y):
        o_ref[0, r0:r0 + sub, :] = y.astype(o_ref.dtype)

    _conv_core(rows, pad_ref, shift_ref, w_ref, cb_ref, lg_ref, lb_ref, store)


def _conformer_conv(u, conv_w, conv_b, ln_g, ln_b, rows):
    b, l, c = u.shape
    hb = rows // HALO
    n_halo = l // HALO
    return pl.pallas_call(
        functools.partial(_conv_body, rows),
        grid=(b, l // rows),
        in_specs=[pl.BlockSpec((1, rows, c), lambda bi, i: (bi, i, 0)),
                  pl.BlockSpec((1, HALO, c), lambda bi, i: (bi, jnp.maximum(i * hb - 1, 0), 0)),
                  pl.BlockSpec((1, HALO, c), lambda bi, i: (bi, jnp.minimum((i + 1) * hb, n_halo - 1), 0)),
                  pl.BlockSpec((CONV_WIDTH, SUBLANES, c), lambda bi, i: (0, 0, 0)),
                  pl.BlockSpec((1, c), lambda bi, i: (0, 0)),
                  pl.BlockSpec((1, c), lambda bi, i: (0, 0)),
                  pl.BlockSpec((1, c), lambda bi, i: (0, 0))],
        out_specs=pl.BlockSpec((1, rows, c), lambda bi, i: (bi, i, 0)),
        out_shape=jax.ShapeDtypeStruct((b, l, c), BF16),
        scratch_shapes=[pltpu.VMEM((rows + 2 * HALO, c), F32),
                        pltpu.VMEM((SUBLANES, rows + 2 * HALO - SUBLANES, c), F32)],
        compiler_params=_params(("arbitrary", "arbitrary")),
        name="conformer_conv",
    )(u, u, u, jnp.broadcast_to(conv_w[:, None, :], (CONV_WIDTH, SUBLANES, c)),
      conv_b.reshape(1, c), ln_g.reshape(1, c), ln_b.reshape(1, c))


def _ctx_layer_body(x_ref, g0_ref, s1p0_ref, sh0_ref, wi0_ref, qg_ref, kg_ref, mq_ref, mk_ref, cw_ref, cb_ref,
                    lg_ref, lb_ref, wo0_ref, gate0_ref, g1_ref, s1p1_ref, sh1_ref, wkv1_ref,
                    ck_ref, cv_ref, kc_ref, vc_ref, pad_ref, shift_ref, ya_ref):
    rows = x_ref.shape[1]
    x = x_ref[0]
    u = _dot(_modulate(x, g0_ref, s1p0_ref, sh0_ref), wi0_ref[...])
    glu, q, k_dup, v_dup, sz = _ab_epilogue(u, mq_ref, mk_ref, qg_ref, kg_ref, None)
    ck_ref[0] = k_dup
    cv_ref[0] = v_dup
    pairs, kv_heads = q.shape[1] // LANES, k_dup.shape[1] // LANES
    yb = []
    for p in range(pairs):
        kv = p * kv_heads // pairs
        keys = [(k_dup[:, kv * LANES:(kv + 1) * LANES], v_dup[:, kv * 2 * LANES:(kv + 1) * 2 * LANES], None)]
        yb.append(_online_softmax_pv(_split_heads(q[:, p * LANES:(p + 1) * LANES]), keys, online=False))
    zeros = jnp.zeros((HALO, glu.shape[1]), F32)
    pad_ref[0:HALO, :] = zeros
    pad_ref[HALO:HALO + rows, :] = glu
    pad_ref[HALO + rows:HALO + rows + HALO, :] = zeros

    def store(r0, sub, y):
        ya_ref[r0:r0 + sub, :] = y

    _conv_core(rows, pad_ref, shift_ref, cw_ref, cb_ref, lg_ref, lb_ref, store)
    y = jnp.concatenate([ya_ref[...]] + yb, axis=-1) * sz.astype(F32)
    x1 = x + gate0_ref[0] * _dot(y.astype(BF16), wo0_ref[...])
    u1 = _dot(_modulate(x1, g1_ref, s1p1_ref, sh1_ref), wkv1_ref[...])
    kc_ref[0] = u1[:, 0:512].astype(BF16)
    vc_ref[0] = _pairs_with_ones(u1[:, 512:1024]).astype(BF16)


def _ctx_layer(ctx, g0, s1p0, sh0, w_in0, qg, kg, mq, mk, conv_w, conv_b, ln_g, ln_b, w_out0, gate0,
               g1, s1p1, sh1, w_in1, kv_col_block):
    b, l, d = ctx.shape
    c = conv_w.shape[1]
    row = lambda bi: (bi, 0, 0)
    const2 = lambda bi: (0, 0)
    vec = pl.BlockSpec((1, d), const2)
    per_b = pl.BlockSpec((1, 1, d), row)
    cvec = pl.BlockSpec((1, c), const2)
    return pl.pallas_call(
        _ctx_layer_body,
        grid=(b,),
        in_specs=[pl.BlockSpec((1, l, d), row), vec, per_b, per_b,
                  pl.BlockSpec(w_in0.shape, const2),
                  pl.BlockSpec((1, 512), const2), pl.BlockSpec((1, 128), const2),
                  pl.BlockSpec((512, 512), const2), pl.BlockSpec((128, 128), const2),
                  pl.BlockSpec((CONV_WIDTH, SUBLANES, c), lambda bi: (0, 0, 0)), cvec, cvec, cvec,
                  pl.BlockSpec(w_out0.shape, const2), per_b,
                  vec, per_b, per_b,
                  pl.BlockSpec((d, 1024), lambda bi: (0, kv_col_block))],
        out_specs=[pl.BlockSpec((1, l, 256), row), pl.BlockSpec((1, l, 512), row),
                   pl.BlockSpec((1, l, 512), row), pl.BlockSpec((1, l, 1024), row)],
        out_shape=[jax.ShapeDtypeStruct((b, l, 256), BF16), jax.ShapeDtypeStruct((b, l, 512), BF16),
                   jax.ShapeDtypeStruct((b, l, 512), BF16), jax.ShapeDtypeStruct((b, l, 1024), BF16)],
        scratch_shapes=[pltpu.VMEM((l + 2 * HALO, c), F32),
                        pltpu.VMEM((SUBLANES, l + 2 * HALO - SUBLANES, c), F32),
                        pltpu.VMEM((l, c), F32)],
        compiler_params=_params(("arbitrary",)),
        name="ctx_layer",
    )(ctx, g0, s1p0, sh0, w_in0, qg, kg, mq, mk,
      jnp.broadcast_to(conv_w[:, None, :], (CONV_WIDTH, SUBLANES, c)), conv_b.reshape(1, c), ln_g.reshape(1, c),
      ln_b.reshape(1, c), w_out0, gate0, g1, s1p1, sh1, w_in1)


def _outproj_body(final, ya_ref, yb_ref, sz_ref, w_ref, x_ref, gate_ref, *rest):
    if final:
        perm_ref, fg_ref, o_ref = rest
        stacked = jnp.concatenate([ya_ref[0, j] for j in range(ya_ref.shape[1])], axis=0)
        ya = _dot(perm_ref[...], stacked)
    else:
        ya = ya_ref[0].astype(F32)
    y = jnp.concatenate([ya, yb_ref[0].astype(F32)], axis=-1) * sz_ref[0].astype(F32)
    xn = x_ref[0] + gate_ref[0] * _dot(y.astype(BF16), w_ref[...])
    if final:
        o_ref[0] = xn * lax.rsqrt(jnp.mean(xn * xn, axis=-1, keepdims=True) + EPS) * fg_ref[...]
    else:
        rest[0][0] = xn


def _outproj(ya, yb, sz, w, x, gate, final_g, tm):
    b, s, d = x.shape
    row = lambda bi, i: (bi, i, 0)
    const = lambda bi, i: (0, 0)
    per_b = lambda bi, i: (bi, 0, 0)
    final = final_g is not None
    ya_spec = (pl.BlockSpec((1, DFT_RADIX, tm // DFT_RADIX, 512), lambda bi, i: (bi, 0, i, 0)) if final
               else pl.BlockSpec((1, tm, 512), row))
    in_specs = [ya_spec,
                pl.BlockSpec((1, tm, 512), row),
                pl.BlockSpec((1, tm, 1024), row),
                pl.BlockSpec((1024, d), const),
                pl.BlockSpec((1, tm, d), row),
                pl.BlockSpec((1, 1, d), per_b)]
    args = [ya, yb, sz, w, x, gate]
    if final:
        in_specs += [pl.BlockSpec((tm, tm), const), pl.BlockSpec((1, d), const)]
        args += [jnp.asarray(_class_row_permutation(tm)).astype(BF16), final_g]
    return pl.pallas_call(
        functools.partial(_outproj_body, final),
        grid=(b, s // tm),
        in_specs=in_specs,
        out_specs=pl.BlockSpec((1, tm, d), row),
        out_shape=jax.ShapeDtypeStruct((b, s, d), F32),
        compiler_params=_params(("arbitrary", "arbitrary")),
        name="outproj_final" if final_g is not None else "outproj",
    )(*args)


def _fourier_w_body(cc_ref, sc_ref, w_ref, o_ref):
    for g in range(w_ref.shape[0]):
        wc = jnp.dot(cc_ref[...], w_ref[g], preferred_element_type=F32, precision=HIGHEST)
        ws = jnp.dot(sc_ref[...], w_ref[g], preferred_element_type=F32, precision=HIGHEST)
        o_ref[g] = jnp.concatenate([wc, ws], axis=-1).astype(o_ref.dtype)


def _fourier_weights(cc, sc, w):
    g, c, _ = w.shape
    return pl.pallas_call(
        _fourier_w_body,
        out_shape=jax.ShapeDtypeStruct((g, c, 2 * c), BF16),
        name="fourier_weights",
    )(cc, sc, w)


def _fourier_tables(seq):
    quarter = seq // DFT_RADIX
    m = np.arange(quarter)
    tabs = []
    for j in range(DFT_RADIX):
        ang = 2.0 * np.pi * (((DFT_RADIX * m + j)[:, None] * m[None, :]) % seq) / seq
        tabs.append(np.concatenate([np.cos(ang), -np.sin(ang)], axis=1) / math.sqrt(seq))
    return np.stack(tabs).astype(np.float32)


def _class_row_permutation(rows):
    per = rows // DFT_RADIX
    p = np.zeros((rows, rows), np.float32)
    for j in range(DFT_RADIX):
        p[DFT_RADIX * np.arange(per) + j, j * per + np.arange(per)] = 1.0
    return p


def _fourier_body(f_ref, w1_ref, tab_ref, o_ref, yc_ref, ys_ref):
    groups, c, _ = w1_ref.shape
    quarter = f_ref.shape[1] // DFT_RADIX
    for g in range(groups):
        yg = _dot(f_ref[0, :, g * c:(g + 1) * c], w1_ref[g])
        yc_ref[:, g * c:(g + 1) * c] = yg[:, :c]
        ys_ref[:, g * c:(g + 1) * c] = yg[:, c:]
    for j in range(DFT_RADIX):
        pc, ps = [], []
        for q in range(DFT_RADIX):
            rows = slice(q * quarter, (q + 1) * quarter)
            a = (j * q) % 4
            if a == 0:
                pc.append((1, yc_ref, rows)), ps.append((1, ys_ref, rows))
            elif a == 1:
                pc.append((-1, ys_ref, rows)), ps.append((1, yc_ref, rows))
            elif a == 2:
                pc.append((-1, yc_ref, rows)), ps.append((-1, ys_ref, rows))
            else:
                pc.append((1, ys_ref, rows)), ps.append((-1, yc_ref, rows))
        folded = []
        for terms in (pc, ps):
            terms = sorted(terms, key=lambda t: -t[0])
            acc = terms[0][1][terms[0][2], :]
            for sign, ref, rows in terms[1:]:
                acc = acc + ref[rows, :] if sign > 0 else acc - ref[rows, :]
            folded.append(acc.astype(BF16))
        o_ref[0, j] = _dot(tab_ref[j], jnp.concatenate(folded, axis=0)).astype(o_ref.dtype)


def _fourier_mix(f, w1, tab):
    b, s, w = f.shape
    quarter = s // DFT_RADIX
    return pl.pallas_call(
        _fourier_body,
        grid=(b,),
        in_specs=[pl.BlockSpec((1, s, w), lambda bi: (bi, 0, 0)),
                  pl.BlockSpec(w1.shape, lambda bi: (0, 0, 0)),
                  pl.BlockSpec(tab.shape, lambda bi: (0, 0, 0))],
        out_specs=pl.BlockSpec((1, DFT_RADIX, quarter, w), lambda bi: (bi, 0, 0, 0)),
        out_shape=jax.ShapeDtypeStruct((b, DFT_RADIX, quarter, w), BF16),
        scratch_shapes=[pltpu.VMEM((s, w), F32), pltpu.VMEM((s, w), F32)],
        compiler_params=_params(("arbitrary",)),
        name="fourier_mix",
    )(f, w1, tab)


def _na_tile_geometry(rows_total):
    last_r0 = rows_total - NA_TILE_ROWS
    return ((0, 0), (2 * NA_TILE_ROWS, 2 * NA_TILE_ROWS - NA_ROWS // 2), (last_r0, rows_total - NA_WIN_ROWS))


def _na_bias_body(rows_total, rpb_ref, o_ref, t2_ref):
    h = pl.program_id(0)
    wq = lax.broadcasted_iota(jnp.int32, (GRID_W, LANES), 0)
    wk = lax.broadcasted_iota(jnp.int32, (GRID_W, LANES), 1) & (GRID_W - 1)
    cs = jnp.clip(wq - NA_COLS // 2, 0, GRID_W - NA_COLS)
    col_ok = (wk >= cs) & (wk < cs + NA_COLS)
    dc = wk - wq + (NA_COLS - 1)
    n_dr, n_dc = 2 * NA_ROWS - 1, 2 * NA_COLS - 1
    neg = jnp.full((GRID_W, LANES), NEG_INF, F32)
    for dr in range(n_dr):
        val = neg
        for c in range(n_dc):
            val = jnp.where(col_ok & (dc == c), rpb_ref[h * (n_dr * n_dc) + dr * n_dc + c] * LOG2E, val)
        t2_ref[dr] = val
    lane = lax.broadcasted_iota(jnp.int32, (GRID_W, LANES), 1)
    left = lane < GRID_W
    for ty, (r0, ws) in enumerate(_na_tile_geometry(rows_total)):
        for i in range(NA_TILE_ROWS):
            r = r0 + i
            rs = min(max(r - NA_ROWS // 2, 0), rows_total - NA_ROWS)
            for jp in range(NA_WIN_ROWS // 2):
                halves = []
                for kr in (ws + 2 * jp, ws + 2 * jp + 1):
                    halves.append(t2_ref[kr - r + NA_ROWS - 1] if rs <= kr < rs + NA_ROWS else neg)
                o_ref[0, ty, i * GRID_W:(i + 1) * GRID_W, jp * LANES:(jp + 1) * LANES] = jnp.where(left, halves[0], halves[1])


def _na_bias_tables(rpb, rows_total):
    h = rpb.shape[0]
    q_rows, k_cols = NA_TILE_ROWS * GRID_W, NA_WIN_ROWS * GRID_W
    return pl.pallas_call(
        functools.partial(_na_bias_body, rows_total),
        grid=(h,),
        in_specs=[pl.BlockSpec(memory_space=pltpu.SMEM)],
        out_specs=pl.BlockSpec((1, 3, q_rows, k_cols), lambda hi: (hi, 0, 0, 0)),
        out_shape=jax.ShapeDtypeStruct((h, 3, q_rows, k_cols), F32),
        scratch_shapes=[pltpu.VMEM((2 * NA_ROWS - 1, GRID_W, LANES), F32)],
        compiler_params=_params(("arbitrary",)),
        name="na_bias_tables",
    )(rpb.reshape(-1))


def _na_body(rows_total, q_ref, k_ref, v_ref, kc_ref, vc_ref, bias_ref, o_ref):
    t = pl.program_id(2)
    n_t = pl.num_programs(2)
    ty = jnp.where(t == 0, 0, jnp.where(t == n_t - 1, 2, 1))
    ws = jnp.clip(t * NA_TILE_ROWS - NA_ROWS // 2, 0, rows_total - NA_WIN_ROWS)
    off = pl.multiple_of(ws * GRID_W, GRID_W)
    n_win = NA_WIN_ROWS * GRID_W
    bias = jnp.concatenate([bias_ref[0, ty], bias_ref[1, ty]], axis=0)
    for bb in range(q_ref.shape[0]):
        q2 = _split_heads(q_ref[bb])
        o = _online_softmax_pv(q2, [(kc_ref[bb], vc_ref[bb], None),
                                    (k_ref[bb, pl.ds(off, n_win), :], v_ref[bb, pl.ds(off, n_win), :], bias)],
                               online=False)
        o_ref[bb] = o.astype(o_ref.dtype)


def _neighborhood_attention(q, k, v, kc, vc, bias):
    b, s, w = q.shape
    l = kc.shape[1]
    rows_total = s // GRID_W
    tq = NA_TILE_ROWS * GRID_W
    nb = NA_BATCH_PER_STEP
    full = lambda j, bi, t: (bi, 0, j)
    return pl.pallas_call(
        functools.partial(_na_body, rows_total),
        grid=(w // LANES, b // nb, s // tq),
        in_specs=[pl.BlockSpec((nb, tq, LANES), lambda j, bi, t: (bi, t, j)),
                  pl.BlockSpec((nb, s, LANES), full),
                  pl.BlockSpec((nb, s, 2 * LANES), full),
                  pl.BlockSpec((nb, l, LANES), full),
                  pl.BlockSpec((nb, l, 2 * LANES), full),
                  pl.BlockSpec((2,) + bias.shape[1:], lambda j, bi, t: (j, 0, 0, 0))],
        out_specs=pl.BlockSpec((nb, tq, LANES), lambda j, bi, t: (bi, t, j)),
        out_shape=jax.ShapeDtypeStruct((b, s, w), BF16),
        compiler_params=_params(("arbitrary", "arbitrary", "arbitrary")),
        name="neighborhood_attention",
    )(q, k, v, kc, vc, bias)


def _modulation(m, rows, batch):
    d = m.shape[1] // 3
    pick = lambda lo: jnp.broadcast_to(m[rows, lo:lo + d].reshape(-1, 1, d), (batch, 1, d))
    return pick(0), 1.0 + pick(d), pick(2 * d)


def kernel(x, c, ctx, c_ctx, ab_w_ada, ab_b_ada, ab_norm_g, ab_w_in, ab_conv_w, ab_conv_b, ab_ln_g, ab_ln_b, ab_q_norm_g, ab_k_norm_g, ab_w_out, cd_w_ada, cd_b_ada, cd_norm_g, cd_w_in, cd_w_fourier, cd_rpb, cd_w_out, final_norm_g):
    b, s, d = x.shape
    l = ctx.shape[1]
    assert d == D_MODEL and s % (NA_TILE_ROWS * GRID_W) == 0 and ab_w_ada.shape[0] == 1 and cd_w_ada.shape[0] == 1

    cond = jnp.zeros((16, d), F32).at[:b].set(c).at[b].set(c_ctx)
    lat = slice(0, b)
    cx = slice(b, b + 1)

    m0 = _adaln(cond, ab_w_ada[0], ab_b_ada[0])
    m1 = _adaln(cond, cd_w_ada[0], cd_b_ada[0])
    sh, s1p, gate = _modulation(m0, lat, b)
    csh, cs1p, cgate = _modulation(m0, cx, b)
    sh1, s1p1, gate1 = _modulation(m1, lat, b)
    csh1, cs1p1, _ = _modulation(m1, cx, b)
    ng1 = cd_norm_g[0].reshape(1, d)
    w_in1 = cd_w_in[0].astype(BF16)

    w_in0 = ab_w_in[0].astype(BF16)
    w_out0 = ab_w_out[0].astype(BF16)

    cos, sin = _rope_tables(s)
    cos, sin = jnp.asarray(cos), jnp.asarray(sin)
    mq = jnp.asarray(_block_ones(512)).astype(BF16)
    mk = jnp.asarray(_block_ones(128)).astype(BF16)
    qg = jnp.tile(ab_q_norm_g[0], 512 // HEAD_DIM).reshape(1, 512)
    kg = jnp.tile(ab_k_norm_g[0], 128 // HEAD_DIM).reshape(1, 128)
    ng0 = ab_norm_g[0].reshape(1, d)

    ck, cv, kc, vc = _ctx_layer(ctx, ng0, cs1p, csh, w_in0, qg, kg, mq, mk, ab_conv_w[0], ab_conv_b[0], ab_ln_g[0],
                                ab_ln_b[0], w_out0, cgate, ng1, cs1p1, csh1, w_in1, 1)

    glu, q, k, v, sz = _inproj_ab(x, ng0, s1p, sh, w_in0, qg, kg, cos, sin, mq, mk, True, 1024)
    y_b = _gqa_attention(q, [ck, k], [cv, v], 1024, 1)
    y_a = _conformer_conv(glu, ab_conv_w[0], ab_conv_b[0], ab_ln_g[0], ab_ln_b[0], 256)

    x1, f, q, k, v, sz = _outproj_inproj_cd(y_a, y_b, sz, w_out0, x, gate, ng1, s1p1, sh1, w_in1, 512)
    gate = gate1

    c_c, s_c = _dft_tables(cd_w_fourier.shape[-1])
    w1 = _fourier_weights(jnp.asarray(c_c), jnp.asarray(s_c), cd_w_fourier[0])
    y_c = _fourier_mix(f, w1, jnp.asarray(_fourier_tables(s)).astype(BF16))

    bias = _na_bias_tables(cd_rpb[0], s // GRID_W)
    y_d = _neighborhood_attention(q, k, v, kc, vc, bias)

    return _outproj(y_c, y_d, sz, cd_w_out[0].astype(BF16), x1, gate, final_norm_g.reshape(1, d), 512)
```

```python
import functools
import math

import numpy as np
import jax
import jax.numpy as jnp
from jax import lax
from jax.experimental import pallas as pl
from jax.experimental.pallas import tpu as pltpu

F32 = jnp.float32
BF16 = jnp.bfloat16
HIGHEST = lax.Precision.HIGHEST

D_MODEL = 1024
HEAD_DIM = 64
GRID_W = 64
CONV_WIDTH = 31
NA_ROWS = 8
NA_COLS = 16
ROPE_THETA = 10000.0
EPS = 1e-6
NEG_INF = -1e30
LOG2E = math.log2(math.e)
Q_SCALE = HEAD_DIM ** -0.5 * LOG2E
LANES = 128
SUBLANES = 8
HALO = 16
VMEM_LIMIT = 48 * 1024 * 1024

NA_TILE_ROWS = 4
NA_WIN_ROWS = NA_TILE_ROWS + NA_ROWS
NA_BATCH_PER_STEP = 8
NA_KEY_CHUNK = 128
IN_PROJ_SUBTILE = 256
DFT_RADIX = 4

def _params(sem, flags=None):
    return pltpu.CompilerParams(dimension_semantics=sem, vmem_limit_bytes=VMEM_LIMIT, flags=flags)


def _silu(t):
    return t * jax.nn.sigmoid(t)


def _dot(a, b):
    return jnp.dot(a, b, preferred_element_type=F32)


def _dot_nt(a, b):
    return lax.dot_general(a, b, (((1,), (1,)), ((), ())), preferred_element_type=F32)


def _rope_tables(seq):
    t = np.arange(seq)
    row = (t // GRID_W).astype(np.float64)
    col = (t % GRID_W).astype(np.float64)
    nf = HEAD_DIM // 4
    inv = np.float32(ROPE_THETA) ** (-np.arange(nf, dtype=np.float32) / nf)
    ang = np.concatenate([row[:, None] * inv, col[:, None] * inv], axis=-1)
    cos = np.repeat(np.cos(ang), 2, axis=-1)
    sin = np.repeat(np.sin(ang), 2, axis=-1)
    sign = np.tile(np.array([-1.0, 1.0]), HEAD_DIM // 2)
    cos = np.tile(cos, (1, LANES // HEAD_DIM))
    sin = np.tile(sin * sign, (1, LANES // HEAD_DIM))
    return cos.astype(np.float32), sin.astype(np.float32)


def _dft_tables(n):
    k = np.arange(n)
    ang = 2.0 * np.pi * ((k[:, None] * k[None, :]) % n) / n
    s = 1.0 / math.sqrt(n)
    return (np.cos(ang) * s).astype(np.float32), (np.sin(ang) * s).astype(np.float32)


def _block_ones(width):
    h = np.arange(width) // HEAD_DIM
    return (h[:, None] == h[None, :]).astype(np.float32)


def _adaln_body(c_ref, w_ref, b_ref, o_ref):
    a = _silu(c_ref[...])
    o_ref[...] = jnp.dot(a, w_ref[...], preferred_element_type=F32, precision=HIGHEST) + b_ref[...]


def _adaln(cond, w, b):
    r, d = cond.shape
    n = w.shape[1]
    tn = 1024
    return pl.pallas_call(
        _adaln_body,
        grid=(n // tn,),
        in_specs=[pl.BlockSpec((r, d), lambda j: (0, 0)),
                  pl.BlockSpec((d, tn), lambda j: (0, j)),
                  pl.BlockSpec((1, tn), lambda j: (0, j))],
        out_specs=pl.BlockSpec((r, tn), lambda j: (0, j)),
        out_shape=jax.ShapeDtypeStruct((r, n), F32),
        compiler_params=_params(("arbitrary",)),
        name="adaln",
    )(cond, w, b.reshape(1, n))


def _head_rms(t, ones_ref, gain):
    ms = _dot((t * t).astype(BF16), ones_ref[...]) * (1.0 / HEAD_DIM)
    return t * lax.rsqrt(ms + EPS) * gain


def _rope(t, cos, sin_signed):
    rows = t.shape[0]
    lane = lax.broadcasted_iota(jnp.int32, (rows, LANES), 1)
    even = (lane & 1) == 0
    outs = []
    for c in range(t.shape[1] // LANES):
        xc = t[:, c * LANES:(c + 1) * LANES]
        swapped = jnp.where(even, pltpu.roll(xc, LANES - 1, 1), pltpu.roll(xc, 1, 1))
        outs.append(xc * cos + swapped * sin_signed)
    return outs[0] if len(outs) == 1 else jnp.concatenate(outs, axis=1)


def _modulate(x, g_ref, s1p_ref, sh_ref):
    y = x * lax.rsqrt(jnp.mean(x * x, axis=-1, keepdims=True) + EPS) * g_ref[...]
    return (y * s1p_ref[0] + sh_ref[0]).astype(BF16)


def _modulated_norm(x_ref, g_ref, s1p_ref, sh_ref, rows):
    return _modulate(x_ref[0, rows, :], g_ref, s1p_ref, sh_ref)


def _row_subtiles(tm):
    sub = min(tm, IN_PROJ_SUBTILE)
    return [slice(r, r + sub) for r in range(0, tm, sub)]


def _inproj_ab_body(rope, x_ref, g_ref, s1p_ref, sh_ref, w_ref, qg_ref, kg_ref, cos_ref, sin_ref, mq_ref, mk_ref,
                    glu_ref, q_ref, k_ref, v_ref, sz_ref):
    for rows in _row_subtiles(x_ref.shape[1]):
        u = _dot(_modulated_norm(x_ref, g_ref, s1p_ref, sh_ref, rows), w_ref[...])
        tabs = (cos_ref[rows, :], sin_ref[rows, :]) if rope else None
        glu, q, k, v, sz = _ab_epilogue(u, mq_ref, mk_ref, qg_ref, kg_ref, tabs)
        glu_ref[0, rows, :] = glu
        q_ref[0, rows, :] = q
        k_ref[0, rows, :] = k
        v_ref[0, rows, :] = v
        sz_ref[0, rows, :] = sz


def _ab_epilogue(u, mq_ref, mk_ref, qg_ref, kg_ref, rope_tabs):
    glu = u[:, 0:512] * jax.nn.sigmoid(u[:, 512:1024])
    q = _head_rms(u[:, 1024:1536], mq_ref, qg_ref[...])
    k = _head_rms(u[:, 1536:1664], mk_ref, kg_ref[...])
    if rope_tabs is not None:
        q = _rope(q, *rope_tabs)
        k = _rope(k, *rope_tabs)
    v = u[:, 1664:1792]
    k_sw, v_sw = pltpu.roll(k, HEAD_DIM, 1), pltpu.roll(v, HEAD_DIM, 1)
    left = lax.broadcasted_iota(jnp.int32, k.shape, 1) < HEAD_DIM
    one = jnp.ones_like(v)
    k_dup = jnp.concatenate([jnp.where(left, k, k_sw), jnp.where(left, k_sw, k)], axis=1)
    v_dup = jnp.concatenate([jnp.where(left, v, one), jnp.where(left, one, v_sw),
                             jnp.where(left, v_sw, one), jnp.where(left, one, v)], axis=1)
    return (glu, (q * Q_SCALE).astype(BF16), k_dup.astype(BF16), v_dup.astype(BF16),
            _silu(u[:, 1792:2816]).astype(BF16))


def _inproj_ab(x, g, s1p, sh, w, qg, kg, cos, sin, mq, mk, rope, tm):
    b, s, d = x.shape
    n = w.shape[1]
    row = lambda bi, i: (bi, i, 0)
    const = lambda bi, i: (0, 0)
    per_b = lambda bi, i: (bi, 0, 0)
    tab = lambda bi, i: (i, 0)
    return pl.pallas_call(
        functools.partial(_inproj_ab_body, rope),
        grid=(b, s // tm),
        in_specs=[pl.BlockSpec((1, tm, d), row),
                  pl.BlockSpec((1, d), const),
                  pl.BlockSpec((1, 1, d), per_b),
                  pl.BlockSpec((1, 1, d), per_b),
                  pl.BlockSpec((d, n), const),
                  pl.BlockSpec((1, 512), const),
                  pl.BlockSpec((1, 128), const),
                  pl.BlockSpec((tm, LANES), tab),
                  pl.BlockSpec((tm, LANES), tab),
                  pl.BlockSpec((512, 512), const),
                  pl.BlockSpec((128, 128), const)],
        out_specs=[pl.BlockSpec((1, tm, 512), row),
                   pl.BlockSpec((1, tm, 512), row),
                   pl.BlockSpec((1, tm, 256), row),
                   pl.BlockSpec((1, tm, 512), row),
                   pl.BlockSpec((1, tm, 1024), row)],
        out_shape=[jax.ShapeDtypeStruct((b, s, 512), F32),
                   jax.ShapeDtypeStruct((b, s, 512), BF16),
                   jax.ShapeDtypeStruct((b, s, 256), BF16),
                   jax.ShapeDtypeStruct((b, s, 512), BF16),
                   jax.ShapeDtypeStruct((b, s, 1024), BF16)],
        compiler_params=_params(("arbitrary", "arbitrary")),
        name="inproj_ab_rope" if rope else "inproj_ab_ctx",
    )(x, g, s1p, sh, w, qg, kg, cos, sin, mq, mk)


def _outproj_inproj_cd_body(ya_ref, yb_ref, sz0_ref, wo_ref, x_ref, gate_ref, g_ref, s1p_ref, sh_ref, wi_ref,
                            x1_ref, f_ref, q_ref, k_ref, v_ref, sz_ref):
    for rows in _row_subtiles(x_ref.shape[1]):
        y = (jnp.concatenate([ya_ref[0, rows, :], yb_ref[0, rows, :]], axis=-1).astype(F32)
             * sz0_ref[0, rows, :].astype(F32))
        xn = x_ref[0, rows, :] + gate_ref[0] * _dot(y.astype(BF16), wo_ref[...])
        x1_ref[0, rows, :] = xn
        u = _dot(_modulate(xn, g_ref, s1p_ref, sh_ref), wi_ref[...])
        f_ref[0, rows, :] = u[:, 0:512].astype(BF16)
        q_ref[0, rows, :] = (u[:, 512:1024] * Q_SCALE).astype(BF16)
        k_ref[0, rows, :] = u[:, 1024:1536].astype(BF16)
        v_ref[0, rows, :] = _pairs_with_ones(u[:, 1536:2048]).astype(BF16)
        sz_ref[0, rows, :] = _silu(u[:, 2048:3072]).astype(BF16)


def _outproj_inproj_cd(ya, yb, sz0, w_out, x, gate, g, s1p, sh, w_in, tm):
    b, s, d = x.shape
    n = w_in.shape[1]
    row = lambda bi, i: (bi, i, 0)
    const = lambda bi, i: (0, 0)
    per_b = lambda bi, i: (bi, 0, 0)
    return pl.pallas_call(
        _outproj_inproj_cd_body,
        grid=(b, s // tm),
        in_specs=[pl.BlockSpec((1, tm, 512), row),
                  pl.BlockSpec((1, tm, 512), row),
                  pl.BlockSpec((1, tm, 1024), row),
                  pl.BlockSpec((1024, d), const),
                  pl.BlockSpec((1, tm, d), row),
                  pl.BlockSpec((1, 1, d), per_b),
                  pl.BlockSpec((1, d), const),
                  pl.BlockSpec((1, 1, d), per_b),
                  pl.BlockSpec((1, 1, d), per_b),
                  pl.BlockSpec((d, n), const)],
        out_specs=[pl.BlockSpec((1, tm, d), row)] + [pl.BlockSpec((1, tm, 512), row)] * 3
        + [pl.BlockSpec((1, tm, 1024), row)] * 2,
        out_shape=[jax.ShapeDtypeStruct((b, s, d), F32)] + [jax.ShapeDtypeStruct((b, s, 512), BF16)] * 3
        + [jax.ShapeDtypeStruct((b, s, 1024), BF16)] * 2,
        compiler_params=_params(("arbitrary", "arbitrary")),
        name="outproj_inproj_cd",
    )(ya, yb, sz0, w_out, x, gate, g, s1p, sh, w_in)


def _inproj_kv_body(x_ref, g_ref, s1p_ref, sh_ref, w_ref, k_ref, v_ref):
    u = _dot(_modulated_norm(x_ref, g_ref, s1p_ref, sh_ref, slice(None)), w_ref[...])
    k_ref[0] = u[:, 0:512].astype(BF16)
    v_ref[0] = _pairs_with_ones(u[:, 512:1024]).astype(BF16)


def _inproj_kv(x, g, s1p, sh, w, col_block):
    b, s, d = x.shape
    n = 1024
    row = lambda bi: (bi, 0, 0)
    const = lambda bi: (0, 0)
    return pl.pallas_call(
        _inproj_kv_body,
        grid=(b,),
        in_specs=[pl.BlockSpec((1, s, d), row),
                  pl.BlockSpec((1, d), const),
                  pl.BlockSpec((1, 1, d), row),
                  pl.BlockSpec((1, 1, d), row),
                  pl.BlockSpec((d, n), lambda bi: (0, col_block))],
        out_specs=[pl.BlockSpec((1, s, 512), row), pl.BlockSpec((1, s, 1024), row)],
        out_shape=[jax.ShapeDtypeStruct((b, s, 512), BF16), jax.ShapeDtypeStruct((b, s, 1024), BF16)],
        compiler_params=_params(("arbitrary",)),
        name="inproj_kv_ctx",
    )(x, g, s1p, sh, w)


def _split_heads(q):
    lane = lax.broadcasted_iota(jnp.int32, q.shape, 1)
    zero = jnp.zeros_like(q)
    return jnp.concatenate([jnp.where(lane < HEAD_DIM, q, zero), jnp.where(lane >= HEAD_DIM, q, zero)], axis=0)


def _with_ones(v):
    left = lax.broadcasted_iota(jnp.int32, v.shape, 1) < HEAD_DIM
    one = jnp.ones_like(v)
    return jnp.concatenate([jnp.where(left, v, one), jnp.where(left, pltpu.roll(v, HEAD_DIM, 1), one)], axis=1)


def _pairs_with_ones(v):
    return jnp.concatenate([_with_ones(v[:, c:c + LANES]) for c in range(0, v.shape[1], LANES)], axis=1)


def _pv(p, v2, rows):
    pv = _dot(p.astype(BF16), v2)
    return jnp.concatenate([pv[:rows, :LANES], pv[rows:, LANES:]], axis=0)


def _online_softmax_pv(q2, chunks, online=True):
    rows = q2.shape[0] // 2
    scores = [_dot_nt(q2, k) if bias is None else _dot_nt(q2, k) + bias for k, _, bias in chunks]
    if online:
        m = acc = None
        for s, (_, v2, _) in zip(scores, chunks):
            mc = jnp.max(s, axis=-1, keepdims=True)
            m_new = mc if m is None else jnp.maximum(m, mc)
            pv = _pv(jnp.exp2(s - m_new), v2, rows)
            acc = pv if m is None else jnp.exp2(m - m_new) * acc + pv
            m = m_new
    else:
        m = functools.reduce(jnp.maximum, [jnp.max(s, axis=-1, keepdims=True) for s in scores])
        acc = functools.reduce(jnp.add, [_pv(jnp.exp2(s - m), v2, rows) for s, (_, v2, _) in zip(scores, chunks)])
    left = lax.broadcasted_iota(jnp.int32, (rows, LANES), 1) < HEAD_DIM
    top, bot = acc[:rows], acc[rows:]
    return jnp.where(left, top, bot) / pltpu.roll(jnp.where(left, bot, top), HEAD_DIM, 1)


V_ROWS = 80


def _softmax_pv_keys_major(q2, chunks):
    half = q2.shape[0] // 2
    scores = [_dot_nt(k, q2) if bias is None else _dot_nt(k, q2) + bias for k, _, bias in chunks]
    m = acc = None
    for s, (_, v_ones, _) in zip(scores, chunks):
        mc = jnp.max(s, axis=0, keepdims=True)
        m_new = mc if m is None else jnp.maximum(m, mc)
        p = jnp.exp2(s - m_new).astype(BF16)
        if v_ones.shape[1] == LANES:
            pv = _dot(v_ones.T[:V_ROWS], p)
        else:
            pv = jnp.concatenate([_dot(v_ones[:, :LANES].T[:V_ROWS], p[:, :half]),
                                  _dot(v_ones[:, LANES:].T[:V_ROWS], p[:, half:])], axis=1)
        acc = pv if m is None else jnp.exp2(m - m_new) * acc + pv
        m = m_new
    o = acc[:HEAD_DIM] / acc[HEAD_DIM:HEAD_DIM + 1]
    return jnp.concatenate([o[:, :half], o[:, half:]], axis=0).T


def _key_chunks(k_ref, v_ref, kv, chunk):
    n = k_ref.shape[1]
    kl, vl = slice(kv * LANES, (kv + 1) * LANES), slice(kv * 2 * LANES, (kv + 1) * 2 * LANES)
    return [(k_ref[0, c:c + chunk, kl], v_ref[0, c:c + chunk, vl], None) for c in range(0, n, chunk)]


def _gqa_body(n_src, chunk, q_ref, *refs):
    k_refs, v_refs, o_ref = refs[:n_src], refs[n_src:2 * n_src], refs[2 * n_src]
    pairs = q_ref.shape[2] // LANES
    kv_heads = k_refs[0].shape[2] // LANES
    for p in range(pairs):
        lanes = slice(p * LANES, (p + 1) * LANES)
        kv = p * kv_heads // pairs
        chunks = [c for k, v in zip(k_refs, v_refs) for c in _key_chunks(k, v, kv, min(chunk, k.shape[1]))]
        chunks = [(k, v2[:, :LANES], None) for k, v2, _ in chunks]
        o_ref[0, :, lanes] = _softmax_pv_keys_major(_split_heads(q_ref[0, :, lanes]), chunks).astype(o_ref.dtype)


def _gqa_attention(q, ks, vs, tq, pairs, chunk=512):
    b, lq, w = q.shape
    n_src = len(ks)
    n_pairs = w // LANES
    kv_heads = ks[0].shape[2] // LANES
    kv_blk = max(1, pairs * kv_heads // n_pairs)
    kv_idx = lambda bi, i, j: (bi, 0, j * pairs * kv_heads // n_pairs // kv_blk)
    k_specs = [pl.BlockSpec((1, k.shape[1], kv_blk * LANES), kv_idx) for k in ks]
    v_specs = [pl.BlockSpec((1, v.shape[1], kv_blk * 2 * LANES), kv_idx) for v in vs]
    return pl.pallas_call(
        functools.partial(_gqa_body, n_src, chunk),
        grid=(b, lq // tq, n_pairs // pairs),
        in_specs=[pl.BlockSpec((1, tq, pairs * LANES), lambda bi, i, j: (bi, i, j))] + k_specs + v_specs,
        out_specs=pl.BlockSpec((1, tq, pairs * LANES), lambda bi, i, j: (bi, i, j)),
        out_shape=jax.ShapeDtypeStruct((b, lq, w), BF16),
        compiler_params=_params(("arbitrary", "arbitrary", "arbitrary")),
        name=f"gqa_attention_{n_src}src",
    )(q, *ks, *vs)


def _conv_core(rows, pad_ref, shift_ref, w_ref, cb_ref, lg_ref, lb_ref, store):
    span = shift_ref.shape[1]
    for r in range(1, SUBLANES):
        shift_ref[r] = pad_ref[r:r + span, :]
    sub = 32
    base = HALO - CONV_WIDTH // 2
    for r0 in range(0, rows, sub):
        acc = jnp.zeros((sub, pad_ref.shape[1]), F32) + cb_ref[...]
        for k in range(CONV_WIDTH):
            a, r = divmod(base + k, SUBLANES)
            lo = a * SUBLANES + r0
            tap = pad_ref[lo:lo + sub, :] if r == 0 else shift_ref[r, lo:lo + sub, :]
            acc = acc + tap * jnp.concatenate([w_ref[k]] * (sub // SUBLANES), axis=0)
        mu = jnp.mean(acc, axis=-1, keepdims=True)
        xc = acc - mu
        y = xc * lax.rsqrt(jnp.mean(xc * xc, axis=-1, keepdims=True) + EPS) * lg_ref[...] + lb_ref[...]
        store(r0, sub, _silu(y))


def _conv_body(rows, x_ref, prev_ref, next_ref, w_ref, cb_ref, lg_ref, lb_ref, o_ref, pad_ref, shift_ref):
    i = pl.program_id(1)
    last = pl.num_programs(1) - 1
    pad_ref[0:HALO, :] = jnp.where(i > 0, prev_ref[0], 0.0)
    pad_ref[HALO:HALO + rows, :] = x_ref[0]
    pad_ref[HALO + rows:HALO + rows + HALO, :] = jnp.where(i < last, next_ref[0], 0.0)

    def store(r0, sub, y):
        o_ref[0, r0:r0 + sub, :] = y.astype(o_ref.dtype)

    _conv_core(rows, pad_ref, shift_ref, w_ref, cb_ref, lg_ref, lb_ref, store)


def _conformer_conv(u, conv_w, conv_b, ln_g, ln_b, rows):
    b, l, c = u.shape
    hb = rows // HALO
    n_halo = l // HALO
    return pl.pallas_call(
        functools.partial(_conv_body, rows),
        grid=(b, l // rows),
        in_specs=[pl.BlockSpec((1, rows, c), lambda bi, i: (bi, i, 0)),
                  pl.BlockSpec((1, HALO, c), lambda bi, i: (bi, jnp.maximum(i * hb - 1, 0), 0)),
                  pl.BlockSpec((1, HALO, c), lambda bi, i: (bi, jnp.minimum((i + 1) * hb, n_halo - 1), 0)),
                  pl.BlockSpec((CONV_WIDTH, SUBLANES, c), lambda bi, i: (0, 0, 0)),
                  pl.BlockSpec((1, c), lambda bi, i: (0, 0)),
                  pl.BlockSpec((1, c), lambda bi, i: (0, 0)),
                  pl.BlockSpec((1, c), lambda bi, i: (0, 0))],
        out_specs=pl.BlockSpec((1, rows, c), lambda bi, i: (bi, i, 0)),
        out_shape=jax.ShapeDtypeStruct((b, l, c), BF16),
        scratch_shapes=[pltpu.VMEM((rows + 2 * HALO, c), F32),
                        pltpu.VMEM((SUBLANES, rows + 2 * HALO - SUBLANES, c), F32)],
        compiler_params=_params(("arbitrary", "arbitrary")),
        name="conformer_conv",
    )(u, u, u, jnp.broadcast_to(conv_w[:, None, :], (CONV_WIDTH, SUBLANES, c)),
      conv_b.reshape(1, c), ln_g.reshape(1, c), ln_b.reshape(1, c))


def _ctx_layer_body(x_ref, g0_ref, s1p0_ref, sh0_ref, wi0_ref, qg_ref, kg_ref, mq_ref, mk_ref, cw_ref, cb_ref,
                    lg_ref, lb_ref, wo0_ref, gate0_ref, g1_ref, s1p1_ref, sh1_ref, wkv1_ref,
                    ck_ref, cv_ref, kc_ref, vc_ref, pad_ref, shift_ref, ya_ref):
    rows = x_ref.shape[1]
    x = x_ref[0]
    u = _dot(_modulate(x, g0_ref, s1p0_ref, sh0_ref), wi0_ref[...])
    glu, q, k_dup, v_dup, sz = _ab_epilogue(u, mq_ref, mk_ref, qg_ref, kg_ref, None)
    ck_ref[0] = k_dup
    cv_ref[0] = v_dup
    pairs, kv_heads = q.shape[1] // LANES, k_dup.shape[1] // LANES
    yb = []
    for p in range(pairs):
        kv = p * kv_heads // pairs
        keys = [(k_dup[:, kv * LANES:(kv + 1) * LANES], v_dup[:, kv * 2 * LANES:(kv + 1) * 2 * LANES], None)]
        yb.append(_online_softmax_pv(_split_heads(q[:, p * LANES:(p + 1) * LANES]), keys, online=False))
    zeros = jnp.zeros((HALO, glu.shape[1]), F32)
    pad_ref[0:HALO, :] = zeros
    pad_ref[HALO:HALO + rows, :] = glu
    pad_ref[HALO + rows:HALO + rows + HALO, :] = zeros

    def store(r0, sub, y):
        ya_ref[r0:r0 + sub, :] = y

    _conv_core(rows, pad_ref, shift_ref, cw_ref, cb_ref, lg_ref, lb_ref, store)
    y = jnp.concatenate([ya_ref[...]] + yb, axis=-1) * sz.astype(F32)
    x1 = x + gate0_ref[0] * _dot(y.astype(BF16), wo0_ref[...])
    u1 = _dot(_modulate(x1, g1_ref, s1p1_ref, sh1_ref), wkv1_ref[...])
    kc_ref[0] = u1[:, 0:512].astype(BF16)
    vc_ref[0] = _pairs_with_ones(u1[:, 512:1024]).astype(BF16)


def _ctx_layer(ctx, g0, s1p0, sh0, w_in0, qg, kg, mq, mk, conv_w, conv_b, ln_g, ln_b, w_out0, gate0,
               g1, s1p1, sh1, w_in1, kv_col_block):
    b, l, d = ctx.shape
    c = conv_w.shape[1]
    row = lambda bi: (bi, 0, 0)
    const2 = lambda bi: (0, 0)
    vec = pl.BlockSpec((1, d), const2)
    per_b = pl.BlockSpec((1, 1, d), row)
    cvec = pl.BlockSpec((1, c), const2)
    return pl.pallas_call(
        _ctx_layer_body,
        grid=(b,),
        in_specs=[pl.BlockSpec((1, l, d), row), vec, per_b, per_b,
                  pl.BlockSpec(w_in0.shape, const2),
                  pl.BlockSpec((1, 512), const2), pl.BlockSpec((1, 128), const2),
                  pl.BlockSpec((512, 512), const2), pl.BlockSpec((128, 128), const2),
                  pl.BlockSpec((CONV_WIDTH, SUBLANES, c), lambda bi: (0, 0, 0)), cvec, cvec, cvec,
                  pl.BlockSpec(w_out0.shape, const2), per_b,
                  vec, per_b, per_b,
                  pl.BlockSpec((d, 1024), lambda bi: (0, kv_col_block))],
        out_specs=[pl.BlockSpec((1, l, 256), row), pl.BlockSpec((1, l, 512), row),
                   pl.BlockSpec((1, l, 512), row), pl.BlockSpec((1, l, 1024), row)],
        out_shape=[jax.ShapeDtypeStruct((b, l, 256), BF16), jax.ShapeDtypeStruct((b, l, 512), BF16),
                   jax.ShapeDtypeStruct((b, l, 512), BF16), jax.ShapeDtypeStruct((b, l, 1024), BF16)],
        scratch_shapes=[pltpu.VMEM((l + 2 * HALO, c), F32),
                        pltpu.VMEM((SUBLANES, l + 2 * HALO - SUBLANES, c), F32),
                        pltpu.VMEM((l, c), F32)],
        compiler_params=_params(("arbitrary",)),
        name="ctx_layer",
    )(ctx, g0, s1p0, sh0, w_in0, qg, kg, mq, mk,
      jnp.broadcast_to(conv_w[:, None, :], (CONV_WIDTH, SUBLANES, c)), conv_b.reshape(1, c), ln_g.reshape(1, c),
      ln_b.reshape(1, c), w_out0, gate0, g1, s1p1, sh1, w_in1)


def _outproj_body(final, ya_ref, yb_ref, sz_ref, w_ref, x_ref, gate_ref, *rest):
    if final:
        perm_ref, fg_ref, o_ref = rest
        stacked = jnp.concatenate([ya_ref[0, j] for j in range(ya_ref.shape[1])], axis=0)
        ya = _dot(perm_ref[...], stacked)
    else:
        ya = ya_ref[0].astype(F32)
    y = jnp.concatenate([ya, yb_ref[0].astype(F32)], axis=-1) * sz_ref[0].astype(F32)
    xn = x_ref[0] + gate_ref[0] * _dot(y.astype(BF16), w_ref[...])
    if final:
        o_ref[0] = xn * lax.rsqrt(jnp.mean(xn * xn, axis=-1, keepdims=True) + EPS) * fg_ref[...]
    else:
        rest[0][0] = xn


def _outproj(ya, yb, sz, w, x, gate, final_g, tm):
    b, s, d = x.shape
    row = lambda bi, i: (bi, i, 0)
    const = lambda bi, i: (0, 0)
    per_b = lambda bi, i: (bi, 0, 0)
    final = final_g is not None
    ya_spec = (pl.BlockSpec((1, DFT_RADIX, tm // DFT_RADIX, 512), lambda bi, i: (bi, 0, i, 0)) if final
               else pl.BlockSpec((1, tm, 512), row))
    in_specs = [ya_spec,
                pl.BlockSpec((1, tm, 512), row),
                pl.BlockSpec((1, tm, 1024), row),
                pl.BlockSpec((1024, d), const),
                pl.BlockSpec((1, tm, d), row),
                pl.BlockSpec((1, 1, d), per_b)]
    args = [ya, yb, sz, w, x, gate]
    if final:
        in_specs += [pl.BlockSpec((tm, tm), const), pl.BlockSpec((1, d), const)]
        args += [jnp.asarray(_class_row_permutation(tm)).astype(BF16), final_g]
    return pl.pallas_call(
        functools.partial(_outproj_body, final),
        grid=(b, s // tm),
        in_specs=in_specs,
        out_specs=pl.BlockSpec((1, tm, d), row),
        out_shape=jax.ShapeDtypeStruct((b, s, d), F32),
        compiler_params=_params(("arbitrary", "arbitrary")),
        name="outproj_final" if final_g is not None else "outproj",
    )(*args)


def _fourier_w_body(cc_ref, sc_ref, w_ref, o_ref):
    for g in range(w_ref.shape[0]):
        wc = jnp.dot(cc_ref[...], w_ref[g], preferred_element_type=F32, precision=HIGHEST)
        ws = jnp.dot(sc_ref[...], w_ref[g], preferred_element_type=F32, precision=HIGHEST)
        o_ref[g] = jnp.concatenate([wc, ws], axis=-1).astype(o_ref.dtype)


def _fourier_weights(cc, sc, w):
    g, c, _ = w.shape
    return pl.pallas_call(
        _fourier_w_body,
        out_shape=jax.ShapeDtypeStruct((g, c, 2 * c), BF16),
        name="fourier_weights",
    )(cc, sc, w)


def _fourier_tables(seq):
    quarter = seq // DFT_RADIX
    m = np.arange(quarter)
    tabs = []
    for j in range(DFT_RADIX):
        ang = 2.0 * np.pi * (((DFT_RADIX * m + j)[:, None] * m[None, :]) % seq) / seq
        tabs.append(np.concatenate([np.cos(ang), -np.sin(ang)], axis=1) / math.sqrt(seq))
    return np.stack(tabs).astype(np.float32)


def _class_row_permutation(rows):
    per = rows // DFT_RADIX
    p = np.zeros((rows, rows), np.float32)
    for j in range(DFT_RADIX):
        p[DFT_RADIX * np.arange(per) + j, j * per + np.arange(per)] = 1.0
    return p


def _fourier_body(f_ref, w1_ref, tab_ref, o_ref, yc_ref, ys_ref):
    groups, c, _ = w1_ref.shape
    quarter = f_ref.shape[1] // DFT_RADIX
    for g in range(groups):
        yg = _dot(f_ref[0, :, g * c:(g + 1) * c], w1_ref[g])
        yc_ref[:, g * c:(g + 1) * c] = yg[:, :c]
        ys_ref[:, g * c:(g + 1) * c] = yg[:, c:]
    for j in range(DFT_RADIX):
        pc, ps = [], []
        for q in range(DFT_RADIX):
            rows = slice(q * quarter, (q + 1) * quarter)
            a = (j * q) % 4
            if a == 0:
                pc.append((1, yc_ref, rows)), ps.append((1, ys_ref, rows))
            elif a == 1:
                pc.append((-1, ys_ref, rows)), ps.append((1, yc_ref, rows))
            elif a == 2:
                pc.append((-1, yc_ref, rows)), ps.append((-1, ys_ref, rows))
            else:
                pc.append((1, ys_ref, rows)), ps.append((-1, yc_ref, rows))
        folded = []
        for terms in (pc, ps):
            terms = sorted(terms, key=lambda t: -t[0])
            acc = terms[0][1][terms[0][2], :]
            for sign, ref, rows in terms[1:]:
                acc = acc + ref[rows, :] if sign > 0 else acc - ref[rows, :]
            folded.append(acc.astype(BF16))
        o_ref[0, j] = _dot(tab_ref[j], jnp.concatenate(folded, axis=0)).astype(o_ref.dtype)


def _fourier_mix(f, w1, tab):
    b, s, w = f.shape
    quarter = s // DFT_RADIX
    return pl.pallas_call(
        _fourier_body,
        grid=(b,),
        in_specs=[pl.BlockSpec((1, s, w), lambda bi: (bi, 0, 0)),
                  pl.BlockSpec(w1.shape, lambda bi: (0, 0, 0)),
                  pl.BlockSpec(tab.shape, lambda bi: (0, 0, 0))],
        out_specs=pl.BlockSpec((1, DFT_RADIX, quarter, w), lambda bi: (bi, 0, 0, 0)),
        out_shape=jax.ShapeDtypeStruct((b, DFT_RADIX, quarter, w), BF16),
        scratch_shapes=[pltpu.VMEM((s, w), F32), pltpu.VMEM((s, w), F32)],
        compiler_params=_params(("arbitrary",)),
        name="fourier_mix",
    )(f, w1, tab)


def _na_tile_geometry(rows_total):
    last_r0 = rows_total - NA_TILE_ROWS
    return ((0, 0), (2 * NA_TILE_ROWS, 2 * NA_TILE_ROWS - NA_ROWS // 2), (last_r0, rows_total - NA_WIN_ROWS))


def _na_bias_body(rows_total, rpb_ref, o_ref, t2_ref):
    h = pl.program_id(0)
    wk = lax.broadcasted_iota(jnp.int32, (GRID_W, LANES), 0)
    wq = lax.broadcasted_iota(jnp.int32, (GRID_W, LANES), 1) & (GRID_W - 1)
    cs = jnp.clip(wq - NA_COLS // 2, 0, GRID_W - NA_COLS)
    col_ok = (wk >= cs) & (wk < cs + NA_COLS)
    dc = wk - wq + (NA_COLS - 1)
    n_dr, n_dc = 2 * NA_ROWS - 1, 2 * NA_COLS - 1
    neg = jnp.full((GRID_W, LANES), NEG_INF, F32)
    for dr in range(n_dr):
        val = neg
        for c in range(n_dc):
            val = jnp.where(col_ok & (dc == c), rpb_ref[h * (n_dr * n_dc) + dr * n_dc + c] * LOG2E, val)
        t2_ref[dr] = val
    lane = lax.broadcasted_iota(jnp.int32, (GRID_W, LANES), 1)
    left = lane < GRID_W
    for ty, (r0, ws) in enumerate(_na_tile_geometry(rows_total)):
        for j in range(NA_WIN_ROWS):
            kr = ws + j
            for ip in range(NA_TILE_ROWS // 2):
                halves = []
                for r in (r0 + 2 * ip, r0 + 2 * ip + 1):
                    rs = min(max(r - NA_ROWS // 2, 0), rows_total - NA_ROWS)
                    halves.append(t2_ref[kr - r + NA_ROWS - 1] if rs <= kr < rs + NA_ROWS else neg)
                o_ref[0, ty, j * GRID_W:(j + 1) * GRID_W, ip * LANES:(ip + 1) * LANES] = jnp.where(left, halves[0], halves[1])


def _na_bias_tables(rpb, rows_total):
    h = rpb.shape[0]
    q_rows, k_cols = NA_WIN_ROWS * GRID_W, NA_TILE_ROWS * GRID_W
    return pl.pallas_call(
        functools.partial(_na_bias_body, rows_total),
        grid=(h,),
        in_specs=[pl.BlockSpec(memory_space=pltpu.SMEM)],
        out_specs=pl.BlockSpec((1, 3, q_rows, k_cols), lambda hi: (hi, 0, 0, 0)),
        out_shape=jax.ShapeDtypeStruct((h, 3, q_rows, k_cols), F32),
        scratch_shapes=[pltpu.VMEM((2 * NA_ROWS - 1, GRID_W, LANES), F32)],
        compiler_params=_params(("arbitrary",)),
        name="na_bias_tables",
    )(rpb.reshape(-1))


def _na_body(rows_total, q_ref, k_ref, v_ref, kc_ref, vc_ref, bias_ref, o_ref):
    t = pl.program_id(2)
    n_t = pl.num_programs(2)
    ty = jnp.where(t == 0, 0, jnp.where(t == n_t - 1, 2, 1))
    ws = jnp.clip(t * NA_TILE_ROWS - NA_ROWS // 2, 0, rows_total - NA_WIN_ROWS)
    off = pl.multiple_of(ws * GRID_W, GRID_W)
    n_win = NA_WIN_ROWS * GRID_W
    bias = jnp.concatenate([bias_ref[0, ty], bias_ref[1, ty]], axis=1)
    ck = NA_KEY_CHUNK
    for bb in range(q_ref.shape[0]):
        chunks = [(kc_ref[bb], vc_ref[bb], None)]
        for c in range(0, n_win, ck):
            rows = pl.ds(off + c, ck)
            chunks.append((k_ref[bb, rows, :], v_ref[bb, rows, :], bias[c:c + ck]))
        o_ref[bb] = _softmax_pv_keys_major(_split_heads(q_ref[bb]), chunks).astype(o_ref.dtype)


def _neighborhood_attention(q, k, v, kc, vc, bias):
    b, s, w = q.shape
    l = kc.shape[1]
    rows_total = s // GRID_W
    tq = NA_TILE_ROWS * GRID_W
    nb = NA_BATCH_PER_STEP
    full = lambda j, bi, t: (bi, 0, j)
    return pl.pallas_call(
        functools.partial(_na_body, rows_total),
        grid=(w // LANES, b // nb, s // tq),
        in_specs=[pl.BlockSpec((nb, tq, LANES), lambda j, bi, t: (bi, t, j)),
                  pl.BlockSpec((nb, s, LANES), full),
                  pl.BlockSpec((nb, s, 2 * LANES), full),
                  pl.BlockSpec((nb, l, LANES), full),
                  pl.BlockSpec((nb, l, 2 * LANES), full),
                  pl.BlockSpec((2,) + bias.shape[1:], lambda j, bi, t: (j, 0, 0, 0))],
        out_specs=pl.BlockSpec((nb, tq, LANES), lambda j, bi, t: (bi, t, j)),
        out_shape=jax.ShapeDtypeStruct((b, s, w), BF16),
        compiler_params=_params(("arbitrary", "arbitrary", "arbitrary")),
        name="neighborhood_attention",
    )(q, k, v, kc, vc, bias)


def _modulation(m, rows, batch):
    d = m.shape[1] // 3
    pick = lambda lo: jnp.broadcast_to(m[rows, lo:lo + d].reshape(-1, 1, d), (batch, 1, d))
    return pick(0), 1.0 + pick(d), pick(2 * d)


def kernel(x, c, ctx, c_ctx, ab_w_ada, ab_b_ada, ab_norm_g, ab_w_in, ab_conv_w, ab_conv_b, ab_ln_g, ab_ln_b, ab_q_norm_g, ab_k_norm_g, ab_w_out, cd_w_ada, cd_b_ada, cd_norm_g, cd_w_in, cd_w_fourier, cd_rpb, cd_w_out, final_norm_g):
    b, s, d = x.shape
    l = ctx.shape[1]
    assert d == D_MODEL and s % (NA_TILE_ROWS * GRID_W) == 0 and ab_w_ada.shape[0] == 1 and cd_w_ada.shape[0] == 1

    cond = jnp.zeros((16, d), F32).at[:b].set(c).at[b].set(c_ctx)
    lat = slice(0, b)
    cx = slice(b, b + 1)

    m0 = _adaln(cond, ab_w_ada[0], ab_b_ada[0])
    m1 = _adaln(cond, cd_w_ada[0], cd_b_ada[0])
    sh, s1p, gate = _modulation(m0, lat, b)
    csh, cs1p, cgate = _modulation(m0, cx, b)
    sh1, s1p1, gate1 = _modulation(m1, lat, b)
    csh1, cs1p1, _ = _modulation(m1, cx, b)
    ng1 = cd_norm_g[0].reshape(1, d)
    w_in1 = cd_w_in[0].astype(BF16)

    w_in0 = ab_w_in[0].astype(BF16)
    w_out0 = ab_w_out[0].astype(BF16)

    cos, sin = _rope_tables(s)
    cos, sin = jnp.asarray(cos), jnp.asarray(sin)
    mq = jnp.asarray(_block_ones(512)).astype(BF16)
    mk = jnp.asarray(_block_ones(128)).astype(BF16)
    qg = jnp.tile(ab_q_norm_g[0], 512 // HEAD_DIM).reshape(1, 512)
    kg = jnp.tile(ab_k_norm_g[0], 128 // HEAD_DIM).reshape(1, 128)
    ng0 = ab_norm_g[0].reshape(1, d)

    ck, cv, kc, vc = _ctx_layer(ctx, ng0, cs1p, csh, w_in0, qg, kg, mq, mk, ab_conv_w[0], ab_conv_b[0], ab_ln_g[0],
                                ab_ln_b[0], w_out0, cgate, ng1, cs1p1, csh1, w_in1, 1)

    glu, q, k, v, sz = _inproj_ab(x, ng0, s1p, sh, w_in0, qg, kg, cos, sin, mq, mk, True, 1024)
    y_b = _gqa_attention(q, [ck, k], [cv, v], 1024, 1, chunk=128)
    y_a = _conformer_conv(glu, ab_conv_w[0], ab_conv_b[0], ab_ln_g[0], ab_ln_b[0], 256)

    x1, f, q, k, v, sz = _outproj_inproj_cd(y_a, y_b, sz, w_out0, x, gate, ng1, s1p1, sh1, w_in1, 512)
    gate = gate1

    c_c, s_c = _dft_tables(cd_w_fourier.shape[-1])
    w1 = _fourier_weights(jnp.asarray(c_c), jnp.asarray(s_c), cd_w_fourier[0])
    y_c = _fourier_mix(f, w1, jnp.asarray(_fourier_tables(s)).astype(BF16))

    bias = _na_bias_tables(cd_rpb[0], s // GRID_W)
    y_d = _neighborhood_attention(q, k, v, kc, vc, bias)

    return _outproj(y_c, y_d, sz, cd_w_out[0].astype(BF16), x1, gate, final_norm_g.reshape(1, d), 512)
```

```python
import functools
import math

import numpy as np
import jax
import jax.numpy as jnp
from jax import lax
from jax.experimental import pallas as pl
from jax.experimental.pallas import tpu as pltpu

F32 = jnp.float32
BF16 = jnp.bfloat16
HIGHEST = lax.Precision.HIGHEST

D_MODEL = 1024
HEAD_DIM = 64
GRID_W = 64
CONV_WIDTH = 31
NA_ROWS = 8
NA_COLS = 16
ROPE_THETA = 10000.0
EPS = 1e-6
NEG_INF = -1e30
LOG2E = math.log2(math.e)
Q_SCALE = HEAD_DIM ** -0.5 * LOG2E
LANES = 128
SUBLANES = 8
HALO = 16
VMEM_LIMIT = 48 * 1024 * 1024

NA_TILE_ROWS = 4
NA_WIN_ROWS = NA_TILE_ROWS + NA_ROWS
NA_BATCH_PER_STEP = 8
IN_PROJ_SUBTILE = 256
DFT_RADIX = 4


def _params(sem, flags=None):
    return pltpu.CompilerParams(dimension_semantics=sem, vmem_limit_bytes=VMEM_LIMIT, flags=flags)


def _silu(t):
    return t * jax.nn.sigmoid(t)


def _dot(a, b):
    return jnp.dot(a, b, preferred_element_type=F32)


def _dot_nt(a, b):
    return lax.dot_general(a, b, (((1,), (1,)), ((), ())), preferred_element_type=F32)


def _rope_tables(seq):
    t = np.arange(seq)
    row = (t // GRID_W).astype(np.float64)
    col = (t % GRID_W).astype(np.float64)
    nf = HEAD_DIM // 4
    inv = np.float32(ROPE_THETA) ** (-np.arange(nf, dtype=np.float32) / nf)
    ang = np.concatenate([row[:, None] * inv, col[:, None] * inv], axis=-1)
    cos = np.repeat(np.cos(ang), 2, axis=-1)
    sin = np.repeat(np.sin(ang), 2, axis=-1)
    sign = np.tile(np.array([-1.0, 1.0]), HEAD_DIM // 2)
    cos = np.tile(cos, (1, LANES // HEAD_DIM))
    sin = np.tile(sin * sign, (1, LANES // HEAD_DIM))
    return cos.astype(np.float32), sin.astype(np.float32)


def _dft_tables(n):
    k = np.arange(n)
    ang = 2.0 * np.pi * ((k[:, None] * k[None, :]) % n) / n
    s = 1.0 / math.sqrt(n)
    return (np.cos(ang) * s).astype(np.float32), (np.sin(ang) * s).astype(np.float32)


def _block_ones(width):
    h = np.arange(width) // HEAD_DIM
    return (h[:, None] == h[None, :]).astype(np.float32)


def _adaln_body(c_ref, w_ref, b_ref, o_ref):
    a = _silu(c_ref[...]).astype(BF16)
    o_ref[...] = _dot(a, w_ref[...].astype(BF16)) + b_ref[...]


def _adaln(cond, w, b):
    r, d = cond.shape
    n = w.shape[1]
    tn = 1024
    return pl.pallas_call(
        _adaln_body,
        grid=(n // tn,),
        in_specs=[pl.BlockSpec((r, d), lambda j: (0, 0)),
                  pl.BlockSpec((d, tn), lambda j: (0, j)),
                  pl.BlockSpec((1, tn), lambda j: (0, j))],
        out_specs=pl.BlockSpec((r, tn), lambda j: (0, j)),
        out_shape=jax.ShapeDtypeStruct((r, n), F32),
        compiler_params=_params(("arbitrary",)),
        name="adaln",
    )(cond, w, b.reshape(1, n))


def _head_rms(t, ones_ref, gain):
    ms = _dot((t * t).astype(BF16), ones_ref[...]) * (1.0 / HEAD_DIM)
    return t * lax.rsqrt(ms + EPS) * gain


def _rope(t, cos, sin_signed):
    rows = t.shape[0]
    lane = lax.broadcasted_iota(jnp.int32, (rows, LANES), 1)
    even = (lane & 1) == 0
    outs = []
    for c in range(t.shape[1] // LANES):
        xc = t[:, c * LANES:(c + 1) * LANES]
        swapped = jnp.where(even, pltpu.roll(xc, LANES - 1, 1), pltpu.roll(xc, 1, 1))
        outs.append(xc * cos + swapped * sin_signed)
    return outs[0] if len(outs) == 1 else jnp.concatenate(outs, axis=1)


def _modulate(x, g_ref, s1p_ref, sh_ref):
    y = x * lax.rsqrt(jnp.mean(x * x, axis=-1, keepdims=True) + EPS) * g_ref[...]
    return (y * s1p_ref[0] + sh_ref[0]).astype(BF16)


def _row_subtiles(tm):
    sub = min(tm, IN_PROJ_SUBTILE)
    return [slice(r, r + sub) for r in range(0, tm, sub)]


def _inproj_ab_body(x_ref, g_ref, s1p_ref, sh_ref, w_ref, qg_ref, kg_ref, cos_ref, sin_ref, mq_ref, mk_ref,
                    glu_ref, q_ref, k_ref, v_ref, sz_ref):
    for rows in _row_subtiles(x_ref.shape[1]):
        u = _dot(_modulate(x_ref[0, rows, :], g_ref, s1p_ref, sh_ref), w_ref[...])
        glu, q, k, v, sz = _ab_epilogue(u, mq_ref, mk_ref, qg_ref, kg_ref, (cos_ref[rows, :], sin_ref[rows, :]))
        glu_ref[0, rows, :] = glu
        q_ref[0, rows, :] = q
        k_ref[0, rows, :] = k
        v_ref[0, rows, :] = v
        sz_ref[0, rows, :] = sz


def _ab_epilogue(u, mq_ref, mk_ref, qg_ref, kg_ref, rope_tabs):
    glu = u[:, 0:512] * jax.nn.sigmoid(u[:, 512:1024])
    q = _head_rms(u[:, 1024:1536], mq_ref, qg_ref[...])
    k = _head_rms(u[:, 1536:1664], mk_ref, kg_ref[...])
    if rope_tabs is not None:
        q = _rope(q, *rope_tabs)
        k = _rope(k, *rope_tabs)
    v = u[:, 1664:1792]
    k_sw, v_sw = pltpu.roll(k, HEAD_DIM, 1), pltpu.roll(v, HEAD_DIM, 1)
    left = lax.broadcasted_iota(jnp.int32, k.shape, 1) < HEAD_DIM
    one = jnp.ones_like(v)
    k_dup = jnp.concatenate([jnp.where(left, k, k_sw), jnp.where(left, k_sw, k)], axis=1)
    v_dup = jnp.concatenate([jnp.where(left, v, one), jnp.where(left, one, v_sw),
                             jnp.where(left, v_sw, one), jnp.where(left, one, v)], axis=1)
    return (glu, (q * Q_SCALE).astype(BF16), k_dup.astype(BF16), v_dup.astype(BF16),
            _silu(u[:, 1792:2816]).astype(BF16))


def _inproj_ab(x, g, s1p, sh, w, qg, kg, cos, sin, mq, mk, tm):
    b, s, d = x.shape
    n = w.shape[1]
    row = lambda bi, i: (bi, i, 0)
    const = lambda bi, i: (0, 0)
    per_b = lambda bi, i: (bi, 0, 0)
    tab = lambda bi, i: (i, 0)
    return pl.pallas_call(
        _inproj_ab_body,
        grid=(b, s // tm),
        in_specs=[pl.BlockSpec((1, tm, d), row),
                  pl.BlockSpec((1, d), const),
                  pl.BlockSpec((1, 1, d), per_b),
                  pl.BlockSpec((1, 1, d), per_b),
                  pl.BlockSpec((d, n), const),
                  pl.BlockSpec((1, 512), const),
                  pl.BlockSpec((1, 128), const),
                  pl.BlockSpec((tm, LANES), tab),
                  pl.BlockSpec((tm, LANES), tab),
                  pl.BlockSpec((512, 512), const),
                  pl.BlockSpec((128, 128), const)],
        out_specs=[pl.BlockSpec((1, tm, 512), row),
                   pl.BlockSpec((1, tm, 512), row),
                   pl.BlockSpec((1, tm, 256), row),
                   pl.BlockSpec((1, tm, 512), row),
                   pl.BlockSpec((1, tm, 1024), row)],
        out_shape=[jax.ShapeDtypeStruct((b, s, 512), F32),
                   jax.ShapeDtypeStruct((b, s, 512), BF16),
                   jax.ShapeDtypeStruct((b, s, 256), BF16),
                   jax.ShapeDtypeStruct((b, s, 512), BF16),
                   jax.ShapeDtypeStruct((b, s, 1024), BF16)],
        compiler_params=_params(("arbitrary", "arbitrary")),
        name="inproj_ab",
    )(x, g, s1p, sh, w, qg, kg, cos, sin, mq, mk)


def _outproj_inproj_cd_body(ya_ref, yb_ref, sz0_ref, wo_ref, x_ref, gate_ref, g_ref, s1p_ref, sh_ref, wi_ref,
                            x1_ref, f_ref, q_ref, k_ref, v_ref, sz_ref):
    for rows in _row_subtiles(x_ref.shape[1]):
        y = (jnp.concatenate([ya_ref[0, rows, :], yb_ref[0, rows, :]], axis=-1).astype(F32)
             * sz0_ref[0, rows, :].astype(F32))
        xn = x_ref[0, rows, :] + gate_ref[0] * _dot(y.astype(BF16), wo_ref[...])
        x1_ref[0, rows, :] = xn
        u = _dot(_modulate(xn, g_ref, s1p_ref, sh_ref), wi_ref[...])
        f_ref[0, rows, :] = u[:, 0:512].astype(BF16)
        q_ref[0, rows, :] = (u[:, 512:1024] * Q_SCALE).astype(BF16)
        k_ref[0, rows, :] = u[:, 1024:1536].astype(BF16)
        v_ref[0, rows, :] = _pairs_with_ones(u[:, 1536:2048]).astype(BF16)
        sz_ref[0, rows, :] = _silu(u[:, 2048:3072]).astype(BF16)


def _outproj_inproj_cd(ya, yb, sz0, w_out, x, gate, g, s1p, sh, w_in, tm):
    b, s, d = x.shape
    n = w_in.shape[1]
    row = lambda bi, i: (bi, i, 0)
    const = lambda bi, i: (0, 0)
    per_b = lambda bi, i: (bi, 0, 0)
    return pl.pallas_call(
        _outproj_inproj_cd_body,
        grid=(b, s // tm),
        in_specs=[pl.BlockSpec((1, tm, 512), row),
                  pl.BlockSpec((1, tm, 512), row),
                  pl.BlockSpec((1, tm, 1024), row),
                  pl.BlockSpec((1024, d), const, pipeline_mode=pl.Buffered(1)),
                  pl.BlockSpec((1, tm, d), row),
                  pl.BlockSpec((1, 1, d), per_b),
                  pl.BlockSpec((1, d), const),
                  pl.BlockSpec((1, 1, d), per_b),
                  pl.BlockSpec((1, 1, d), per_b),
                  pl.BlockSpec((d, n), const, pipeline_mode=pl.Buffered(1))],
        out_specs=[pl.BlockSpec((1, tm, d), row)] + [pl.BlockSpec((1, tm, 512), row)] * 3
        + [pl.BlockSpec((1, tm, 1024), row)] * 2,
        out_shape=[jax.ShapeDtypeStruct((b, s, d), F32)] + [jax.ShapeDtypeStruct((b, s, 512), BF16)] * 3
        + [jax.ShapeDtypeStruct((b, s, 1024), BF16)] * 2,
        compiler_params=_params(("arbitrary", "arbitrary")),
        name="outproj_inproj_cd",
    )(ya, yb, sz0, w_out, x, gate, g, s1p, sh, w_in)


def _split_heads(q):
    lane = lax.broadcasted_iota(jnp.int32, q.shape, 1)
    zero = jnp.zeros_like(q)
    return jnp.concatenate([jnp.where(lane < HEAD_DIM, q, zero), jnp.where(lane >= HEAD_DIM, q, zero)], axis=0)


def _with_ones(v):
    lane = lax.broadcasted_iota(jnp.int32, v.shape, 1)
    one = jnp.ones_like(v)
    return jnp.concatenate([jnp.where(lane < HEAD_DIM, v, one), jnp.where(lane < HEAD_DIM, one, v)], axis=1)


def _pairs_with_ones(v):
    return jnp.concatenate([_with_ones(v[:, c:c + LANES]) for c in range(0, v.shape[1], LANES)], axis=1)


def _pv(p, v2, rows):
    pv = _dot(p.astype(BF16), v2)
    return jnp.concatenate([pv[:rows, :LANES], pv[rows:, LANES:]], axis=0)


def _online_softmax_pv(q2, chunks, online=True):
    rows = q2.shape[0] // 2
    scores = [_dot_nt(q2, k) if bias is None else _dot_nt(q2, k) + bias for k, _, bias in chunks]
    if online:
        m = acc = None
        for s, (_, v2, _) in zip(scores, chunks):
            mc = jnp.max(s, axis=-1, keepdims=True)
            m_new = mc if m is None else jnp.maximum(m, mc)
            pv = _pv(jnp.exp2(s - m_new), v2, rows)
            acc = pv if m is None else jnp.exp2(m - m_new) * acc + pv
            m = m_new
    else:
        m = functools.reduce(jnp.maximum, [jnp.max(s, axis=-1, keepdims=True) for s in scores])
        acc = functools.reduce(jnp.add, [_pv(jnp.exp2(s - m), v2, rows) for s, (_, v2, _) in zip(scores, chunks)])
    left = lax.broadcasted_iota(jnp.int32, (rows, LANES), 1) < HEAD_DIM
    top, bot = acc[:rows], acc[rows:]
    return jnp.where(left, top, bot) / pltpu.roll(jnp.where(left, bot, top), HEAD_DIM, 1)


V_ROWS = 80

def _softmax_pv_keys_major(q2, chunks):
    half = q2.shape[0] // 2
    scores = [_dot_nt(k, q2) for k, _ in chunks]
    m = acc = None
    for s, (_, v_ones) in zip(scores, chunks):
        mc = jnp.max(s, axis=0, keepdims=True)
        m_new = mc if m is None else jnp.maximum(m, mc)
        pv = _dot(v_ones.T[:V_ROWS], jnp.exp2(s - m_new).astype(BF16))
        acc = pv if m is None else jnp.exp2(m - m_new) * acc + pv
        m = m_new
    o = acc[:HEAD_DIM] / acc[HEAD_DIM:HEAD_DIM + 1]
    return jnp.concatenate([o[:, :half], o[:, half:]], axis=0).T


def _key_chunks(k_ref, v_ref, kv, chunk):
    n = k_ref.shape[1]
    kl, vl = slice(kv * LANES, (kv + 1) * LANES), slice(kv * 2 * LANES, (kv + 1) * 2 * LANES)
    return [(k_ref[0, c:c + chunk, kl], v_ref[0, c:c + chunk, vl], None) for c in range(0, n, chunk)]


def _gqa_body(n_src, chunk, q_ref, *refs):
    k_refs, v_refs, o_ref = refs[:n_src], refs[n_src:2 * n_src], refs[2 * n_src]
    pairs = q_ref.shape[2] // LANES
    kv_heads = k_refs[0].shape[2] // LANES
    for p in range(pairs):
        lanes = slice(p * LANES, (p + 1) * LANES)
        kv = p * kv_heads // pairs
        chunks = [c for k, v in zip(k_refs, v_refs) for c in _key_chunks(k, v, kv, min(chunk, k.shape[1]))]
        chunks = [(k, v2[:, :LANES]) for k, v2, _ in chunks]
        o_ref[0, :, lanes] = _softmax_pv_keys_major(_split_heads(q_ref[0, :, lanes]), chunks).astype(o_ref.dtype)


def _gqa_attention(q, ks, vs, tq, pairs, chunk=512):
    b, lq, w = q.shape
    n_src = len(ks)
    n_pairs = w // LANES
    kv_heads = ks[0].shape[2] // LANES
    kv_blk = max(1, pairs * kv_heads // n_pairs)
    kv_idx = lambda bi, i, j: (bi, 0, j * pairs * kv_heads // n_pairs // kv_blk)
    k_specs = [pl.BlockSpec((1, k.shape[1], kv_blk * LANES), kv_idx) for k in ks]
    v_specs = [pl.BlockSpec((1, v.shape[1], kv_blk * 2 * LANES), kv_idx) for v in vs]
    return pl.pallas_call(
        functools.partial(_gqa_body, n_src, chunk),
        grid=(b, lq // tq, n_pairs // pairs),
        in_specs=[pl.BlockSpec((1, tq, pairs * LANES), lambda bi, i, j: (bi, i, j))] + k_specs + v_specs,
        out_specs=pl.BlockSpec((1, tq, pairs * LANES), lambda bi, i, j: (bi, i, j)),
        out_shape=jax.ShapeDtypeStruct((b, lq, w), BF16),
        compiler_params=_params(("arbitrary", "arbitrary", "arbitrary")),
        name=f"gqa_attention_{n_src}src",
    )(q, *ks, *vs)


def _conv_core(rows, pad_ref, shift_ref, w_ref, cb_ref, lg_ref, lb_ref, store):
    span = shift_ref.shape[1]
    for r in range(1, SUBLANES):
        shift_ref[r] = pad_ref[r:r + span, :]
    sub = 32
    base = HALO - CONV_WIDTH // 2
    for r0 in range(0, rows, sub):
        acc = jnp.zeros((sub, pad_ref.shape[1]), F32) + cb_ref[...]
        for k in range(CONV_WIDTH):
            a, r = divmod(base + k, SUBLANES)
            lo = a * SUBLANES + r0
            tap = pad_ref[lo:lo + sub, :] if r == 0 else shift_ref[r, lo:lo + sub, :]
            acc = acc + tap * jnp.concatenate([w_ref[k]] * (sub // SUBLANES), axis=0)
        mu = jnp.mean(acc, axis=-1, keepdims=True)
        xc = acc - mu
        y = xc * lax.rsqrt(jnp.mean(xc * xc, axis=-1, keepdims=True) + EPS) * lg_ref[...] + lb_ref[...]
        store(r0, sub, _silu(y))


def _conv_body(rows, x_ref, prev_ref, next_ref, w_ref, cb_ref, lg_ref, lb_ref, o_ref, pad_ref, shift_ref):
    i = pl.program_id(1)
    last = pl.num_programs(1) - 1
    pad_ref[0:HALO, :] = jnp.where(i > 0, prev_ref[0], 0.0)
    pad_ref[HALO:HALO + rows, :] = x_ref[0]
    pad_ref[HALO + rows:HALO + rows + HALO, :] = jnp.where(i < last, next_ref[0], 0.0)

    def store(r0, sub, y):
        o_ref[0, r0:r0 + sub, :] = y.astype(o_ref.dtype)

    _conv_core(rows, pad_ref, shift_ref, w_ref, cb_ref, lg_ref, lb_ref, store)


def _conformer_conv(u, conv_w, conv_b, ln_g, ln_b, rows):
    b, l, c = u.shape
    hb = rows // HALO
    n_halo = l // HALO
    return pl.pallas_call(
        functools.partial(_conv_body, rows),
        grid=(b, l // rows),
        in_specs=[pl.BlockSpec((1, rows, c), lambda bi, i: (bi, i, 0)),
                  pl.BlockSpec((1, HALO, c), lambda bi, i: (bi, jnp.maximum(i * hb - 1, 0), 0)),
                  pl.BlockSpec((1, HALO, c), lambda bi, i: (bi, jnp.minimum((i + 1) * hb, n_halo - 1), 0)),
                  pl.BlockSpec((CONV_WIDTH, SUBLANES, c), lambda bi, i: (0, 0, 0)),
                  pl.BlockSpec((1, c), lambda bi, i: (0, 0)),
                  pl.BlockSpec((1, c), lambda bi, i: (0, 0)),
                  pl.BlockSpec((1, c), lambda bi, i: (0, 0))],
        out_specs=pl.BlockSpec((1, rows, c), lambda bi, i: (bi, i, 0)),
        out_shape=jax.ShapeDtypeStruct((b, l, c), BF16),
        scratch_shapes=[pltpu.VMEM((rows + 2 * HALO, c), F32),
                        pltpu.VMEM((SUBLANES, rows + 2 * HALO - SUBLANES, c), F32)],
        compiler_params=_params(("arbitrary", "arbitrary")),
        name="conformer_conv",
    )(u, u, u, jnp.broadcast_to(conv_w[:, None, :], (CONV_WIDTH, SUBLANES, c)),
      conv_b.reshape(1, c), ln_g.reshape(1, c), ln_b.reshape(1, c))


def _ctx_layer_body(x_ref, g0_ref, s1p0_ref, sh0_ref, wi0_ref, qg_ref, kg_ref, mq_ref, mk_ref, cw_ref, cb_ref,
                    lg_ref, lb_ref, wo0_ref, gate0_ref, g1_ref, s1p1_ref, sh1_ref, wkv1_ref,
                    ck_ref, cv_ref, kc_ref, vc_ref, pad_ref, shift_ref, ya_ref):
    rows = x_ref.shape[1]
    x = x_ref[0]
    u = _dot(_modulate(x, g0_ref, s1p0_ref, sh0_ref), wi0_ref[...])
    glu, q, k_dup, v_dup, sz = _ab_epilogue(u, mq_ref, mk_ref, qg_ref, kg_ref, None)
    ck_ref[0] = k_dup
    cv_ref[0] = v_dup
    pairs, kv_heads = q.shape[1] // LANES, k_dup.shape[1] // LANES
    yb = []
    for p in range(pairs):
        kv = p * kv_heads // pairs
        keys = [(k_dup[:, kv * LANES:(kv + 1) * LANES], v_dup[:, kv * 2 * LANES:(kv + 1) * 2 * LANES], None)]
        yb.append(_online_softmax_pv(_split_heads(q[:, p * LANES:(p + 1) * LANES]), keys, online=False))
    zeros = jnp.zeros((HALO, glu.shape[1]), F32)
    pad_ref[0:HALO, :] = zeros
    pad_ref[HALO:HALO + rows, :] = glu
    pad_ref[HALO + rows:HALO + rows + HALO, :] = zeros

    def store(r0, sub, y):
        ya_ref[r0:r0 + sub, :] = y

    _conv_core(rows, pad_ref, shift_ref, cw_ref, cb_ref, lg_ref, lb_ref, store)
    y = jnp.concatenate([ya_ref[...]] + yb, axis=-1) * sz.astype(F32)
    x1 = x + gate0_ref[0] * _dot(y.astype(BF16), wo0_ref[...])
    u1 = _dot(_modulate(x1, g1_ref, s1p1_ref, sh1_ref), wkv1_ref[...])
    kc_ref[0] = u1[:, 0:512].astype(BF16)
    vc_ref[0] = _pairs_with_ones(u1[:, 512:1024]).astype(BF16)


def _ctx_layer(ctx, g0, s1p0, sh0, w_in0, qg, kg, mq, mk, conv_w, conv_b, ln_g, ln_b, w_out0, gate0,
               g1, s1p1, sh1, w_in1, kv_col_block):
    b, l, d = ctx.shape
    c = conv_w.shape[1]
    row = lambda bi: (bi, 0, 0)
    const2 = lambda bi: (0, 0)
    vec = pl.BlockSpec((1, d), const2)
    per_b = pl.BlockSpec((1, 1, d), row)
    cvec = pl.BlockSpec((1, c), const2)
    return pl.pallas_call(
        _ctx_layer_body,
        grid=(b,),
        in_specs=[pl.BlockSpec((1, l, d), row), vec, per_b, per_b,
                  pl.BlockSpec(w_in0.shape, const2),
                  pl.BlockSpec((1, 512), const2), pl.BlockSpec((1, 128), const2),
                  pl.BlockSpec((512, 512), const2), pl.BlockSpec((128, 128), const2),
                  pl.BlockSpec((CONV_WIDTH, SUBLANES, c), lambda bi: (0, 0, 0)), cvec, cvec, cvec,
                  pl.BlockSpec(w_out0.shape, const2), per_b,
                  vec, per_b, per_b,
                  pl.BlockSpec((d, 1024), lambda bi: (0, kv_col_block))],
        out_specs=[pl.BlockSpec((1, l, 256), row), pl.BlockSpec((1, l, 512), row),
                   pl.BlockSpec((1, l, 512), row), pl.BlockSpec((1, l, 1024), row)],
        out_shape=[jax.ShapeDtypeStruct((b, l, 256), BF16), jax.ShapeDtypeStruct((b, l, 512), BF16),
                   jax.ShapeDtypeStruct((b, l, 512), BF16), jax.ShapeDtypeStruct((b, l, 1024), BF16)],
        scratch_shapes=[pltpu.VMEM((l + 2 * HALO, c), F32),
                        pltpu.VMEM((SUBLANES, l + 2 * HALO - SUBLANES, c), F32),
                        pltpu.VMEM((l, c), F32)],
        compiler_params=_params(("arbitrary",)),
        name="ctx_layer",
    )(ctx, g0, s1p0, sh0, w_in0, qg, kg, mq, mk,
      jnp.broadcast_to(conv_w[:, None, :], (CONV_WIDTH, SUBLANES, c)), conv_b.reshape(1, c), ln_g.reshape(1, c),
      ln_b.reshape(1, c), w_out0, gate0, g1, s1p1, sh1, w_in1)


def _outproj_final_body(ya_ref, yb_ref, sz_ref, w_ref, x_ref, gate_ref, perm_ref, fg_ref, o_ref):
    stacked = jnp.concatenate([ya_ref[0, j] for j in range(ya_ref.shape[1])], axis=0)
    ya = _dot(perm_ref[...], stacked)
    y = jnp.concatenate([ya, yb_ref[0].astype(F32)], axis=-1) * sz_ref[0].astype(F32)
    xn = x_ref[0] + gate_ref[0] * _dot(y.astype(BF16), w_ref[...])
    o_ref[0] = xn * lax.rsqrt(jnp.mean(xn * xn, axis=-1, keepdims=True) + EPS) * fg_ref[...]


def _outproj_final(ya, yb, sz, w, x, gate, final_g, tm):
    b, s, d = x.shape
    row = lambda bi, i: (bi, i, 0)
    const = lambda bi, i: (0, 0)
    return pl.pallas_call(
        _outproj_final_body,
        grid=(b, s // tm),
        in_specs=[pl.BlockSpec((1, DFT_RADIX, tm // DFT_RADIX, 512), lambda bi, i: (bi, 0, i, 0)),
                  pl.BlockSpec((1, tm, 512), row),
                  pl.BlockSpec((1, tm, 1024), row),
                  pl.BlockSpec((1024, d), const),
                  pl.BlockSpec((1, tm, d), row),
                  pl.BlockSpec((1, 1, d), lambda bi, i: (bi, 0, 0)),
                  pl.BlockSpec((tm, tm), const),
                  pl.BlockSpec((1, d), const)],
        out_specs=pl.BlockSpec((1, tm, d), row),
        out_shape=jax.ShapeDtypeStruct((b, s, d), F32),
        compiler_params=_params(("arbitrary", "arbitrary")),
        name="outproj_final",
    )(ya, yb, sz, w, x, gate, jnp.asarray(_class_row_permutation(tm)).astype(BF16), final_g)


def _fourier_w_body(cc_ref, sc_ref, w_ref, o_ref):
    for g in range(w_ref.shape[0]):
        wc = jnp.dot(cc_ref[...], w_ref[g], preferred_element_type=F32, precision=HIGHEST)
        ws = jnp.dot(sc_ref[...], w_ref[g], preferred_element_type=F32, precision=HIGHEST)
        o_ref[g] = jnp.concatenate([wc, ws], axis=-1).astype(o_ref.dtype)


def _fourier_weights(cc, sc, w):
    g, c, _ = w.shape
    return pl.pallas_call(
        _fourier_w_body,
        out_shape=jax.ShapeDtypeStruct((g, c, 2 * c), BF16),
        name="fourier_weights",
    )(cc, sc, w)


def _fourier_tables(seq):
    quarter = seq // DFT_RADIX
    m = np.arange(quarter)
    tabs = []
    for j in range(DFT_RADIX):
        ang = 2.0 * np.pi * (((DFT_RADIX * m + j)[:, None] * m[None, :]) % seq) / seq
        tabs.append(np.concatenate([np.cos(ang), -np.sin(ang)], axis=1) / math.sqrt(seq))
    return np.stack(tabs).astype(np.float32)


def _class_row_permutation(rows):
    per = rows // DFT_RADIX
    p = np.zeros((rows, rows), np.float32)
    for j in range(DFT_RADIX):
        p[DFT_RADIX * np.arange(per) + j, j * per + np.arange(per)] = 1.0
    return p


def _fourier_body(f_ref, w1_ref, tab_ref, o_ref, yc_ref, ys_ref):
    groups, c, _ = w1_ref.shape
    quarter = f_ref.shape[1] // DFT_RADIX
    for g in range(groups):
        yg = _dot(f_ref[0, :, g * c:(g + 1) * c], w1_ref[g])
        yc_ref[:, g * c:(g + 1) * c] = yg[:, :c]
        ys_ref[:, g * c:(g + 1) * c] = yg[:, c:]
    for j in range(DFT_RADIX):
        pc, ps = [], []
        for q in range(DFT_RADIX):
            rows = slice(q * quarter, (q + 1) * quarter)
            a = (j * q) % 4
            if a == 0:
                pc.append((1, yc_ref, rows)), ps.append((1, ys_ref, rows))
            elif a == 1:
                pc.append((-1, ys_ref, rows)), ps.append((1, yc_ref, rows))
            elif a == 2:
                pc.append((-1, yc_ref, rows)), ps.append((-1, ys_ref, rows))
            else:
                pc.append((1, ys_ref, rows)), ps.append((-1, yc_ref, rows))
        folded = []
        for terms in (pc, ps):
            terms = sorted(terms, key=lambda t: -t[0])
            acc = terms[0][1][terms[0][2], :]
            for sign, ref, rows in terms[1:]:
                acc = acc + ref[rows, :] if sign > 0 else acc - ref[rows, :]
            folded.append(acc.astype(BF16))
        o_ref[0, j] = _dot(tab_ref[j], jnp.concatenate(folded, axis=0)).astype(o_ref.dtype)


def _fourier_mix(f, w1, tab):
    b, s, w = f.shape
    quarter = s // DFT_RADIX
    return pl.pallas_call(
        _fourier_body,
        grid=(b,),
        in_specs=[pl.BlockSpec((1, s, w), lambda bi: (bi, 0, 0)),
                  pl.BlockSpec(w1.shape, lambda bi: (0, 0, 0)),
                  pl.BlockSpec(tab.shape, lambda bi: (0, 0, 0))],
        out_specs=pl.BlockSpec((1, DFT_RADIX, quarter, w), lambda bi: (bi, 0, 0, 0)),
        out_shape=jax.ShapeDtypeStruct((b, DFT_RADIX, quarter, w), BF16),
        scratch_shapes=[pltpu.VMEM((s, w), F32), pltpu.VMEM((s, w), F32)],
        compiler_params=_params(("arbitrary",)),
        name="fourier_mix",
    )(f, w1, tab)


def _na_tile_geometry(rows_total):
    last_r0 = rows_total - NA_TILE_ROWS
    return ((0, 0), (2 * NA_TILE_ROWS, 2 * NA_TILE_ROWS - NA_ROWS // 2), (last_r0, rows_total - NA_WIN_ROWS))


def _na_bias_body(rows_total, rpb_ref, o_ref, t2_ref):
    h = pl.program_id(0)
    wq = lax.broadcasted_iota(jnp.int32, (GRID_W, LANES), 0)
    wk = lax.broadcasted_iota(jnp.int32, (GRID_W, LANES), 1) & (GRID_W - 1)
    cs = jnp.clip(wq - NA_COLS // 2, 0, GRID_W - NA_COLS)
    col_ok = (wk >= cs) & (wk < cs + NA_COLS)
    dc = wk - wq + (NA_COLS - 1)
    n_dr, n_dc = 2 * NA_ROWS - 1, 2 * NA_COLS - 1
    neg = jnp.full((GRID_W, LANES), NEG_INF, F32)
    for dr in range(n_dr):
        val = neg
        for c in range(n_dc):
            val = jnp.where(col_ok & (dc == c), rpb_ref[h * (n_dr * n_dc) + dr * n_dc + c] * LOG2E, val)
        t2_ref[dr] = val
    lane = lax.broadcasted_iota(jnp.int32, (GRID_W, LANES), 1)
    left = lane < GRID_W
    for ty, (r0, ws) in enumerate(_na_tile_geometry(rows_total)):
        for i in range(NA_TILE_ROWS):
            r = r0 + i
            rs = min(max(r - NA_ROWS // 2, 0), rows_total - NA_ROWS)
            for jp in range(NA_WIN_ROWS // 2):
                halves = []
                for kr in (ws + 2 * jp, ws + 2 * jp + 1):
                    halves.append(t2_ref[kr - r + NA_ROWS - 1] if rs <= kr < rs + NA_ROWS else neg)
                o_ref[0, ty, i * GRID_W:(i + 1) * GRID_W, jp * LANES:(jp + 1) * LANES] = jnp.where(left, halves[0], halves[1])


def _na_bias_tables(rpb, rows_total):
    h = rpb.shape[0]
    q_rows, k_cols = NA_TILE_ROWS * GRID_W, NA_WIN_ROWS * GRID_W
    return pl.pallas_call(
        functools.partial(_na_bias_body, rows_total),
        grid=(h,),
        in_specs=[pl.BlockSpec(memory_space=pltpu.SMEM)],
        out_specs=pl.BlockSpec((1, 3, q_rows, k_cols), lambda hi: (hi, 0, 0, 0)),
        out_shape=jax.ShapeDtypeStruct((h, 3, q_rows, k_cols), F32),
        scratch_shapes=[pltpu.VMEM((2 * NA_ROWS - 1, GRID_W, LANES), F32)],
        compiler_params=_params(("arbitrary",)),
        name="na_bias_tables",
    )(rpb.reshape(-1))


def _na_body(rows_total, q_ref, k_ref, v_ref, kc_ref, vc_ref, bias_ref, o_ref):
    t = pl.program_id(2)
    n_t = pl.num_programs(2)
    ty = jnp.where(t == 0, 0, jnp.where(t == n_t - 1, 2, 1))
    ws = jnp.clip(t * NA_TILE_ROWS - NA_ROWS // 2, 0, rows_total - NA_WIN_ROWS)
    off = pl.multiple_of(ws * GRID_W, GRID_W)
    n_win = NA_WIN_ROWS * GRID_W
    bias = jnp.concatenate([bias_ref[0, ty], bias_ref[1, ty]], axis=0)
    for bb in range(q_ref.shape[0]):
        chunks = [(kc_ref[bb], vc_ref[bb], None),
                  (k_ref[bb, pl.ds(off, n_win), :], v_ref[bb, pl.ds(off, n_win), :], bias)]
        o_ref[bb] = _online_softmax_pv(_split_heads(q_ref[bb]), chunks, online=False).astype(o_ref.dtype)


def _neighborhood_attention(q, k, v, kc, vc, bias):
    b, s, w = q.shape
    l = kc.shape[1]
    rows_total = s // GRID_W
    tq = NA_TILE_ROWS * GRID_W
    nb = NA_BATCH_PER_STEP
    full = lambda j, bi, t: (bi, 0, j)
    return pl.pallas_call(
        functools.partial(_na_body, rows_total),
        grid=(w // LANES, b // nb, s // tq),
        in_specs=[pl.BlockSpec((nb, tq, LANES), lambda j, bi, t: (bi, t, j)),
                  pl.BlockSpec((nb, s, LANES), full),
                  pl.BlockSpec((nb, s, 2 * LANES), full),
                  pl.BlockSpec((nb, l, LANES), full),
                  pl.BlockSpec((nb, l, 2 * LANES), full),
                  pl.BlockSpec((2,) + bias.shape[1:], lambda j, bi, t: (j, 0, 0, 0))],
        out_specs=pl.BlockSpec((nb, tq, LANES), lambda j, bi, t: (bi, t, j)),
        out_shape=jax.ShapeDtypeStruct((b, s, w), BF16),
        compiler_params=_params(("arbitrary", "arbitrary", "arbitrary")),
        name="neighborhood_attention",
    )(q, k, v, kc, vc, bias)


def _modulation(m, rows, batch):
    d = m.shape[1] // 3
    pick = lambda lo: jnp.broadcast_to(m[rows, lo:lo + d].reshape(-1, 1, d), (batch, 1, d))
    return pick(0), 1.0 + pick(d), pick(2 * d)


def kernel(x, c, ctx, c_ctx, ab_w_ada, ab_b_ada, ab_norm_g, ab_w_in, ab_conv_w, ab_conv_b, ab_ln_g, ab_ln_b, ab_q_norm_g, ab_k_norm_g, ab_w_out, cd_w_ada, cd_b_ada, cd_norm_g, cd_w_in, cd_w_fourier, cd_rpb, cd_w_out, final_norm_g):
    b, s, d = x.shape
    l = ctx.shape[1]
    assert d == D_MODEL and s % (NA_TILE_ROWS * GRID_W) == 0 and ab_w_ada.shape[0] == 1 and cd_w_ada.shape[0] == 1

    cond = jnp.zeros((16, d), F32).at[:b].set(c).at[b].set(c_ctx)
    lat = slice(0, b)
    cx = slice(b, b + 1)

    m0 = _adaln(cond, ab_w_ada[0], ab_b_ada[0])
    m1 = _adaln(cond, cd_w_ada[0], cd_b_ada[0])
    sh, s1p, gate = _modulation(m0, lat, b)
    csh, cs1p, cgate = _modulation(m0, cx, b)
    sh1, s1p1, gate1 = _modulation(m1, lat, b)
    csh1, cs1p1, _ = _modulation(m1, cx, b)
    ng1 = cd_norm_g[0].reshape(1, d)
    w_in1 = cd_w_in[0].astype(BF16)

    w_in0 = ab_w_in[0].astype(BF16)
    w_out0 = ab_w_out[0].astype(BF16)

    cos, sin = _rope_tables(s)
    cos, sin = jnp.asarray(cos), jnp.asarray(sin)
    mq = jnp.asarray(_block_ones(512)).astype(BF16)
    mk = jnp.asarray(_block_ones(128)).astype(BF16)
    qg = jnp.tile(ab_q_norm_g[0], 512 // HEAD_DIM).reshape(1, 512)
    kg = jnp.tile(ab_k_norm_g[0], 128 // HEAD_DIM).reshape(1, 128)
    ng0 = ab_norm_g[0].reshape(1, d)

    ck, cv, kc, vc = _ctx_layer(ctx, ng0, cs1p, csh, w_in0, qg, kg, mq, mk, ab_conv_w[0], ab_conv_b[0], ab_ln_g[0],
                                ab_ln_b[0], w_out0, cgate, ng1, cs1p1, csh1, w_in1, 1)

    glu, q, k, v, sz = _inproj_ab(x, ng0, s1p, sh, w_in0, qg, kg, cos, sin, mq, mk, 1024)
    y_b = _gqa_attention(q, [ck, k], [cv, v], 1024, 1, chunk=128)
    y_a = _conformer_conv(glu, ab_conv_w[0], ab_conv_b[0], ab_ln_g[0], ab_ln_b[0], 256)

    x1, f, q, k, v, sz = _outproj_inproj_cd(y_a, y_b, sz, w_out0, x, gate, ng1, s1p1, sh1, w_in1, 1024)
    gate = gate1

    c_c, s_c = _dft_tables(cd_w_fourier.shape[-1])
    w1 = _fourier_weights(jnp.asarray(c_c), jnp.asarray(s_c), cd_w_fourier[0])
    y_c = _fourier_mix(f, w1, jnp.asarray(_fourier_tables(s)).astype(BF16))

    bias = _na_bias_tables(cd_rpb[0], s // GRID_W)
    y_d = _neighborhood_attention(q, k, v, kc, vc, bias)

    return _outproj_final(y_c, y_d, sz, cd_w_out[0].astype(BF16), x1, gate, final_norm_g.reshape(1, d), 512)
```

```python
import functools
import math

import numpy as np
import jax
import jax.numpy as jnp
from jax import lax
from jax.experimental import pallas as pl
from jax.experimental.pallas import tpu as pltpu

F32 = jnp.float32
BF16 = jnp.bfloat16
HIGHEST = lax.Precision.HIGHEST

D_MODEL = 1024
HEAD_DIM = 64
GRID_W = 64
CONV_WIDTH = 31
NA_ROWS = 8
NA_COLS = 16
ROPE_THETA = 10000.0
EPS = 1e-6
NEG_INF = -1e30
LOG2E = math.log2(math.e)
Q_SCALE = HEAD_DIM ** -0.5 * LOG2E
LANES = 128
SUBLANES = 8
HALO = 16
VMEM_LIMIT = 48 * 1024 * 1024

NA_TILE_ROWS = 4
NA_WIN_ROWS = NA_TILE_ROWS + NA_ROWS
NA_BATCH_PER_STEP = 8
IN_PROJ_SUBTILE = 256
OUT_PROJ_SUBTILE = 512
DFT_RADIX = 4


def _params(sem, flags=None):
    return pltpu.CompilerParams(dimension_semantics=sem, vmem_limit_bytes=VMEM_LIMIT, flags=flags)


def _silu(t):
    return t * jax.nn.sigmoid(t)


def _dot(a, b):
    return jnp.dot(a, b, preferred_element_type=F32)


def _dot_nt(a, b):
    return lax.dot_general(a, b, (((1,), (1,)), ((), ())), preferred_element_type=F32)


def _rope_tables(seq):
    t = np.arange(seq)
    row = (t // GRID_W).astype(np.float64)
    col = (t % GRID_W).astype(np.float64)
    nf = HEAD_DIM // 4
    inv = np.float32(ROPE_THETA) ** (-np.arange(nf, dtype=np.float32) / nf)
    ang = np.concatenate([row[:, None] * inv, col[:, None] * inv], axis=-1)
    cos = np.repeat(np.cos(ang), 2, axis=-1)
    sin = np.repeat(np.sin(ang), 2, axis=-1)
    sign = np.tile(np.array([-1.0, 1.0]), HEAD_DIM // 2)
    cos = np.tile(cos, (1, LANES // HEAD_DIM))
    sin = np.tile(sin * sign, (1, LANES // HEAD_DIM))
    return cos.astype(np.float32), sin.astype(np.float32)


def _dft_tables(n):
    k = np.arange(n)
    ang = 2.0 * np.pi * ((k[:, None] * k[None, :]) % n) / n
    s = 1.0 / math.sqrt(n)
    return (np.cos(ang) * s).astype(np.float32), (np.sin(ang) * s).astype(np.float32)


def _block_ones(width):
    h = np.arange(width) // HEAD_DIM
    return (h[:, None] == h[None, :]).astype(np.float32)


def _adaln_body(c_ref, w_ref, b_ref, o_ref):
    a = _silu(c_ref[...]).astype(BF16)
    o_ref[...] = _dot(a, w_ref[...].astype(BF16)) + b_ref[...]


def _adaln(cond, w, b):
    r, d = cond.shape
    n = w.shape[1]
    tn = 1024
    return pl.pallas_call(
        _adaln_body,
        grid=(n // tn,),
        in_specs=[pl.BlockSpec((r, d), lambda j: (0, 0)),
                  pl.BlockSpec((d, tn), lambda j: (0, j)),
                  pl.BlockSpec((1, tn), lambda j: (0, j))],
        out_specs=pl.BlockSpec((r, tn), lambda j: (0, j)),
        out_shape=jax.ShapeDtypeStruct((r, n), F32),
        compiler_params=_params(("arbitrary",)),
        name="adaln",
    )(cond, w, b.reshape(1, n))


def _head_rms(t, ones_ref, gain):
    ms = _dot((t * t).astype(BF16), ones_ref[...]) * (1.0 / HEAD_DIM)
    return t * lax.rsqrt(ms + EPS) * gain


def _rope(t, cos, sin_signed):
    rows = t.shape[0]
    lane = lax.broadcasted_iota(jnp.int32, (rows, LANES), 1)
    even = (lane & 1) == 0
    outs = []
    for c in range(t.shape[1] // LANES):
        xc = t[:, c * LANES:(c + 1) * LANES]
        swapped = jnp.where(even, pltpu.roll(xc, LANES - 1, 1), pltpu.roll(xc, 1, 1))
        outs.append(xc * cos + swapped * sin_signed)
    return outs[0] if len(outs) == 1 else jnp.concatenate(outs, axis=1)


def _modulate(x, g_ref, s1p_ref, sh_ref):
    y = x * lax.rsqrt(jnp.mean(x * x, axis=-1, keepdims=True) + EPS) * g_ref[...]
    return (y * s1p_ref[0] + sh_ref[0]).astype(BF16)


def _row_subtiles(tm):
    sub = min(tm, IN_PROJ_SUBTILE)
    return [slice(r, r + sub) for r in range(0, tm, sub)]


def _inproj_ab_body(x_ref, g_ref, s1p_ref, sh_ref, w_ref, qg_ref, kg_ref, cos_ref, sin_ref, mq_ref, mk_ref,
                    glu_ref, q_ref, k_ref, v_ref, sz_ref):
    for rows in _row_subtiles(x_ref.shape[1]):
        u = _dot(_modulate(x_ref[0, rows, :], g_ref, s1p_ref, sh_ref), w_ref[...])
        glu, q, k, v, sz = _ab_epilogue(u, mq_ref, mk_ref, qg_ref, kg_ref, (cos_ref[rows, :], sin_ref[rows, :]))
        glu_ref[0, rows, :] = glu
        q_ref[0, rows, :] = q
        k_ref[0, rows, :] = k
        v_ref[0, rows, :] = v
        sz_ref[0, rows, :] = sz


def _ab_epilogue(u, mq_ref, mk_ref, qg_ref, kg_ref, rope_tabs):
    glu = u[:, 0:512] * jax.nn.sigmoid(u[:, 512:1024])
    q = _head_rms(u[:, 1024:1536], mq_ref, qg_ref[...])
    k = _head_rms(u[:, 1536:1664], mk_ref, kg_ref[...])
    if rope_tabs is not None:
        q = _rope(q, *rope_tabs)
        k = _rope(k, *rope_tabs)
    v = u[:, 1664:1792]
    k_sw, v_sw = pltpu.roll(k, HEAD_DIM, 1), pltpu.roll(v, HEAD_DIM, 1)
    left = lax.broadcasted_iota(jnp.int32, k.shape, 1) < HEAD_DIM
    one = jnp.ones_like(v)
    k_dup = jnp.concatenate([jnp.where(left, k, k_sw), jnp.where(left, k_sw, k)], axis=1)
    v_dup = jnp.concatenate([jnp.where(left, v, one), jnp.where(left, one, v_sw),
                             jnp.where(left, v_sw, one), jnp.where(left, one, v)], axis=1)
    return (glu, (q * Q_SCALE).astype(BF16), k_dup.astype(BF16), v_dup.astype(BF16),
            _silu(u[:, 1792:2816]).astype(BF16))


def _inproj_ab(x, g, s1p, sh, w, qg, kg, cos, sin, mq, mk, tm):
    b, s, d = x.shape
    n = w.shape[1]
    row = lambda bi, i: (bi, i, 0)
    const = lambda bi, i: (0, 0)
    per_b = lambda bi, i: (bi, 0, 0)
    tab = lambda bi, i: (i, 0)
    return pl.pallas_call(
        _inproj_ab_body,
        grid=(b, s // tm),
        in_specs=[pl.BlockSpec((1, tm, d), row),
                  pl.BlockSpec((1, d), const),
                  pl.BlockSpec((1, 1, d), per_b),
                  pl.BlockSpec((1, 1, d), per_b),
                  pl.BlockSpec((d, n), const),
                  pl.BlockSpec((1, 512), const),
                  pl.BlockSpec((1, 128), const),
                  pl.BlockSpec((tm, LANES), tab),
                  pl.BlockSpec((tm, LANES), tab),
                  pl.BlockSpec((512, 512), const),
                  pl.BlockSpec((128, 128), const)],
        out_specs=[pl.BlockSpec((1, tm, 512), row),
                   pl.BlockSpec((1, tm, 512), row),
                   pl.BlockSpec((1, tm, 256), row),
                   pl.BlockSpec((1, tm, 512), row),
                   pl.BlockSpec((1, tm, 1024), row)],
        out_shape=[jax.ShapeDtypeStruct((b, s, 512), F32),
                   jax.ShapeDtypeStruct((b, s, 512), BF16),
                   jax.ShapeDtypeStruct((b, s, 256), BF16),
                   jax.ShapeDtypeStruct((b, s, 512), BF16),
                   jax.ShapeDtypeStruct((b, s, 1024), BF16)],
        compiler_params=_params(("arbitrary", "arbitrary")),
        name="inproj_ab",
    )(x, g, s1p, sh, w, qg, kg, cos, sin, mq, mk)


def _outproj_inproj_cd_body(ya_ref, yb_ref, sz0_ref, wo_ref, x_ref, gate_ref, g_ref, s1p_ref, sh_ref, wi_ref,
                            x1_ref, f_ref, q_ref, k_ref, v_ref, sz_ref):
    for rows in _row_subtiles(x_ref.shape[1]):
        y = (jnp.concatenate([ya_ref[0, rows, :], yb_ref[0, rows, :]], axis=-1).astype(F32)
             * sz0_ref[0, rows, :].astype(F32))
        xn = x_ref[0, rows, :] + gate_ref[0] * _dot(y.astype(BF16), wo_ref[...])
        x1_ref[0, rows, :] = xn
        u = _dot(_modulate(xn, g_ref, s1p_ref, sh_ref), wi_ref[...])
        f_ref[0, rows, :] = u[:, 0:512].astype(BF16)
        q_ref[0, rows, :] = (u[:, 512:1024] * Q_SCALE).astype(BF16)
        k_ref[0, rows, :] = u[:, 1024:1536].astype(BF16)
        v_ref[0, rows, :] = _pairs_with_ones(u[:, 1536:2048]).astype(BF16)
        sz_ref[0, rows, :] = _silu(u[:, 2048:3072]).astype(BF16)


def _outproj_inproj_cd(ya, yb, sz0, w_out, x, gate, g, s1p, sh, w_in, tm):
    b, s, d = x.shape
    n = w_in.shape[1]
    row = lambda bi, i: (bi, i, 0)
    const = lambda bi, i: (0, 0)
    per_b = lambda bi, i: (bi, 0, 0)
    return pl.pallas_call(
        _outproj_inproj_cd_body,
        grid=(b, s // tm),
        in_specs=[pl.BlockSpec((1, tm, 512), row),
                  pl.BlockSpec((1, tm, 512), row),
                  pl.BlockSpec((1, tm, 1024), row),
                  pl.BlockSpec((1024, d), const, pipeline_mode=pl.Buffered(1)),
                  pl.BlockSpec((1, tm, d), row),
                  pl.BlockSpec((1, 1, d), per_b),
                  pl.BlockSpec((1, d), const),
                  pl.BlockSpec((1, 1, d), per_b),
                  pl.BlockSpec((1, 1, d), per_b),
                  pl.BlockSpec((d, n), const, pipeline_mode=pl.Buffered(1))],
        out_specs=[pl.BlockSpec((1, tm, d), row)] + [pl.BlockSpec((1, tm, 512), row)] * 3
        + [pl.BlockSpec((1, tm, 1024), row)] * 2,
        out_shape=[jax.ShapeDtypeStruct((b, s, d), F32)] + [jax.ShapeDtypeStruct((b, s, 512), BF16)] * 3
        + [jax.ShapeDtypeStruct((b, s, 1024), BF16)] * 2,
        compiler_params=_params(("arbitrary", "arbitrary")),
        name="outproj_inproj_cd",
    )(ya, yb, sz0, w_out, x, gate, g, s1p, sh, w_in)


def _split_heads(q):
    lane = lax.broadcasted_iota(jnp.int32, q.shape, 1)
    zero = jnp.zeros_like(q)
    return jnp.concatenate([jnp.where(lane < HEAD_DIM, q, zero), jnp.where(lane >= HEAD_DIM, q, zero)], axis=0)


def _with_ones(v):
    lane = lax.broadcasted_iota(jnp.int32, v.shape, 1)
    one = jnp.ones_like(v)
    return jnp.concatenate([jnp.where(lane < HEAD_DIM, v, one), jnp.where(lane < HEAD_DIM, one, v)], axis=1)


def _pairs_with_ones(v):
    return jnp.concatenate([_with_ones(v[:, c:c + LANES]) for c in range(0, v.shape[1], LANES)], axis=1)


def _pv(p, v2, rows):
    pv = _dot(p.astype(BF16), v2)
    return jnp.concatenate([pv[:rows, :LANES], pv[rows:, LANES:]], axis=0)


def _online_softmax_pv(q2, chunks, online=True):
    rows = q2.shape[0] // 2
    scores = [_dot_nt(q2, k) if bias is None else _dot_nt(q2, k) + bias for k, _, bias in chunks]
    if online:
        m = acc = None
        for s, (_, v2, _) in zip(scores, chunks):
            mc = jnp.max(s, axis=-1, keepdims=True)
            m_new = mc if m is None else jnp.maximum(m, mc)
            pv = _pv(jnp.exp2(s - m_new), v2, rows)
            acc = pv if m is None else jnp.exp2(m - m_new) * acc + pv
            m = m_new
    else:
        m = functools.reduce(jnp.maximum, [jnp.max(s, axis=-1, keepdims=True) for s in scores])
        acc = functools.reduce(jnp.add, [_pv(jnp.exp2(s - m), v2, rows) for s, (_, v2, _) in zip(scores, chunks)])
    left = lax.broadcasted_iota(jnp.int32, (rows, LANES), 1) < HEAD_DIM
    top, bot = acc[:rows], acc[rows:]
    return jnp.where(left, top, bot) / pltpu.roll(jnp.where(left, bot, top), HEAD_DIM, 1)


V_ROWS = 80

def _softmax_pv_keys_major(q2, chunks):
    half = q2.shape[0] // 2
    scores = [_dot_nt(k, q2) for k, _ in chunks]
    m = acc = None
    for s, (_, v_ones) in zip(scores, chunks):
        mc = jnp.max(s, axis=0, keepdims=True)
        m_new = mc if m is None else jnp.maximum(m, mc)
        pv = _dot(v_ones.T[:V_ROWS], jnp.exp2(s - m_new).astype(BF16))
        acc = pv if m is None else jnp.exp2(m - m_new) * acc + pv
        m = m_new
    o = acc[:HEAD_DIM] / acc[HEAD_DIM:HEAD_DIM + 1]
    return jnp.concatenate([o[:, :half], o[:, half:]], axis=0).T


def _key_chunks(k_ref, v_ref, kv, chunk):
    n = k_ref.shape[1]
    kl, vl = slice(kv * LANES, (kv + 1) * LANES), slice(kv * 2 * LANES, (kv + 1) * 2 * LANES)
    return [(k_ref[0, c:c + chunk, kl], v_ref[0, c:c + chunk, vl], None) for c in range(0, n, chunk)]


def _gqa_body(n_src, chunk, q_ref, *refs):
    k_refs, v_refs, o_ref = refs[:n_src], refs[n_src:2 * n_src], refs[2 * n_src]
    pairs = q_ref.shape[2] // LANES
    kv_heads = k_refs[0].shape[2] // LANES
    for p in range(pairs):
        lanes = slice(p * LANES, (p + 1) * LANES)
        kv = p * kv_heads // pairs
        chunks = [c for k, v in zip(k_refs, v_refs) for c in _key_chunks(k, v, kv, min(chunk, k.shape[1]))]
        chunks = [(k, v2[:, :LANES]) for k, v2, _ in chunks]
        o_ref[0, :, lanes] = _softmax_pv_keys_major(_split_heads(q_ref[0, :, lanes]), chunks).astype(o_ref.dtype)


def _gqa_attention(q, ks, vs, tq, pairs, chunk=512):
    b, lq, w = q.shape
    n_src = len(ks)
    n_pairs = w // LANES
    kv_heads = ks[0].shape[2] // LANES
    kv_blk = max(1, pairs * kv_heads // n_pairs)
    kv_idx = lambda bi, i, j: (bi, 0, j * pairs * kv_heads // n_pairs // kv_blk)
    k_specs = [pl.BlockSpec((1, k.shape[1], kv_blk * LANES), kv_idx) for k in ks]
    v_specs = [pl.BlockSpec((1, v.shape[1], kv_blk * 2 * LANES), kv_idx) for v in vs]
    return pl.pallas_call(
        functools.partial(_gqa_body, n_src, chunk),
        grid=(b, lq // tq, n_pairs // pairs),
        in_specs=[pl.BlockSpec((1, tq, pairs * LANES), lambda bi, i, j: (bi, i, j))] + k_specs + v_specs,
        out_specs=pl.BlockSpec((1, tq, pairs * LANES), lambda bi, i, j: (bi, i, j)),
        out_shape=jax.ShapeDtypeStruct((b, lq, w), BF16),
        compiler_params=_params(("arbitrary", "arbitrary", "arbitrary")),
        name=f"gqa_attention_{n_src}src",
    )(q, *ks, *vs)


def _conv_core(rows, pad_ref, shift_ref, w_ref, cb_ref, lg_ref, lb_ref, store):
    span = shift_ref.shape[1]
    for r in range(1, SUBLANES):
        shift_ref[r] = pad_ref[r:r + span, :]
    sub = 32
    base = HALO - CONV_WIDTH // 2
    for r0 in range(0, rows, sub):
        acc = jnp.zeros((sub, pad_ref.shape[1]), F32) + cb_ref[...]
        for k in range(CONV_WIDTH):
            a, r = divmod(base + k, SUBLANES)
            lo = a * SUBLANES + r0
            tap = pad_ref[lo:lo + sub, :] if r == 0 else shift_ref[r, lo:lo + sub, :]
            acc = acc + tap * jnp.concatenate([w_ref[k]] * (sub // SUBLANES), axis=0)
        mu = jnp.mean(acc, axis=-1, keepdims=True)
        xc = acc - mu
        y = xc * lax.rsqrt(jnp.mean(xc * xc, axis=-1, keepdims=True) + EPS) * lg_ref[...] + lb_ref[...]
        store(r0, sub, _silu(y))


def _conv_body(rows, x_ref, prev_ref, next_ref, w_ref, cb_ref, lg_ref, lb_ref, o_ref, pad_ref, shift_ref):
    i = pl.program_id(1)
    last = pl.num_programs(1) - 1
    pad_ref[0:HALO, :] = jnp.where(i > 0, prev_ref[0], 0.0)
    pad_ref[HALO:HALO + rows, :] = x_ref[0]
    pad_ref[HALO + rows:HALO + rows + HALO, :] = jnp.where(i < last, next_ref[0], 0.0)

    def store(r0, sub, y):
        o_ref[0, r0:r0 + sub, :] = y.astype(o_ref.dtype)

    _conv_core(rows, pad_ref, shift_ref, w_ref, cb_ref, lg_ref, lb_ref, store)


def _conformer_conv(u, conv_w, conv_b, ln_g, ln_b, rows):
    b, l, c = u.shape
    hb = rows // HALO
    n_halo = l // HALO
    return pl.pallas_call(
        functools.partial(_conv_body, rows),
        grid=(b, l // rows),
        in_specs=[pl.BlockSpec((1, rows, c), lambda bi, i: (bi, i, 0)),
                  pl.BlockSpec((1, HALO, c), lambda bi, i: (bi, jnp.maximum(i * hb - 1, 0), 0)),
                  pl.BlockSpec((1, HALO, c), lambda bi, i: (bi, jnp.minimum((i + 1) * hb, n_halo - 1), 0)),
                  pl.BlockSpec((CONV_WIDTH, SUBLANES, c), lambda bi, i: (0, 0, 0)),
                  pl.BlockSpec((1, c), lambda bi, i: (0, 0)),
                  pl.BlockSpec((1, c), lambda bi, i: (0, 0)),
                  pl.BlockSpec((1, c), lambda bi, i: (0, 0))],
        out_specs=pl.BlockSpec((1, rows, c), lambda bi, i: (bi, i, 0)),
        out_shape=jax.ShapeDtypeStruct((b, l, c), BF16),
        scratch_shapes=[pltpu.VMEM((rows + 2 * HALO, c), F32),
                        pltpu.VMEM((SUBLANES, rows + 2 * HALO - SUBLANES, c), F32)],
        compiler_params=_params(("arbitrary", "arbitrary")),
        name="conformer_conv",
    )(u, u, u, jnp.broadcast_to(conv_w[:, None, :], (CONV_WIDTH, SUBLANES, c)),
      conv_b.reshape(1, c), ln_g.reshape(1, c), ln_b.reshape(1, c))


def _ctx_layer_body(x_ref, g0_ref, s1p0_ref, sh0_ref, wi0_ref, qg_ref, kg_ref, mq_ref, mk_ref, cw_ref, cb_ref,
                    lg_ref, lb_ref, wo0_ref, gate0_ref, g1_ref, s1p1_ref, sh1_ref, wkv1_ref,
                    ck_ref, cv_ref, kc_ref, vc_ref, pad_ref, shift_ref, ya_ref):
    rows = x_ref.shape[1]
    x = x_ref[0]
    u = _dot(_modulate(x, g0_ref, s1p0_ref, sh0_ref), wi0_ref[...])
    glu, q, k_dup, v_dup, sz = _ab_epilogue(u, mq_ref, mk_ref, qg_ref, kg_ref, None)
    ck_ref[0] = k_dup
    cv_ref[0] = v_dup
    pairs, kv_heads = q.shape[1] // LANES, k_dup.shape[1] // LANES
    yb = []
    for p in range(pairs):
        kv = p * kv_heads // pairs
        keys = [(k_dup[:, kv * LANES:(kv + 1) * LANES], v_dup[:, kv * 2 * LANES:(kv + 1) * 2 * LANES], None)]
        yb.append(_online_softmax_pv(_split_heads(q[:, p * LANES:(p + 1) * LANES]), keys, online=False))
    zeros = jnp.zeros((HALO, glu.shape[1]), F32)
    pad_ref[0:HALO, :] = zeros
    pad_ref[HALO:HALO + rows, :] = glu
    pad_ref[HALO + rows:HALO + rows + HALO, :] = zeros

    def store(r0, sub, y):
        ya_ref[r0:r0 + sub, :] = y

    _conv_core(rows, pad_ref, shift_ref, cw_ref, cb_ref, lg_ref, lb_ref, store)
    y = jnp.concatenate([ya_ref[...]] + yb, axis=-1) * sz.astype(F32)
    x1 = x + gate0_ref[0] * _dot(y.astype(BF16), wo0_ref[...])
    u1 = _dot(_modulate(x1, g1_ref, s1p1_ref, sh1_ref), wkv1_ref[...])
    kc_ref[0] = u1[:, 0:512].astype(BF16)
    vc_ref[0] = _pairs_with_ones(u1[:, 512:1024]).astype(BF16)


def _ctx_layer(ctx, g0, s1p0, sh0, w_in0, qg, kg, mq, mk, conv_w, conv_b, ln_g, ln_b, w_out0, gate0,
               g1, s1p1, sh1, w_in1, kv_col_block):
    b, l, d = ctx.shape
    c = conv_w.shape[1]
    row = lambda bi: (bi, 0, 0)
    const2 = lambda bi: (0, 0)
    vec = pl.BlockSpec((1, d), const2)
    per_b = pl.BlockSpec((1, 1, d), row)
    cvec = pl.BlockSpec((1, c), const2)
    return pl.pallas_call(
        _ctx_layer_body,
        grid=(b,),
        in_specs=[pl.BlockSpec((1, l, d), row), vec, per_b, per_b,
                  pl.BlockSpec(w_in0.shape, const2),
                  pl.BlockSpec((1, 512), const2), pl.BlockSpec((1, 128), const2),
                  pl.BlockSpec((512, 512), const2), pl.BlockSpec((128, 128), const2),
                  pl.BlockSpec((CONV_WIDTH, SUBLANES, c), lambda bi: (0, 0, 0)), cvec, cvec, cvec,
                  pl.BlockSpec(w_out0.shape, const2), per_b,
                  vec, per_b, per_b,
                  pl.BlockSpec((d, 1024), lambda bi: (0, kv_col_block))],
        out_specs=[pl.BlockSpec((1, l, 256), row), pl.BlockSpec((1, l, 512), row),
                   pl.BlockSpec((1, l, 512), row), pl.BlockSpec((1, l, 1024), row)],
        out_shape=[jax.ShapeDtypeStruct((b, l, 256), BF16), jax.ShapeDtypeStruct((b, l, 512), BF16),
                   jax.ShapeDtypeStruct((b, l, 512), BF16), jax.ShapeDtypeStruct((b, l, 1024), BF16)],
        scratch_shapes=[pltpu.VMEM((l + 2 * HALO, c), F32),
                        pltpu.VMEM((SUBLANES, l + 2 * HALO - SUBLANES, c), F32),
                        pltpu.VMEM((l, c), F32)],
        compiler_params=_params(("arbitrary",)),
        name="ctx_layer",
    )(ctx, g0, s1p0, sh0, w_in0, qg, kg, mq, mk,
      jnp.broadcast_to(conv_w[:, None, :], (CONV_WIDTH, SUBLANES, c)), conv_b.reshape(1, c), ln_g.reshape(1, c),
      ln_b.reshape(1, c), w_out0, gate0, g1, s1p1, sh1, w_in1)


def _outproj_final_body(ya_ref, yb_ref, sz_ref, w_ref, x_ref, gate_ref, perm_ref, fg_ref, o_ref):
    sub = perm_ref.shape[0]
    per = sub // DFT_RADIX
    for h in range(x_ref.shape[1] // sub):
        rows = slice(h * sub, (h + 1) * sub)
        stacked = jnp.concatenate([ya_ref[0, j, h * per:(h + 1) * per, :] for j in range(DFT_RADIX)], axis=0)
        ya = _dot(perm_ref[...], stacked)
        y = jnp.concatenate([ya, yb_ref[0, rows, :].astype(F32)], axis=-1) * sz_ref[0, rows, :].astype(F32)
        xn = x_ref[0, rows, :] + gate_ref[0] * _dot(y.astype(BF16), w_ref[...])
        o_ref[0, rows, :] = xn * lax.rsqrt(jnp.mean(xn * xn, axis=-1, keepdims=True) + EPS) * fg_ref[...]


def _outproj_final(ya, yb, sz, w, x, gate, final_g, tm):
    b, s, d = x.shape
    row = lambda bi, i: (bi, i, 0)
    const = lambda bi, i: (0, 0)
    return pl.pallas_call(
        _outproj_final_body,
        grid=(b, s // tm),
        in_specs=[pl.BlockSpec((1, DFT_RADIX, tm // DFT_RADIX, 512), lambda bi, i: (bi, 0, i, 0)),
                  pl.BlockSpec((1, tm, 512), row),
                  pl.BlockSpec((1, tm, 1024), row),
                  pl.BlockSpec((1024, d), const),
                  pl.BlockSpec((1, tm, d), row),
                  pl.BlockSpec((1, 1, d), lambda bi, i: (bi, 0, 0)),
                  pl.BlockSpec((OUT_PROJ_SUBTILE, OUT_PROJ_SUBTILE), const),
                  pl.BlockSpec((1, d), const)],
        out_specs=pl.BlockSpec((1, tm, d), row),
        out_shape=jax.ShapeDtypeStruct((b, s, d), F32),
        compiler_params=_params(("arbitrary", "arbitrary")),
        name="outproj_final",
    )(ya, yb, sz, w, x, gate, jnp.asarray(_class_row_permutation(OUT_PROJ_SUBTILE)).astype(BF16), final_g)


def _fourier_w_body(cc_ref, sc_ref, w_ref, o_ref):
    for g in range(w_ref.shape[0]):
        wc = jnp.dot(cc_ref[...], w_ref[g], preferred_element_type=F32, precision=HIGHEST)
        ws = jnp.dot(sc_ref[...], w_ref[g], preferred_element_type=F32, precision=HIGHEST)
        o_ref[g] = jnp.concatenate([wc, ws], axis=-1).astype(o_ref.dtype)


def _fourier_weights(cc, sc, w):
    g, c, _ = w.shape
    return pl.pallas_call(
        _fourier_w_body,
        out_shape=jax.ShapeDtypeStruct((g, c, 2 * c), BF16),
        name="fourier_weights",
    )(cc, sc, w)


def _fourier_tables(seq):
    quarter = seq // DFT_RADIX
    m = np.arange(quarter)
    tabs = []
    for j in range(DFT_RADIX):
        ang = 2.0 * np.pi * (((DFT_RADIX * m + j)[:, None] * m[None, :]) % seq) / seq
        tabs.append(np.concatenate([np.cos(ang), -np.sin(ang)], axis=1) / math.sqrt(seq))
    return np.stack(tabs).astype(np.float32)


def _class_row_permutation(rows):
    per = rows // DFT_RADIX
    p = np.zeros((rows, rows), np.float32)
    for j in range(DFT_RADIX):
        p[DFT_RADIX * np.arange(per) + j, j * per + np.arange(per)] = 1.0
    return p


def _fourier_body(f_ref, w1_ref, tab_ref, o_ref, yc_ref, ys_ref):
    groups, c, _ = w1_ref.shape
    quarter = f_ref.shape[1] // DFT_RADIX
    for g in range(groups):
        yg = _dot(f_ref[0, :, g * c:(g + 1) * c], w1_ref[g])
        yc_ref[:, g * c:(g + 1) * c] = yg[:, :c]
        ys_ref[:, g * c:(g + 1) * c] = yg[:, c:]
    for j in range(DFT_RADIX):
        pc, ps = [], []
        for q in range(DFT_RADIX):
            rows = slice(q * quarter, (q + 1) * quarter)
            a = (j * q) % 4
            if a == 0:
                pc.append((1, yc_ref, rows)), ps.append((1, ys_ref, rows))
            elif a == 1:
                pc.append((-1, ys_ref, rows)), ps.append((1, yc_ref, rows))
            elif a == 2:
                pc.append((-1, yc_ref, rows)), ps.append((-1, ys_ref, rows))
            else:
                pc.append((1, ys_ref, rows)), ps.append((-1, yc_ref, rows))
        folded = []
        for terms in (pc, ps):
            terms = sorted(terms, key=lambda t: -t[0])
            acc = terms[0][1][terms[0][2], :]
            for sign, ref, rows in terms[1:]:
                acc = acc + ref[rows, :] if sign > 0 else acc - ref[rows, :]
            folded.append(acc.astype(BF16))
        o_ref[0, j] = _dot(tab_ref[j], jnp.concatenate(folded, axis=0)).astype(o_ref.dtype)


def _fourier_mix(f, w1, tab):
    b, s, w = f.shape
    quarter = s // DFT_RADIX
    return pl.pallas_call(
        _fourier_body,
        grid=(b,),
        in_specs=[pl.BlockSpec((1, s, w), lambda bi: (bi, 0, 0)),
                  pl.BlockSpec(w1.shape, lambda bi: (0, 0, 0)),
                  pl.BlockSpec(tab.shape, lambda bi: (0, 0, 0))],
        out_specs=pl.BlockSpec((1, DFT_RADIX, quarter, w), lambda bi: (bi, 0, 0, 0)),
        out_shape=jax.ShapeDtypeStruct((b, DFT_RADIX, quarter, w), BF16),
        scratch_shapes=[pltpu.VMEM((s, w), F32), pltpu.VMEM((s, w), F32)],
        compiler_params=_params(("arbitrary",)),
        name="fourier_mix",
    )(f, w1, tab)


def _na_tile_geometry(rows_total):
    last_r0 = rows_total - NA_TILE_ROWS
    return ((0, 0), (2 * NA_TILE_ROWS, 2 * NA_TILE_ROWS - NA_ROWS // 2), (last_r0, rows_total - NA_WIN_ROWS))


def _na_bias_body(rows_total, rpb_ref, o_ref, t2_ref):
    h = pl.program_id(0)
    wq = lax.broadcasted_iota(jnp.int32, (GRID_W, LANES), 0)
    wk = lax.broadcasted_iota(jnp.int32, (GRID_W, LANES), 1) & (GRID_W - 1)
    cs = jnp.clip(wq - NA_COLS // 2, 0, GRID_W - NA_COLS)
    col_ok = (wk >= cs) & (wk < cs + NA_COLS)
    dc = wk - wq + (NA_COLS - 1)
    n_dr, n_dc = 2 * NA_ROWS - 1, 2 * NA_COLS - 1
    neg = jnp.full((GRID_W, LANES), NEG_INF, F32)
    for dr in range(n_dr):
        val = neg
        for c in range(n_dc):
            val = jnp.where(col_ok & (dc == c), rpb_ref[h * (n_dr * n_dc) + dr * n_dc + c] * LOG2E, val)
        t2_ref[dr] = val
    lane = lax.broadcasted_iota(jnp.int32, (GRID_W, LANES), 1)
    left = lane < GRID_W
    for ty, (r0, ws) in enumerate(_na_tile_geometry(rows_total)):
        for i in range(NA_TILE_ROWS):
            r = r0 + i
            rs = min(max(r - NA_ROWS // 2, 0), rows_total - NA_ROWS)
            for jp in range(NA_WIN_ROWS // 2):
                halves = []
                for kr in (ws + 2 * jp, ws + 2 * jp + 1):
                    halves.append(t2_ref[kr - r + NA_ROWS - 1] if rs <= kr < rs + NA_ROWS else neg)
                o_ref[0, ty, i * GRID_W:(i + 1) * GRID_W, jp * LANES:(jp + 1) * LANES] = jnp.where(left, halves[0], halves[1])


def _na_bias_tables(rpb, rows_total):
    h = rpb.shape[0]
    q_rows, k_cols = NA_TILE_ROWS * GRID_W, NA_WIN_ROWS * GRID_W
    return pl.pallas_call(
        functools.partial(_na_bias_body, rows_total),
        grid=(h,),
        in_specs=[pl.BlockSpec(memory_space=pltpu.SMEM)],
        out_specs=pl.BlockSpec((1, 3, q_rows, k_cols), lambda hi: (hi, 0, 0, 0)),
        out_shape=jax.ShapeDtypeStruct((h, 3, q_rows, k_cols), F32),
        scratch_shapes=[pltpu.VMEM((2 * NA_ROWS - 1, GRID_W, LANES), F32)],
        compiler_params=_params(("arbitrary",)),
        name="na_bias_tables",
    )(rpb.reshape(-1))


def _na_body(rows_total, q_ref, k_ref, v_ref, kc_ref, vc_ref, bias_ref, o_ref):
    t = pl.program_id(2)
    n_t = pl.num_programs(2)
    ty = jnp.where(t == 0, 0, jnp.where(t == n_t - 1, 2, 1))
    ws = jnp.clip(t * NA_TILE_ROWS - NA_ROWS // 2, 0, rows_total - NA_WIN_ROWS)
    off = pl.multiple_of(ws * GRID_W, GRID_W)
    n_win = NA_WIN_ROWS * GRID_W
    bias = jnp.concatenate([bias_ref[0, ty], bias_ref[1, ty]], axis=0)
    for bb in range(q_ref.shape[0]):
        chunks = [(kc_ref[bb], vc_ref[bb], None),
                  (k_ref[bb, pl.ds(off, n_win), :], v_ref[bb, pl.ds(off, n_win), :], bias)]
        o_ref[bb] = _online_softmax_pv(_split_heads(q_ref[bb]), chunks, online=False).astype(o_ref.dtype)


def _neighborhood_attention(q, k, v, kc, vc, bias):
    b, s, w = q.shape
    l = kc.shape[1]
    rows_total = s // GRID_W
    tq = NA_TILE_ROWS * GRID_W
    nb = NA_BATCH_PER_STEP
    full = lambda j, bi, t: (bi, 0, j)
    return pl.pallas_call(
        functools.partial(_na_body, rows_total),
        grid=(w // LANES, b // nb, s // tq),
        in_specs=[pl.BlockSpec((nb, tq, LANES), lambda j, bi, t: (bi, t, j)),
                  pl.BlockSpec((nb, s, LANES), full),
                  pl.BlockSpec((nb, s, 2 * LANES), full),
                  pl.BlockSpec((nb, l, LANES), full),
                  pl.BlockSpec((nb, l, 2 * LANES), full),
                  pl.BlockSpec((2,) + bias.shape[1:], lambda j, bi, t: (j, 0, 0, 0))],
        out_specs=pl.BlockSpec((nb, tq, LANES), lambda j, bi, t: (bi, t, j)),
        out_shape=jax.ShapeDtypeStruct((b, s, w), BF16),
        compiler_params=_params(("arbitrary", "arbitrary", "arbitrary")),
        name="neighborhood_attention",
    )(q, k, v, kc, vc, bias)


def _modulation(m, rows, batch):
    d = m.shape[1] // 3
    pick = lambda lo: jnp.broadcast_to(m[rows, lo:lo + d].reshape(-1, 1, d), (batch, 1, d))
    return pick(0), 1.0 + pick(d), pick(2 * d)


def kernel(x, c, ctx, c_ctx, ab_w_ada, ab_b_ada, ab_norm_g, ab_w_in, ab_conv_w, ab_conv_b, ab_ln_g, ab_ln_b, ab_q_norm_g, ab_k_norm_g, ab_w_out, cd_w_ada, cd_b_ada, cd_norm_g, cd_w_in, cd_w_fourier, cd_rpb, cd_w_out, final_norm_g):
    b, s, d = x.shape
    l = ctx.shape[1]
    assert d == D_MODEL and s % (NA_TILE_ROWS * GRID_W) == 0 and ab_w_ada.shape[0] == 1 and cd_w_ada.shape[0] == 1

    cond = jnp.zeros((16, d), F32).at[:b].set(c).at[b].set(c_ctx)
    lat = slice(0, b)
    cx = slice(b, b + 1)

    m0 = _adaln(cond, ab_w_ada[0], ab_b_ada[0])
    m1 = _adaln(cond, cd_w_ada[0], cd_b_ada[0])
    sh, s1p, gate = _modulation(m0, lat, b)
    csh, cs1p, cgate = _modulation(m0, cx, b)
    sh1, s1p1, gate1 = _modulation(m1, lat, b)
    csh1, cs1p1, _ = _modulation(m1, cx, b)
    ng1 = cd_norm_g[0].reshape(1, d)
    w_in1 = cd_w_in[0].astype(BF16)

    w_in0 = ab_w_in[0].astype(BF16)
    w_out0 = ab_w_out[0].astype(BF16)

    cos, sin = _rope_tables(s)
    cos, sin = jnp.asarray(cos), jnp.asarray(sin)
    mq = jnp.asarray(_block_ones(512)).astype(BF16)
    mk = jnp.asarray(_block_ones(128)).astype(BF16)
    qg = jnp.tile(ab_q_norm_g[0], 512 // HEAD_DIM).reshape(1, 512)
    kg = jnp.tile(ab_k_norm_g[0], 128 // HEAD_DIM).reshape(1, 128)
    ng0 = ab_norm_g[0].reshape(1, d)

    ck, cv, kc, vc = _ctx_layer(ctx, ng0, cs1p, csh, w_in0, qg, kg, mq, mk, ab_conv_w[0], ab_conv_b[0], ab_ln_g[0],
                                ab_ln_b[0], w_out0, cgate, ng1, cs1p1, csh1, w_in1, 1)

    glu, q, k, v, sz = _inproj_ab(x, ng0, s1p, sh, w_in0, qg, kg, cos, sin, mq, mk, 1024)
    y_b = _gqa_attention(q, [ck, k], [cv, v], 1024, 1, chunk=128)
    y_a = _conformer_conv(glu, ab_conv_w[0], ab_conv_b[0], ab_ln_g[0], ab_ln_b[0], 512)

    x1, f, q, k, v, sz = _outproj_inproj_cd(y_a, y_b, sz, w_out0, x, gate, ng1, s1p1, sh1, w_in1, 1024)
    gate = gate1

    c_c, s_c = _dft_tables(cd_w_fourier.shape[-1])
    w1 = _fourier_weights(jnp.asarray(c_c), jnp.asarray(s_c), cd_w_fourier[0])
    y_c = _fourier_mix(f, w1, jnp.asarray(_fourier_tables(s)).astype(BF16))

    bias = _na_bias_tables(cd_rpb[0], s // GRID_W)
    y_d = _neighborhood_attention(q, k, v, kc, vc, bias)

    return _outproj_final(y_c, y_d, sz, cd_w_out[0].astype(BF16), x1, gate, final_norm_g.reshape(1, d), 1024)
```

```python
import functools
import math

import numpy as np
import jax
import jax.numpy as jnp
from jax import lax
from jax.experimental import pallas as pl
from jax.experimental.pallas import tpu as pltpu

F32 = jnp.float32
BF16 = jnp.bfloat16
HIGHEST = lax.Precision.HIGHEST

D_MODEL = 1024
HEAD_DIM = 64
GRID_W = 64
CONV_WIDTH = 31
NA_ROWS = 8
NA_COLS = 16
ROPE_THETA = 10000.0
EPS = 1e-6
NEG_INF = -1e30
LOG2E = math.log2(math.e)
Q_SCALE = HEAD_DIM ** -0.5 * LOG2E
LANES = 128
SUBLANES = 8
HALO = 16
VMEM_LIMIT = 48 * 1024 * 1024

NA_TILE_ROWS = 4
NA_WIN_ROWS = NA_TILE_ROWS + NA_ROWS
NA_BATCH_PER_STEP = 8
IN_PROJ_SUBTILE = 256
MID_SUBTILE = 512
OUT_PROJ_SUBTILE = 512
DFT_RADIX = 4


def _params(sem, flags=None):
    return pltpu.CompilerParams(dimension_semantics=sem, vmem_limit_bytes=VMEM_LIMIT, flags=flags)


def _silu(t):
    return t * jax.nn.sigmoid(t)


def _dot(a, b):
    return jnp.dot(a, b, preferred_element_type=F32)


def _dot_nt(a, b):
    return lax.dot_general(a, b, (((1,), (1,)), ((), ())), preferred_element_type=F32)


def _rope_tables(seq):
    t = np.arange(seq)
    row = (t // GRID_W).astype(np.float64)
    col = (t % GRID_W).astype(np.float64)
    nf = HEAD_DIM // 4
    inv = np.float32(ROPE_THETA) ** (-np.arange(nf, dtype=np.float32) / nf)
    ang = np.concatenate([row[:, None] * inv, col[:, None] * inv], axis=-1)
    cos = np.repeat(np.cos(ang), 2, axis=-1)
    sin = np.repeat(np.sin(ang), 2, axis=-1)
    sign = np.tile(np.array([-1.0, 1.0]), HEAD_DIM // 2)
    cos = np.tile(cos, (1, LANES // HEAD_DIM))
    sin = np.tile(sin * sign, (1, LANES // HEAD_DIM))
    return cos.astype(np.float32), sin.astype(np.float32)


def _dft_tables(n):
    k = np.arange(n)
    ang = 2.0 * np.pi * ((k[:, None] * k[None, :]) % n) / n
    s = 1.0 / math.sqrt(n)
    return (np.cos(ang) * s).astype(np.float32), (np.sin(ang) * s).astype(np.float32)


def _block_ones(width):
    h = np.arange(width) // HEAD_DIM
    return (h[:, None] == h[None, :]).astype(np.float32)


def _adaln_body(c_ref, w_ref, b_ref, o_ref):
    a = _silu(c_ref[...]).astype(BF16)
    o_ref[...] = _dot(a, w_ref[...].astype(BF16)) + b_ref[...]


def _adaln(cond, w, b):
    r, d = cond.shape
    n = w.shape[1]
    tn = 1024
    return pl.pallas_call(
        _adaln_body,
        grid=(n // tn,),
        in_specs=[pl.BlockSpec((r, d), lambda j: (0, 0)),
                  pl.BlockSpec((d, tn), lambda j: (0, j)),
                  pl.BlockSpec((1, tn), lambda j: (0, j))],
        out_specs=pl.BlockSpec((r, tn), lambda j: (0, j)),
        out_shape=jax.ShapeDtypeStruct((r, n), F32),
        compiler_params=_params(("arbitrary",)),
        name="adaln",
    )(cond, w, b.reshape(1, n))


def _head_rms(t, ones_ref, gain):
    ms = _dot((t * t).astype(BF16), ones_ref[...]) * (1.0 / HEAD_DIM)
    return t * lax.rsqrt(ms + EPS) * gain


def _rope(t, cos, sin_signed):
    rows = t.shape[0]
    lane = lax.broadcasted_iota(jnp.int32, (rows, LANES), 1)
    even = (lane & 1) == 0
    outs = []
    for c in range(t.shape[1] // LANES):
        xc = t[:, c * LANES:(c + 1) * LANES]
        swapped = jnp.where(even, pltpu.roll(xc, LANES - 1, 1), pltpu.roll(xc, 1, 1))
        outs.append(xc * cos + swapped * sin_signed)
    return outs[0] if len(outs) == 1 else jnp.concatenate(outs, axis=1)


def _modulate(x, g_ref, s1p_ref, sh_ref):
    y = x * lax.rsqrt(jnp.mean(x * x, axis=-1, keepdims=True) + EPS) * g_ref[...]
    return (y * s1p_ref[0] + sh_ref[0]).astype(BF16)


def _row_subtiles(tm, sub):
    sub = min(tm, sub)
    return [slice(r, r + sub) for r in range(0, tm, sub)]


def _inproj_ab_body(x_ref, g_ref, s1p_ref, sh_ref, w_ref, qg_ref, kg_ref, cos_ref, sin_ref, mq_ref, mk_ref,
                    glu_ref, q_ref, k_ref, v_ref, sz_ref):
    for rows in _row_subtiles(x_ref.shape[1], IN_PROJ_SUBTILE):
        u = _dot(_modulate(x_ref[0, rows, :], g_ref, s1p_ref, sh_ref), w_ref[...])
        glu, q, k, v, sz = _ab_epilogue(u, mq_ref, mk_ref, qg_ref, kg_ref, (cos_ref[rows, :], sin_ref[rows, :]))
        glu_ref[0, rows, :] = glu
        q_ref[0, rows, :] = q
        k_ref[0, rows, :] = k
        v_ref[0, rows, :] = v
        sz_ref[0, rows, :] = sz


def _ab_epilogue(u, mq_ref, mk_ref, qg_ref, kg_ref, rope_tabs):
    glu = u[:, 0:512] * jax.nn.sigmoid(u[:, 512:1024])
    q = _head_rms(u[:, 1024:1536], mq_ref, qg_ref[...])
    k = _head_rms(u[:, 1536:1664], mk_ref, kg_ref[...])
    if rope_tabs is not None:
        q = _rope(q, *rope_tabs)
        k = _rope(k, *rope_tabs)
    v = u[:, 1664:1792]
    k_sw, v_sw = pltpu.roll(k, HEAD_DIM, 1), pltpu.roll(v, HEAD_DIM, 1)
    left = lax.broadcasted_iota(jnp.int32, k.shape, 1) < HEAD_DIM
    one = jnp.ones_like(v)
    k_dup = jnp.concatenate([jnp.where(left, k, k_sw), jnp.where(left, k_sw, k)], axis=1)
    v_dup = jnp.concatenate([jnp.where(left, v, one), jnp.where(left, one, v_sw),
                             jnp.where(left, v_sw, one), jnp.where(left, one, v)], axis=1)
    return (glu, (q * Q_SCALE).astype(BF16), k_dup.astype(BF16), v_dup.astype(BF16),
            _silu(u[:, 1792:2816]).astype(BF16))


def _inproj_ab(x, g, s1p, sh, w, qg, kg, cos, sin, mq, mk, tm):
    b, s, d = x.shape
    n = w.shape[1]
    row = lambda bi, i: (bi, i, 0)
    const = lambda bi, i: (0, 0)
    per_b = lambda bi, i: (bi, 0, 0)
    tab = lambda bi, i: (i, 0)
    return pl.pallas_call(
        _inproj_ab_body,
        grid=(b, s // tm),
        in_specs=[pl.BlockSpec((1, tm, d), row),
                  pl.BlockSpec((1, d), const),
                  pl.BlockSpec((1, 1, d), per_b),
                  pl.BlockSpec((1, 1, d), per_b),
                  pl.BlockSpec((d, n), const),
                  pl.BlockSpec((1, 512), const),
                  pl.BlockSpec((1, 128), const),
                  pl.BlockSpec((tm, LANES), tab),
                  pl.BlockSpec((tm, LANES), tab),
                  pl.BlockSpec((512, 512), const),
                  pl.BlockSpec((128, 128), const)],
        out_specs=[pl.BlockSpec((1, tm, 512), row),
                   pl.BlockSpec((1, tm, 512), row),
                   pl.BlockSpec((1, tm, 256), row),
                   pl.BlockSpec((1, tm, 512), row),
                   pl.BlockSpec((1, tm, 1024), row)],
        out_shape=[jax.ShapeDtypeStruct((b, s, 512), F32),
                   jax.ShapeDtypeStruct((b, s, 512), BF16),
                   jax.ShapeDtypeStruct((b, s, 256), BF16),
                   jax.ShapeDtypeStruct((b, s, 512), BF16),
                   jax.ShapeDtypeStruct((b, s, 1024), BF16)],
        compiler_params=_params(("arbitrary", "arbitrary")),
        name="inproj_ab",
    )(x, g, s1p, sh, w, qg, kg, cos, sin, mq, mk)


def _outproj_inproj_cd_body(ya_ref, yb_ref, sz0_ref, wo_ref, x_ref, gate_ref, g_ref, s1p_ref, sh_ref, wi_ref,
                            x1_ref, f_ref, q_ref, k_ref, v_ref, sz_ref):
    for rows in _row_subtiles(x_ref.shape[1], MID_SUBTILE):
        y = (jnp.concatenate([ya_ref[0, rows, :], yb_ref[0, rows, :]], axis=-1).astype(F32)
             * sz0_ref[0, rows, :].astype(F32))
        xn = x_ref[0, rows, :] + gate_ref[0] * _dot(y.astype(BF16), wo_ref[...])
        x1_ref[0, rows, :] = xn
        u = _dot(_modulate(xn, g_ref, s1p_ref, sh_ref), wi_ref[...])
        f_ref[0, rows, :] = u[:, 0:512].astype(BF16)
        q_ref[0, rows, :] = (u[:, 512:1024] * Q_SCALE).astype(BF16)
        k_ref[0, rows, :] = u[:, 1024:1536].astype(BF16)
        v_ref[0, rows, :] = _pairs_with_ones(u[:, 1536:2048]).astype(BF16)
        sz_ref[0, rows, :] = _silu(u[:, 2048:3072]).astype(BF16)


def _outproj_inproj_cd(ya, yb, sz0, w_out, x, gate, g, s1p, sh, w_in, tm):
    b, s, d = x.shape
    n = w_in.shape[1]
    row = lambda bi, i: (bi, i, 0)
    const = lambda bi, i: (0, 0)
    per_b = lambda bi, i: (bi, 0, 0)
    return pl.pallas_call(
        _outproj_inproj_cd_body,
        grid=(b, s // tm),
        in_specs=[pl.BlockSpec((1, tm, 512), row),
                  pl.BlockSpec((1, tm, 512), row),
                  pl.BlockSpec((1, tm, 1024), row),
                  pl.BlockSpec((1024, d), const, pipeline_mode=pl.Buffered(1)),
                  pl.BlockSpec((1, tm, d), row),
                  pl.BlockSpec((1, 1, d), per_b),
                  pl.BlockSpec((1, d), const),
                  pl.BlockSpec((1, 1, d), per_b),
                  pl.BlockSpec((1, 1, d), per_b),
                  pl.BlockSpec((d, n), const, pipeline_mode=pl.Buffered(1))],
        out_specs=[pl.BlockSpec((1, tm, d), row)] + [pl.BlockSpec((1, tm, 512), row)] * 3
        + [pl.BlockSpec((1, tm, 1024), row)] * 2,
        out_shape=[jax.ShapeDtypeStruct((b, s, d), F32)] + [jax.ShapeDtypeStruct((b, s, 512), BF16)] * 3
        + [jax.ShapeDtypeStruct((b, s, 1024), BF16)] * 2,
        compiler_params=_params(("arbitrary", "arbitrary")),
        name="outproj_inproj_cd",
    )(ya, yb, sz0, w_out, x, gate, g, s1p, sh, w_in)


def _split_heads(q):
    lane = lax.broadcasted_iota(jnp.int32, q.shape, 1)
    zero = jnp.zeros_like(q)
    return jnp.concatenate([jnp.where(lane < HEAD_DIM, q, zero), jnp.where(lane >= HEAD_DIM, q, zero)], axis=0)


def _with_ones(v):
    lane = lax.broadcasted_iota(jnp.int32, v.shape, 1)
    one = jnp.ones_like(v)
    return jnp.concatenate([jnp.where(lane < HEAD_DIM, v, one), jnp.where(lane < HEAD_DIM, one, v)], axis=1)


def _pairs_with_ones(v):
    return jnp.concatenate([_with_ones(v[:, c:c + LANES]) for c in range(0, v.shape[1], LANES)], axis=1)


def _pv(p, v2, rows):
    pv = _dot(p.astype(BF16), v2)
    return jnp.concatenate([pv[:rows, :LANES], pv[rows:, LANES:]], axis=0)


def _online_softmax_pv(q2, chunks, online=True):
    rows = q2.shape[0] // 2
    scores = [_dot_nt(q2, k) if bias is None else _dot_nt(q2, k) + bias for k, _, bias in chunks]
    if online:
        m = acc = None
        for s, (_, v2, _) in zip(scores, chunks):
            mc = jnp.max(s, axis=-1, keepdims=True)
            m_new = mc if m is None else jnp.maximum(m, mc)
            pv = _pv(jnp.exp2(s - m_new), v2, rows)
            acc = pv if m is None else jnp.exp2(m - m_new) * acc + pv
            m = m_new
    else:
        m = functools.reduce(jnp.maximum, [jnp.max(s, axis=-1, keepdims=True) for s in scores])
        acc = functools.reduce(jnp.add, [_pv(jnp.exp2(s - m), v2, rows) for s, (_, v2, _) in zip(scores, chunks)])
    left = lax.broadcasted_iota(jnp.int32, (rows, LANES), 1) < HEAD_DIM
    top, bot = acc[:rows], acc[rows:]
    return jnp.where(left, top, bot) / pltpu.roll(jnp.where(left, bot, top), HEAD_DIM, 1)


V_ROWS = 80

def _softmax_pv_keys_major(q2, chunks):
    half = q2.shape[0] // 2
    scores = [_dot_nt(k, q2) for k, _ in chunks]
    m = acc = None
    for s, (_, v_ones) in zip(scores, chunks):
        mc = jnp.max(s, axis=0, keepdims=True)
        m_new = mc if m is None else jnp.maximum(m, mc)
        pv = _dot(v_ones.T[:V_ROWS], jnp.exp2(s - m_new).astype(BF16))
        acc = pv if m is None else jnp.exp2(m - m_new) * acc + pv
        m = m_new
    o = acc[:HEAD_DIM] / acc[HEAD_DIM:HEAD_DIM + 1]
    return jnp.concatenate([o[:, :half], o[:, half:]], axis=0).T


def _key_chunks(k_ref, v_ref, kv, chunk):
    n = k_ref.shape[1]
    kl, vl = slice(kv * LANES, (kv + 1) * LANES), slice(kv * 2 * LANES, (kv + 1) * 2 * LANES)
    return [(k_ref[0, c:c + chunk, kl], v_ref[0, c:c + chunk, vl], None) for c in range(0, n, chunk)]


def _gqa_body(n_src, chunk, q_ref, *refs):
    k_refs, v_refs, o_ref = refs[:n_src], refs[n_src:2 * n_src], refs[2 * n_src]
    pairs = q_ref.shape[2] // LANES
    kv_heads = k_refs[0].shape[2] // LANES
    for p in range(pairs):
        lanes = slice(p * LANES, (p + 1) * LANES)
        kv = p * kv_heads // pairs
        chunks = [c for k, v in zip(k_refs, v_refs) for c in _key_chunks(k, v, kv, min(chunk, k.shape[1]))]
        chunks = [(k, v2[:, :LANES]) for k, v2, _ in chunks]
        o_ref[0, :, lanes] = _softmax_pv_keys_major(_split_heads(q_ref[0, :, lanes]), chunks).astype(o_ref.dtype)


def _gqa_attention(q, ks, vs, tq, pairs, chunk=512):
    b, lq, w = q.shape
    n_src = len(ks)
    n_pairs = w // LANES
    kv_heads = ks[0].shape[2] // LANES
    kv_blk = max(1, pairs * kv_heads // n_pairs)
    kv_idx = lambda bi, i, j: (bi, 0, j * pairs * kv_heads // n_pairs // kv_blk)
    k_specs = [pl.BlockSpec((1, k.shape[1], kv_blk * LANES), kv_idx) for k in ks]
    v_specs = [pl.BlockSpec((1, v.shape[1], kv_blk * 2 * LANES), kv_idx) for v in vs]
    return pl.pallas_call(
        functools.partial(_gqa_body, n_src, chunk),
        grid=(b, lq // tq, n_pairs // pairs),
        in_specs=[pl.BlockSpec((1, tq, pairs * LANES), lambda bi, i, j: (bi, i, j))] + k_specs + v_specs,
        out_specs=pl.BlockSpec((1, tq, pairs * LANES), lambda bi, i, j: (bi, i, j)),
        out_shape=jax.ShapeDtypeStruct((b, lq, w), BF16),
        compiler_params=_params(("arbitrary", "arbitrary", "arbitrary")),
        name=f"gqa_attention_{n_src}src",
    )(q, *ks, *vs)


def _conv_core(rows, pad_ref, shift_ref, w_ref, cb_ref, lg_ref, lb_ref, store):
    span = shift_ref.shape[1]
    for r in range(1, SUBLANES):
        shift_ref[r] = pad_ref[r:r + span, :]
    sub = 32
    base = HALO - CONV_WIDTH // 2
    for r0 in range(0, rows, sub):
        acc = jnp.zeros((sub, pad_ref.shape[1]), F32) + cb_ref[...]
        for k in range(CONV_WIDTH):
            a, r = divmod(base + k, SUBLANES)
            lo = a * SUBLANES + r0
            tap = pad_ref[lo:lo + sub, :] if r == 0 else shift_ref[r, lo:lo + sub, :]
            acc = acc + tap * jnp.concatenate([w_ref[k]] * (sub // SUBLANES), axis=0)
        mu = jnp.mean(acc, axis=-1, keepdims=True)
        xc = acc - mu
        y = xc * lax.rsqrt(jnp.mean(xc * xc, axis=-1, keepdims=True) + EPS) * lg_ref[...] + lb_ref[...]
        store(r0, sub, _silu(y))


def _conv_body(rows, x_ref, prev_ref, next_ref, w_ref, cb_ref, lg_ref, lb_ref, o_ref, pad_ref, shift_ref):
    i = pl.program_id(1)
    last = pl.num_programs(1) - 1
    pad_ref[0:HALO, :] = jnp.where(i > 0, prev_ref[0], 0.0)
    pad_ref[HALO:HALO + rows, :] = x_ref[0]
    pad_ref[HALO + rows:HALO + rows + HALO, :] = jnp.where(i < last, next_ref[0], 0.0)

    def store(r0, sub, y):
        o_ref[0, r0:r0 + sub, :] = y.astype(o_ref.dtype)

    _conv_core(rows, pad_ref, shift_ref, w_ref, cb_ref, lg_ref, lb_ref, store)


def _conformer_conv(u, conv_w, conv_b, ln_g, ln_b, rows):
    b, l, c = u.shape
    hb = rows // HALO
    n_halo = l // HALO
    return pl.pallas_call(
        functools.partial(_conv_body, rows),
        grid=(b, l // rows),
        in_specs=[pl.BlockSpec((1, rows, c), lambda bi, i: (bi, i, 0)),
                  pl.BlockSpec((1, HALO, c), lambda bi, i: (bi, jnp.maximum(i * hb - 1, 0), 0)),
                  pl.BlockSpec((1, HALO, c), lambda bi, i: (bi, jnp.minimum((i + 1) * hb, n_halo - 1), 0)),
                  pl.BlockSpec((CONV_WIDTH, SUBLANES, c), lambda bi, i: (0, 0, 0)),
                  pl.BlockSpec((1, c), lambda bi, i: (0, 0)),
                  pl.BlockSpec((1, c), lambda bi, i: (0, 0)),
                  pl.BlockSpec((1, c), lambda bi, i: (0, 0))],
        out_specs=pl.BlockSpec((1, rows, c), lambda bi, i: (bi, i, 0)),
        out_shape=jax.ShapeDtypeStruct((b, l, c), BF16),
        scratch_shapes=[pltpu.VMEM((rows + 2 * HALO, c), F32),
                        pltpu.VMEM((SUBLANES, rows + 2 * HALO - SUBLANES, c), F32)],
        compiler_params=_params(("arbitrary", "arbitrary")),
        name="conformer_conv",
    )(u, u, u, jnp.broadcast_to(conv_w[:, None, :], (CONV_WIDTH, SUBLANES, c)),
      conv_b.reshape(1, c), ln_g.reshape(1, c), ln_b.reshape(1, c))


def _ctx_layer_body(x_ref, g0_ref, s1p0_ref, sh0_ref, wi0_ref, qg_ref, kg_ref, mq_ref, mk_ref, cw_ref, cb_ref,
                    lg_ref, lb_ref, wo0_ref, gate0_ref, g1_ref, s1p1_ref, sh1_ref, wkv1_ref,
                    ck_ref, cv_ref, kc_ref, vc_ref, pad_ref, shift_ref, ya_ref):
    rows = x_ref.shape[1]
    x = x_ref[0]
    u = _dot(_modulate(x, g0_ref, s1p0_ref, sh0_ref), wi0_ref[...])
    glu, q, k_dup, v_dup, sz = _ab_epilogue(u, mq_ref, mk_ref, qg_ref, kg_ref, None)
    ck_ref[0] = k_dup
    cv_ref[0] = v_dup
    pairs, kv_heads = q.shape[1] // LANES, k_dup.shape[1] // LANES
    yb = []
    for p in range(pairs):
        kv = p * kv_heads // pairs
        keys = [(k_dup[:, kv * LANES:(kv + 1) * LANES], v_dup[:, kv * 2 * LANES:(kv + 1) * 2 * LANES], None)]
        yb.append(_online_softmax_pv(_split_heads(q[:, p * LANES:(p + 1) * LANES]), keys, online=False))
    zeros = jnp.zeros((HALO, glu.shape[1]), F32)
    pad_ref[0:HALO, :] = zeros
    pad_ref[HALO:HALO + rows, :] = glu
    pad_ref[HALO + rows:HALO + rows + HALO, :] = zeros

    def store(r0, sub, y):
        ya_ref[r0:r0 + sub, :] = y

    _conv_core(rows, pad_ref, shift_ref, cw_ref, cb_ref, lg_ref, lb_ref, store)
    y = jnp.concatenate([ya_ref[...]] + yb, axis=-1) * sz.astype(F32)
    x1 = x + gate0_ref[0] * _dot(y.astype(BF16), wo0_ref[...])
    u1 = _dot(_modulate(x1, g1_ref, s1p1_ref, sh1_ref), wkv1_ref[...])
    kc_ref[0] = u1[:, 0:512].astype(BF16)
    vc_ref[0] = _pairs_with_ones(u1[:, 512:1024]).astype(BF16)


def _ctx_layer(ctx, g0, s1p0, sh0, w_in0, qg, kg, mq, mk, conv_w, conv_b, ln_g, ln_b, w_out0, gate0,
               g1, s1p1, sh1, w_in1, kv_col_block):
    b, l, d = ctx.shape
    c = conv_w.shape[1]
    row = lambda bi: (bi, 0, 0)
    const2 = lambda bi: (0, 0)
    vec = pl.BlockSpec((1, d), const2)
    per_b = pl.BlockSpec((1, 1, d), row)
    cvec = pl.BlockSpec((1, c), const2)
    return pl.pallas_call(
        _ctx_layer_body,
        grid=(b,),
        in_specs=[pl.BlockSpec((1, l, d), row), vec, per_b, per_b,
                  pl.BlockSpec(w_in0.shape, const2),
                  pl.BlockSpec((1, 512), const2), pl.BlockSpec((1, 128), const2),
                  pl.BlockSpec((512, 512), const2), pl.BlockSpec((128, 128), const2),
                  pl.BlockSpec((CONV_WIDTH, SUBLANES, c), lambda bi: (0, 0, 0)), cvec, cvec, cvec,
                  pl.BlockSpec(w_out0.shape, const2), per_b,
                  vec, per_b, per_b,
                  pl.BlockSpec((d, 1024), lambda bi: (0, kv_col_block))],
        out_specs=[pl.BlockSpec((1, l, 256), row), pl.BlockSpec((1, l, 512), row),
                   pl.BlockSpec((1, l, 512), row), pl.BlockSpec((1, l, 1024), row)],
        out_shape=[jax.ShapeDtypeStruct((b, l, 256), BF16), jax.ShapeDtypeStruct((b, l, 512), BF16),
                   jax.ShapeDtypeStruct((b, l, 512), BF16), jax.ShapeDtypeStruct((b, l, 1024), BF16)],
        scratch_shapes=[pltpu.VMEM((l + 2 * HALO, c), F32),
                        pltpu.VMEM((SUBLANES, l + 2 * HALO - SUBLANES, c), F32),
                        pltpu.VMEM((l, c), F32)],
        compiler_params=_params(("arbitrary",)),
        name="ctx_layer",
    )(ctx, g0, s1p0, sh0, w_in0, qg, kg, mq, mk,
      jnp.broadcast_to(conv_w[:, None, :], (CONV_WIDTH, SUBLANES, c)), conv_b.reshape(1, c), ln_g.reshape(1, c),
      ln_b.reshape(1, c), w_out0, gate0, g1, s1p1, sh1, w_in1)


def _outproj_final_body(ya_ref, yb_ref, sz_ref, w_ref, x_ref, gate_ref, perm_ref, fg_ref, o_ref):
    sub = perm_ref.shape[0]
    per = sub // DFT_RADIX
    for h in range(x_ref.shape[1] // sub):
        rows = slice(h * sub, (h + 1) * sub)
        stacked = jnp.concatenate([ya_ref[0, j, h * per:(h + 1) * per, :] for j in range(DFT_RADIX)], axis=0)
        ya = _dot(perm_ref[...], stacked)
        y = jnp.concatenate([ya, yb_ref[0, rows, :].astype(F32)], axis=-1) * sz_ref[0, rows, :].astype(F32)
        xn = x_ref[0, rows, :] + gate_ref[0] * _dot(y.astype(BF16), w_ref[...])
        o_ref[0, rows, :] = xn * lax.rsqrt(jnp.mean(xn * xn, axis=-1, keepdims=True) + EPS) * fg_ref[...]


def _outproj_final(ya, yb, sz, w, x, gate, final_g, tm):
    b, s, d = x.shape
    row = lambda bi, i: (bi, i, 0)
    const = lambda bi, i: (0, 0)
    return pl.pallas_call(
        _outproj_final_body,
        grid=(b, s // tm),
        in_specs=[pl.BlockSpec((1, DFT_RADIX, tm // DFT_RADIX, 512), lambda bi, i: (bi, 0, i, 0)),
                  pl.BlockSpec((1, tm, 512), row),
                  pl.BlockSpec((1, tm, 1024), row),
                  pl.BlockSpec((1024, d), const),
                  pl.BlockSpec((1, tm, d), row),
                  pl.BlockSpec((1, 1, d), lambda bi, i: (bi, 0, 0)),
                  pl.BlockSpec((OUT_PROJ_SUBTILE, OUT_PROJ_SUBTILE), const),
                  pl.BlockSpec((1, d), const)],
        out_specs=pl.BlockSpec((1, tm, d), row),
        out_shape=jax.ShapeDtypeStruct((b, s, d), F32),
        compiler_params=_params(("arbitrary", "arbitrary")),
        name="outproj_final",
    )(ya, yb, sz, w, x, gate, jnp.asarray(_class_row_permutation(OUT_PROJ_SUBTILE)).astype(BF16), final_g)


def _fourier_w_body(cc_ref, sc_ref, w_ref, o_ref):
    for g in range(w_ref.shape[0]):
        wc = jnp.dot(cc_ref[...], w_ref[g], preferred_element_type=F32, precision=HIGHEST)
        ws = jnp.dot(sc_ref[...], w_ref[g], preferred_element_type=F32, precision=HIGHEST)
        o_ref[g] = jnp.concatenate([wc, ws], axis=-1).astype(o_ref.dtype)


def _fourier_weights(cc, sc, w):
    g, c, _ = w.shape
    return pl.pallas_call(
        _fourier_w_body,
        out_shape=jax.ShapeDtypeStruct((g, c, 2 * c), BF16),
        name="fourier_weights",
    )(cc, sc, w)


def _fourier_tables(seq):
    quarter = seq // DFT_RADIX
    m = np.arange(quarter)
    tabs = []
    for j in range(DFT_RADIX):
        ang = 2.0 * np.pi * (((DFT_RADIX * m + j)[:, None] * m[None, :]) % seq) / seq
        tabs.append(np.concatenate([np.cos(ang), -np.sin(ang)], axis=1) / math.sqrt(seq))
    return np.stack(tabs).astype(np.float32)


def _class_row_permutation(rows):
    per = rows // DFT_RADIX
    p = np.zeros((rows, rows), np.float32)
    for j in range(DFT_RADIX):
        p[DFT_RADIX * np.arange(per) + j, j * per + np.arange(per)] = 1.0
    return p


def _fourier_body(f_ref, w1_ref, tab_ref, o_ref, yc_ref, ys_ref):
    groups, c, _ = w1_ref.shape
    quarter = f_ref.shape[1] // DFT_RADIX
    for g in range(groups):
        yg = _dot(f_ref[0, :, g * c:(g + 1) * c], w1_ref[g])
        yc_ref[:, g * c:(g + 1) * c] = yg[:, :c]
        ys_ref[:, g * c:(g + 1) * c] = yg[:, c:]
    for j in range(DFT_RADIX):
        pc, ps = [], []
        for q in range(DFT_RADIX):
            rows = slice(q * quarter, (q + 1) * quarter)
            a = (j * q) % 4
            if a == 0:
                pc.append((1, yc_ref, rows)), ps.append((1, ys_ref, rows))
            elif a == 1:
                pc.append((-1, ys_ref, rows)), ps.append((1, yc_ref, rows))
            elif a == 2:
                pc.append((-1, yc_ref, rows)), ps.append((-1, ys_ref, rows))
            else:
                pc.append((1, ys_ref, rows)), ps.append((-1, yc_ref, rows))
        folded = []
        for terms in (pc, ps):
            terms = sorted(terms, key=lambda t: -t[0])
            acc = terms[0][1][terms[0][2], :]
            for sign, ref, rows in terms[1:]:
                acc = acc + ref[rows, :] if sign > 0 else acc - ref[rows, :]
            folded.append(acc.astype(BF16))
        o_ref[0, j] = _dot(tab_ref[j], jnp.concatenate(folded, axis=0)).astype(o_ref.dtype)


def _fourier_mix(f, w1, tab):
    b, s, w = f.shape
    quarter = s // DFT_RADIX
    return pl.pallas_call(
        _fourier_body,
        grid=(b,),
        in_specs=[pl.BlockSpec((1, s, w), lambda bi: (bi, 0, 0)),
                  pl.BlockSpec(w1.shape, lambda bi: (0, 0, 0)),
                  pl.BlockSpec(tab.shape, lambda bi: (0, 0, 0))],
        out_specs=pl.BlockSpec((1, DFT_RADIX, quarter, w), lambda bi: (bi, 0, 0, 0)),
        out_shape=jax.ShapeDtypeStruct((b, DFT_RADIX, quarter, w), BF16),
        scratch_shapes=[pltpu.VMEM((s, w), F32), pltpu.VMEM((s, w), F32)],
        compiler_params=_params(("arbitrary",)),
        name="fourier_mix",
    )(f, w1, tab)


def _na_tile_geometry(rows_total):
    last_r0 = rows_total - NA_TILE_ROWS
    return ((0, 0), (2 * NA_TILE_ROWS, 2 * NA_TILE_ROWS - NA_ROWS // 2), (last_r0, rows_total - NA_WIN_ROWS))


def _na_bias_body(rows_total, rpb_ref, o_ref, t2_ref):
    h = pl.program_id(0)
    wq = lax.broadcasted_iota(jnp.int32, (GRID_W, LANES), 0)
    wk = lax.broadcasted_iota(jnp.int32, (GRID_W, LANES), 1) & (GRID_W - 1)
    cs = jnp.clip(wq - NA_COLS // 2, 0, GRID_W - NA_COLS)
    col_ok = (wk >= cs) & (wk < cs + NA_COLS)
    dc = wk - wq + (NA_COLS - 1)
    n_dr, n_dc = 2 * NA_ROWS - 1, 2 * NA_COLS - 1
    neg = jnp.full((GRID_W, LANES), NEG_INF, F32)
    for dr in range(n_dr):
        val = neg
        for c in range(n_dc):
            val = jnp.where(col_ok & (dc == c), rpb_ref[h * (n_dr * n_dc) + dr * n_dc + c] * LOG2E, val)
        t2_ref[dr] = val
    lane = lax.broadcasted_iota(jnp.int32, (GRID_W, LANES), 1)
    left = lane < GRID_W
    for ty, (r0, ws) in enumerate(_na_tile_geometry(rows_total)):
        for i in range(NA_TILE_ROWS):
            r = r0 + i
            rs = min(max(r - NA_ROWS // 2, 0), rows_total - NA_ROWS)
            for jp in range(NA_WIN_ROWS // 2):
                halves = []
                for kr in (ws + 2 * jp, ws + 2 * jp + 1):
                    halves.append(t2_ref[kr - r + NA_ROWS - 1] if rs <= kr < rs + NA_ROWS else neg)
                o_ref[0, ty, i * GRID_W:(i + 1) * GRID_W, jp * LANES:(jp + 1) * LANES] = jnp.where(left, halves[0], halves[1])


def _na_bias_tables(rpb, rows_total):
    h = rpb.shape[0]
    q_rows, k_cols = NA_TILE_ROWS * GRID_W, NA_WIN_ROWS * GRID_W
    return pl.pallas_call(
        functools.partial(_na_bias_body, rows_total),
        grid=(h,),
        in_specs=[pl.BlockSpec(memory_space=pltpu.SMEM)],
        out_specs=pl.BlockSpec((1, 3, q_rows, k_cols), lambda hi: (hi, 0, 0, 0)),
        out_shape=jax.ShapeDtypeStruct((h, 3, q_rows, k_cols), F32),
        scratch_shapes=[pltpu.VMEM((2 * NA_ROWS - 1, GRID_W, LANES), F32)],
        compiler_params=_params(("arbitrary",)),
        name="na_bias_tables",
    )(rpb.reshape(-1))


def _na_body(rows_total, q_ref, k_ref, v_ref, kc_ref, vc_ref, bias_ref, o_ref):
    t = pl.program_id(2)
    n_t = pl.num_programs(2)
    ty = jnp.where(t == 0, 0, jnp.where(t == n_t - 1, 2, 1))
    ws = jnp.clip(t * NA_TILE_ROWS - NA_ROWS // 2, 0, rows_total - NA_WIN_ROWS)
    off = pl.multiple_of(ws * GRID_W, GRID_W)
    n_win = NA_WIN_ROWS * GRID_W
    bias = jnp.concatenate([bias_ref[0, ty], bias_ref[1, ty]], axis=0)
    for bb in range(q_ref.shape[0]):
        chunks = [(kc_ref[bb], vc_ref[bb], None),
                  (k_ref[bb, pl.ds(off, n_win), :], v_ref[bb, pl.ds(off, n_win), :], bias)]
        o_ref[bb] = _online_softmax_pv(_split_heads(q_ref[bb]), chunks, online=False).astype(o_ref.dtype)


def _neighborhood_attention(q, k, v, kc, vc, bias):
    b, s, w = q.shape
    l = kc.shape[1]
    rows_total = s // GRID_W
    tq = NA_TILE_ROWS * GRID_W
    nb = NA_BATCH_PER_STEP
    full = lambda j, bi, t: (bi, 0, j)
    return pl.pallas_call(
        functools.partial(_na_body, rows_total),
        grid=(w // LANES, b // nb, s // tq),
        in_specs=[pl.BlockSpec((nb, tq, LANES), lambda j, bi, t: (bi, t, j)),
                  pl.BlockSpec((nb, s, LANES), full),
                  pl.BlockSpec((nb, s, 2 * LANES), full),
                  pl.BlockSpec((nb, l, LANES), full),
                  pl.BlockSpec((nb, l, 2 * LANES), full),
                  pl.BlockSpec((2,) + bias.shape[1:], lambda j, bi, t: (j, 0, 0, 0))],
        out_specs=pl.BlockSpec((nb, tq, LANES), lambda j, bi, t: (bi, t, j)),
        out_shape=jax.ShapeDtypeStruct((b, s, w), BF16),
        compiler_params=_params(("arbitrary", "arbitrary", "arbitrary")),
        name="neighborhood_attention",
    )(q, k, v, kc, vc, bias)


def _modulation(m, rows, batch):
    d = m.shape[1] // 3
    pick = lambda lo: jnp.broadcast_to(m[rows, lo:lo + d].reshape(-1, 1, d), (batch, 1, d))
    return pick(0), 1.0 + pick(d), pick(2 * d)


def kernel(x, c, ctx, c_ctx, ab_w_ada, ab_b_ada, ab_norm_g, ab_w_in, ab_conv_w, ab_conv_b, ab_ln_g, ab_ln_b, ab_q_norm_g, ab_k_norm_g, ab_w_out, cd_w_ada, cd_b_ada, cd_norm_g, cd_w_in, cd_w_fourier, cd_rpb, cd_w_out, final_norm_g):
    b, s, d = x.shape
    l = ctx.shape[1]
    assert d == D_MODEL and s % (NA_TILE_ROWS * GRID_W) == 0 and ab_w_ada.shape[0] == 1 and cd_w_ada.shape[0] == 1

    cond = jnp.zeros((16, d), F32).at[:b].set(c).at[b].set(c_ctx)
    lat = slice(0, b)
    cx = slice(b, b + 1)

    m0 = _adaln(cond, ab_w_ada[0], ab_b_ada[0])
    m1 = _adaln(cond, cd_w_ada[0], cd_b_ada[0])
    sh, s1p, gate = _modulation(m0, lat, b)
    csh, cs1p, cgate = _modulation(m0, cx, b)
    sh1, s1p1, gate1 = _modulation(m1, lat, b)
    csh1, cs1p1, _ = _modulation(m1, cx, b)
    ng1 = cd_norm_g[0].reshape(1, d)
    w_in1 = cd_w_in[0].astype(BF16)

    w_in0 = ab_w_in[0].astype(BF16)
    w_out0 = ab_w_out[0].astype(BF16)

    cos, sin = _rope_tables(s)
    cos, sin = jnp.asarray(cos), jnp.asarray(sin)
    mq = jnp.asarray(_block_ones(512)).astype(BF16)
    mk = jnp.asarray(_block_ones(128)).astype(BF16)
    qg = jnp.tile(ab_q_norm_g[0], 512 // HEAD_DIM).reshape(1, 512)
    kg = jnp.tile(ab_k_norm_g[0], 128 // HEAD_DIM).reshape(1, 128)
    ng0 = ab_norm_g[0].reshape(1, d)

    ck, cv, kc, vc = _ctx_layer(ctx, ng0, cs1p, csh, w_in0, qg, kg, mq, mk, ab_conv_w[0], ab_conv_b[0], ab_ln_g[0],
                                ab_ln_b[0], w_out0, cgate, ng1, cs1p1, csh1, w_in1, 1)

    glu, q, k, v, sz = _inproj_ab(x, ng0, s1p, sh, w_in0, qg, kg, cos, sin, mq, mk, 1024)
    y_b = _gqa_attention(q, [ck, k], [cv, v], 1024, 1, chunk=128)
    y_a = _conformer_conv(glu, ab_conv_w[0], ab_conv_b[0], ab_ln_g[0], ab_ln_b[0], 1024)

    x1, f, q, k, v, sz = _outproj_inproj_cd(y_a, y_b, sz, w_out0, x, gate, ng1, s1p1, sh1, w_in1, 1024)
    gate = gate1

    c_c, s_c = _dft_tables(cd_w_fourier.shape[-1])
    w1 = _fourier_weights(jnp.asarray(c_c), jnp.asarray(s_c), cd_w_fourier[0])
    y_c = _fourier_mix(f, w1, jnp.asarray(_fourier_tables(s)).astype(BF16))

    bias = _na_bias_tables(cd_rpb[0], s // GRID_W)
    y_d = _neighborhood_attention(q, k, v, kc, vc, bias)

    return _outproj_final(y_c, y_d, sz, cd_w_out[0].astype(BF16), x1, gate, final_norm_g.reshape(1, d), 1024)
```

```python
import functools
import math

import numpy as np
import jax
import jax.numpy as jnp
from jax import lax
from jax.experimental import pallas as pl
from jax.experimental.pallas import tpu as pltpu

F32 = jnp.float32
BF16 = jnp.bfloat16
HIGHEST = lax.Precision.HIGHEST

D_MODEL = 1024
HEAD_DIM = 64
GRID_W = 64
CONV_WIDTH = 31
NA_ROWS = 8
NA_COLS = 16
ROPE_THETA = 10000.0
EPS = 1e-6
NEG_INF = -1e30
LOG2E = math.log2(math.e)
Q_SCALE = HEAD_DIM ** -0.5 * LOG2E
LANES = 128
SUBLANES = 8
HALO = 16
VMEM_LIMIT = 48 * 1024 * 1024

NA_TILE_ROWS = 4
NA_WIN_ROWS = NA_TILE_ROWS + NA_ROWS
NA_BATCH_PER_STEP = 8
NA_QUERY_SPLIT = 2
IN_PROJ_SUBTILE = 256
MID_SUBTILE = 512
OUT_PROJ_SUBTILE = 512
DFT_RADIX = 4


def _params(sem):
    return pltpu.CompilerParams(dimension_semantics=sem, vmem_limit_bytes=VMEM_LIMIT)


def _silu(t):
    return t * jax.nn.sigmoid(t)


def _dot(a, b):
    return jnp.dot(a, b, preferred_element_type=F32)


def _dot_nt(a, b):
    return lax.dot_general(a, b, (((1,), (1,)), ((), ())), preferred_element_type=F32)


def _rope_tables(seq):
    t = np.arange(seq)
    row = (t // GRID_W).astype(np.float64)
    col = (t % GRID_W).astype(np.float64)
    nf = HEAD_DIM // 4
    inv = np.float32(ROPE_THETA) ** (-np.arange(nf, dtype=np.float32) / nf)
    ang = np.concatenate([row[:, None] * inv, col[:, None] * inv], axis=-1)
    cos = np.repeat(np.cos(ang), 2, axis=-1)
    sin = np.repeat(np.sin(ang), 2, axis=-1)
    sign = np.tile(np.array([-1.0, 1.0]), HEAD_DIM // 2)
    cos = np.tile(cos, (1, LANES // HEAD_DIM))
    sin = np.tile(sin * sign, (1, LANES // HEAD_DIM))
    return cos.astype(np.float32), sin.astype(np.float32)


def _dft_tables(n):
    k = np.arange(n)
    ang = 2.0 * np.pi * ((k[:, None] * k[None, :]) % n) / n
    s = 1.0 / math.sqrt(n)
    return (np.cos(ang) * s).astype(np.float32), (np.sin(ang) * s).astype(np.float32)


def _block_ones(width):
    h = np.arange(width) // HEAD_DIM
    return (h[:, None] == h[None, :]).astype(np.float32)


def _adaln_body(c_ref, w_ref, b_ref, o_ref):
    a = _silu(c_ref[...]).astype(BF16)
    o_ref[...] = _dot(a, w_ref[...].astype(BF16)) + b_ref[...]


def _adaln(cond, w, b):
    r, d = cond.shape
    n = w.shape[1]
    tn = 1024
    return pl.pallas_call(
        _adaln_body,
        grid=(n // tn,),
        in_specs=[pl.BlockSpec((r, d), lambda j: (0, 0)),
                  pl.BlockSpec((d, tn), lambda j: (0, j)),
                  pl.BlockSpec((1, tn), lambda j: (0, j))],
        out_specs=pl.BlockSpec((r, tn), lambda j: (0, j)),
        out_shape=jax.ShapeDtypeStruct((r, n), F32),
        compiler_params=_params(("arbitrary",)),
        name="adaln",
    )(cond, w, b.reshape(1, n))


def _head_rms(t, ones_ref, gain):
    ms = _dot((t * t).astype(BF16), ones_ref[...]) * (1.0 / HEAD_DIM)
    return t * lax.rsqrt(ms + EPS) * gain


def _rope(t, cos, sin_signed):
    rows = t.shape[0]
    lane = lax.broadcasted_iota(jnp.int32, (rows, LANES), 1)
    even = (lane & 1) == 0
    outs = []
    for c in range(t.shape[1] // LANES):
        xc = t[:, c * LANES:(c + 1) * LANES]
        swapped = jnp.where(even, pltpu.roll(xc, LANES - 1, 1), pltpu.roll(xc, 1, 1))
        outs.append(xc * cos + swapped * sin_signed)
    return outs[0] if len(outs) == 1 else jnp.concatenate(outs, axis=1)


def _modulate(x, g_ref, s1p_ref, sh_ref):
    y = x * lax.rsqrt(jnp.mean(x * x, axis=-1, keepdims=True) + EPS) * g_ref[...]
    return (y * s1p_ref[0] + sh_ref[0]).astype(BF16)


def _row_subtiles(tm, sub):
    sub = min(tm, sub)
    return [slice(r, r + sub) for r in range(0, tm, sub)]


def _inproj_ab_body(x_ref, g_ref, s1p_ref, sh_ref, w_ref, qg_ref, kg_ref, cos_ref, sin_ref, mq_ref, mk_ref,
                    glu_ref, q_ref, k_ref, v_ref, sz_ref):
    for rows in _row_subtiles(x_ref.shape[1], IN_PROJ_SUBTILE):
        u = _dot(_modulate(x_ref[0, rows, :], g_ref, s1p_ref, sh_ref), w_ref[...])
        glu, q, k, v, sz = _ab_epilogue(u, mq_ref, mk_ref, qg_ref, kg_ref, (cos_ref[rows, :], sin_ref[rows, :]))
        glu_ref[0, rows, :] = glu
        q_ref[0, rows, :] = q
        k_ref[0, rows, :] = k
        v_ref[0, rows, :] = v
        sz_ref[0, rows, :] = sz


def _ab_epilogue(u, mq_ref, mk_ref, qg_ref, kg_ref, rope_tabs):
    glu = u[:, 0:512] * jax.nn.sigmoid(u[:, 512:1024])
    q = _head_rms(u[:, 1024:1536], mq_ref, qg_ref[...])
    k = _head_rms(u[:, 1536:1664], mk_ref, kg_ref[...])
    if rope_tabs is not None:
        q = _rope(q, *rope_tabs)
        k = _rope(k, *rope_tabs)
    v = u[:, 1664:1792]
    k_sw, v_sw = pltpu.roll(k, HEAD_DIM, 1), pltpu.roll(v, HEAD_DIM, 1)
    left = lax.broadcasted_iota(jnp.int32, k.shape, 1) < HEAD_DIM
    one = jnp.ones_like(v)
    k_dup = jnp.concatenate([jnp.where(left, k, k_sw), jnp.where(left, k_sw, k)], axis=1)
    v_dup = jnp.concatenate([jnp.where(left, v, one), jnp.where(left, one, v_sw),
                             jnp.where(left, v_sw, one), jnp.where(left, one, v)], axis=1)
    return (glu, (q * Q_SCALE).astype(BF16), k_dup.astype(BF16), v_dup.astype(BF16),
            _silu(u[:, 1792:2816]).astype(BF16))


def _inproj_ab(x, g, s1p, sh, w, qg, kg, cos, sin, mq, mk, tm):
    b, s, d = x.shape
    n = w.shape[1]
    row = lambda bi, i: (bi, i, 0)
    const = lambda bi, i: (0, 0)
    per_b = lambda bi, i: (bi, 0, 0)
    tab = lambda bi, i: (i, 0)
    return pl.pallas_call(
        _inproj_ab_body,
        grid=(b, s // tm),
        in_specs=[pl.BlockSpec((1, tm, d), row),
                  pl.BlockSpec((1, d), const),
                  pl.BlockSpec((1, 1, d), per_b),
                  pl.BlockSpec((1, 1, d), per_b),
                  pl.BlockSpec((d, n), const),
                  pl.BlockSpec((1, 512), const),
                  pl.BlockSpec((1, 128), const),
                  pl.BlockSpec((tm, LANES), tab),
                  pl.BlockSpec((tm, LANES), tab),
                  pl.BlockSpec((512, 512), const),
                  pl.BlockSpec((128, 128), const)],
        out_specs=[pl.BlockSpec((1, tm, 512), row),
                   pl.BlockSpec((1, tm, 512), row),
                   pl.BlockSpec((1, tm, 256), row),
                   pl.BlockSpec((1, tm, 512), row),
                   pl.BlockSpec((1, tm, 1024), row)],
        out_shape=[jax.ShapeDtypeStruct((b, s, 512), F32),
                   jax.ShapeDtypeStruct((b, s, 512), BF16),
                   jax.ShapeDtypeStruct((b, s, 256), BF16),
                   jax.ShapeDtypeStruct((b, s, 512), BF16),
                   jax.ShapeDtypeStruct((b, s, 1024), BF16)],
        compiler_params=_params(("arbitrary", "arbitrary")),
        name="inproj_ab",
    )(x, g, s1p, sh, w, qg, kg, cos, sin, mq, mk)


def _outproj_inproj_cd_body(ya_ref, yb_ref, sz0_ref, wo_ref, x_ref, gate_ref, g_ref, s1p_ref, sh_ref, wi_ref,
                            x1_ref, f_ref, q_ref, k_ref, v_ref, sz_ref):
    for rows in _row_subtiles(x_ref.shape[1], MID_SUBTILE):
        y = (jnp.concatenate([ya_ref[0, rows, :], yb_ref[0, rows, :]], axis=-1).astype(F32)
             * sz0_ref[0, rows, :].astype(F32))
        xn = x_ref[0, rows, :] + gate_ref[0] * _dot(y.astype(BF16), wo_ref[...])
        x1_ref[0, rows, :] = xn
        u = _dot(_modulate(xn, g_ref, s1p_ref, sh_ref), wi_ref[...])
        f_ref[0, rows, :] = u[:, 0:512].astype(BF16)
        q_ref[0, rows, :] = (u[:, 512:1024] * Q_SCALE).astype(BF16)
        k_ref[0, rows, :] = u[:, 1024:1536].astype(BF16)
        v_ref[0, rows, :] = _pairs_with_ones(u[:, 1536:2048]).astype(BF16)
        sz_ref[0, rows, :] = _silu(u[:, 2048:3072]).astype(BF16)


def _outproj_inproj_cd(ya, yb, sz0, w_out, x, gate, g, s1p, sh, w_in, tm):
    b, s, d = x.shape
    n = w_in.shape[1]
    row = lambda bi, i: (bi, i, 0)
    const = lambda bi, i: (0, 0)
    per_b = lambda bi, i: (bi, 0, 0)
    return pl.pallas_call(
        _outproj_inproj_cd_body,
        grid=(b, s // tm),
        in_specs=[pl.BlockSpec((1, tm, 512), row),
                  pl.BlockSpec((1, tm, 512), row),
                  pl.BlockSpec((1, tm, 1024), row),
                  pl.BlockSpec((1024, d), const, pipeline_mode=pl.Buffered(1)),
                  pl.BlockSpec((1, tm, d), row),
                  pl.BlockSpec((1, 1, d), per_b),
                  pl.BlockSpec((1, d), const),
                  pl.BlockSpec((1, 1, d), per_b),
                  pl.BlockSpec((1, 1, d), per_b),
                  pl.BlockSpec((d, n), const, pipeline_mode=pl.Buffered(1))],
        out_specs=[pl.BlockSpec((1, tm, d), row)] + [pl.BlockSpec((1, tm, 512), row)] * 3
        + [pl.BlockSpec((1, tm, 1024), row)] * 2,
        out_shape=[jax.ShapeDtypeStruct((b, s, d), F32)] + [jax.ShapeDtypeStruct((b, s, 512), BF16)] * 3
        + [jax.ShapeDtypeStruct((b, s, 1024), BF16)] * 2,
        compiler_params=_params(("arbitrary", "arbitrary")),
        name="outproj_inproj_cd",
    )(ya, yb, sz0, w_out, x, gate, g, s1p, sh, w_in)


def _split_heads(q):
    lane = lax.broadcasted_iota(jnp.int32, q.shape, 1)
    zero = jnp.zeros_like(q)
    return jnp.concatenate([jnp.where(lane < HEAD_DIM, q, zero), jnp.where(lane >= HEAD_DIM, q, zero)], axis=0)


def _with_ones(v):
    lane = lax.broadcasted_iota(jnp.int32, v.shape, 1)
    one = jnp.ones_like(v)
    return jnp.concatenate([jnp.where(lane < HEAD_DIM, v, one), jnp.where(lane < HEAD_DIM, one, v)], axis=1)


def _pairs_with_ones(v):
    return jnp.concatenate([_with_ones(v[:, c:c + LANES]) for c in range(0, v.shape[1], LANES)], axis=1)


def _pv(p, v2, rows):
    pv = _dot(p.astype(BF16), v2)
    return jnp.concatenate([pv[:rows, :LANES], pv[rows:, LANES:]], axis=0)


def _online_softmax_pv(q2, chunks, online=True):
    rows = q2.shape[0] // 2
    scores = [_dot_nt(q2, k) if bias is None else _dot_nt(q2, k) + bias for k, _, bias in chunks]
    if online:
        m = acc = None
        for s, (_, v2, _) in zip(scores, chunks):
            mc = jnp.max(s, axis=-1, keepdims=True)
            m_new = mc if m is None else jnp.maximum(m, mc)
            pv = _pv(jnp.exp2(s - m_new), v2, rows)
            acc = pv if m is None else jnp.exp2(m - m_new) * acc + pv
            m = m_new
    else:
        m = functools.reduce(jnp.maximum, [jnp.max(s, axis=-1, keepdims=True) for s in scores])
        acc = functools.reduce(jnp.add, [_pv(jnp.exp2(s - m), v2, rows) for s, (_, v2, _) in zip(scores, chunks)])
    left = lax.broadcasted_iota(jnp.int32, (rows, LANES), 1) < HEAD_DIM
    top, bot = acc[:rows], acc[rows:]
    return jnp.where(left, top, bot) / pltpu.roll(jnp.where(left, bot, top), HEAD_DIM, 1)


V_ROWS = 80

def _softmax_pv_keys_major(q2, chunks):
    half = q2.shape[0] // 2
    scores = [_dot_nt(k, q2) for k, _ in chunks]
    m = acc = None
    for s, (_, v_ones) in zip(scores, chunks):
        mc = jnp.max(s, axis=0, keepdims=True)
        m_new = mc if m is None else jnp.maximum(m, mc)
        pv = _dot(v_ones.T[:V_ROWS], jnp.exp2(s - m_new).astype(BF16))
        acc = pv if m is None else jnp.exp2(m - m_new) * acc + pv
        m = m_new
    o = acc[:HEAD_DIM] / acc[HEAD_DIM:HEAD_DIM + 1]
    return jnp.concatenate([o[:, :half], o[:, half:]], axis=0).T


def _key_chunks(k_ref, v_ref, kv, chunk):
    n = k_ref.shape[1]
    kl, vl = slice(kv * LANES, (kv + 1) * LANES), slice(kv * 2 * LANES, (kv + 1) * 2 * LANES)
    return [(k_ref[0, c:c + chunk, kl], v_ref[0, c:c + chunk, vl], None) for c in range(0, n, chunk)]


def _gqa_body(n_src, chunk, q_ref, *refs):
    k_refs, v_refs, o_ref = refs[:n_src], refs[n_src:2 * n_src], refs[2 * n_src]
    pairs = q_ref.shape[2] // LANES
    kv_heads = k_refs[0].shape[2] // LANES
    for p in range(pairs):
        lanes = slice(p * LANES, (p + 1) * LANES)
        kv = p * kv_heads // pairs
        chunks = [c for k, v in zip(k_refs, v_refs) for c in _key_chunks(k, v, kv, min(chunk, k.shape[1]))]
        chunks = [(k, v2[:, :LANES]) for k, v2, _ in chunks]
        o_ref[0, :, lanes] = _softmax_pv_keys_major(_split_heads(q_ref[0, :, lanes]), chunks).astype(o_ref.dtype)


def _gqa_attention(q, ks, vs, tq, pairs, chunk=512):
    b, lq, w = q.shape
    n_src = len(ks)
    n_pairs = w // LANES
    kv_heads = ks[0].shape[2] // LANES
    kv_blk = max(1, pairs * kv_heads // n_pairs)
    kv_idx = lambda bi, i, j: (bi, 0, j * pairs * kv_heads // n_pairs // kv_blk)
    k_specs = [pl.BlockSpec((1, k.shape[1], kv_blk * LANES), kv_idx) for k in ks]
    v_specs = [pl.BlockSpec((1, v.shape[1], kv_blk * 2 * LANES), kv_idx) for v in vs]
    return pl.pallas_call(
        functools.partial(_gqa_body, n_src, chunk),
        grid=(b, lq // tq, n_pairs // pairs),
        in_specs=[pl.BlockSpec((1, tq, pairs * LANES), lambda bi, i, j: (bi, i, j))] + k_specs + v_specs,
        out_specs=pl.BlockSpec((1, tq, pairs * LANES), lambda bi, i, j: (bi, i, j)),
        out_shape=jax.ShapeDtypeStruct((b, lq, w), BF16),
        compiler_params=_params(("arbitrary", "arbitrary", "arbitrary")),
        name=f"gqa_attention_{n_src}src",
    )(q, *ks, *vs)


def _conv_core(rows, pad_ref, shift_ref, w_ref, cb_ref, lg_ref, lb_ref, store):
    span = shift_ref.shape[1]
    for r in range(1, SUBLANES):
        shift_ref[r] = pad_ref[r:r + span, :]
    sub = 32
    base = HALO - CONV_WIDTH // 2
    for r0 in range(0, rows, sub):
        acc = jnp.zeros((sub, pad_ref.shape[1]), F32) + cb_ref[...]
        for k in range(CONV_WIDTH):
            a, r = divmod(base + k, SUBLANES)
            lo = a * SUBLANES + r0
            tap = pad_ref[lo:lo + sub, :] if r == 0 else shift_ref[r, lo:lo + sub, :]
            acc = acc + tap * jnp.concatenate([w_ref[k]] * (sub // SUBLANES), axis=0)
        mu = jnp.mean(acc, axis=-1, keepdims=True)
        xc = acc - mu
        y = xc * lax.rsqrt(jnp.mean(xc * xc, axis=-1, keepdims=True) + EPS) * lg_ref[...] + lb_ref[...]
        store(r0, sub, _silu(y))


def _conv_body(rows, x_ref, prev_ref, next_ref, w_ref, cb_ref, lg_ref, lb_ref, o_ref, pad_ref, shift_ref):
    i = pl.program_id(1)
    last = pl.num_programs(1) - 1
    pad_ref[0:HALO, :] = jnp.where(i > 0, prev_ref[0], 0.0)
    pad_ref[HALO:HALO + rows, :] = x_ref[0]
    pad_ref[HALO + rows:HALO + rows + HALO, :] = jnp.where(i < last, next_ref[0], 0.0)

    def store(r0, sub, y):
        o_ref[0, r0:r0 + sub, :] = y.astype(o_ref.dtype)

    _conv_core(rows, pad_ref, shift_ref, w_ref, cb_ref, lg_ref, lb_ref, store)


def _conformer_conv(u, conv_w, conv_b, ln_g, ln_b, rows):
    b, l, c = u.shape
    hb = rows // HALO
    n_halo = l // HALO
    return pl.pallas_call(
        functools.partial(_conv_body, rows),
        grid=(b, l // rows),
        in_specs=[pl.BlockSpec((1, rows, c), lambda bi, i: (bi, i, 0)),
                  pl.BlockSpec((1, HALO, c), lambda bi, i: (bi, jnp.maximum(i * hb - 1, 0), 0)),
                  pl.BlockSpec((1, HALO, c), lambda bi, i: (bi, jnp.minimum((i + 1) * hb, n_halo - 1), 0)),
                  pl.BlockSpec((CONV_WIDTH, SUBLANES, c), lambda bi, i: (0, 0, 0)),
                  pl.BlockSpec((1, c), lambda bi, i: (0, 0)),
                  pl.BlockSpec((1, c), lambda bi, i: (0, 0)),
                  pl.BlockSpec((1, c), lambda bi, i: (0, 0))],
        out_specs=pl.BlockSpec((1, rows, c), lambda bi, i: (bi, i, 0)),
        out_shape=jax.ShapeDtypeStruct((b, l, c), BF16),
        scratch_shapes=[pltpu.VMEM((rows + 2 * HALO, c), F32),
                        pltpu.VMEM((SUBLANES, rows + 2 * HALO - SUBLANES, c), F32)],
        compiler_params=_params(("arbitrary", "arbitrary")),
        name="conformer_conv",
    )(u, u, u, jnp.broadcast_to(conv_w[:, None, :], (CONV_WIDTH, SUBLANES, c)),
      conv_b.reshape(1, c), ln_g.reshape(1, c), ln_b.reshape(1, c))


def _ctx_layer_body(x_ref, g0_ref, s1p0_ref, sh0_ref, wi0_ref, qg_ref, kg_ref, mq_ref, mk_ref, cw_ref, cb_ref,
                    lg_ref, lb_ref, wo0_ref, gate0_ref, g1_ref, s1p1_ref, sh1_ref, wkv1_ref,
                    ck_ref, cv_ref, kc_ref, vc_ref, pad_ref, shift_ref, ya_ref):
    rows = x_ref.shape[1]
    x = x_ref[0]
    u = _dot(_modulate(x, g0_ref, s1p0_ref, sh0_ref), wi0_ref[...])
    glu, q, k_dup, v_dup, sz = _ab_epilogue(u, mq_ref, mk_ref, qg_ref, kg_ref, None)
    ck_ref[0] = k_dup
    cv_ref[0] = v_dup
    pairs, kv_heads = q.shape[1] // LANES, k_dup.shape[1] // LANES
    yb = []
    for p in range(pairs):
        kv = p * kv_heads // pairs
        keys = [(k_dup[:, kv * LANES:(kv + 1) * LANES], v_dup[:, kv * 2 * LANES:(kv + 1) * 2 * LANES], None)]
        yb.append(_online_softmax_pv(_split_heads(q[:, p * LANES:(p + 1) * LANES]), keys, online=False))
    zeros = jnp.zeros((HALO, glu.shape[1]), F32)
    pad_ref[0:HALO, :] = zeros
    pad_ref[HALO:HALO + rows, :] = glu
    pad_ref[HALO + rows:HALO + rows + HALO, :] = zeros

    def store(r0, sub, y):
        ya_ref[r0:r0 + sub, :] = y

    _conv_core(rows, pad_ref, shift_ref, cw_ref, cb_ref, lg_ref, lb_ref, store)
    y = jnp.concatenate([ya_ref[...]] + yb, axis=-1) * sz.astype(F32)
    x1 = x + gate0_ref[0] * _dot(y.astype(BF16), wo0_ref[...])
    u1 = _dot(_modulate(x1, g1_ref, s1p1_ref, sh1_ref), wkv1_ref[...])
    kc_ref[0] = u1[:, 0:512].astype(BF16)
    vc_ref[0] = _pairs_with_ones(u1[:, 512:1024]).astype(BF16)


def _ctx_layer(ctx, g0, s1p0, sh0, w_in0, qg, kg, mq, mk, conv_w, conv_b, ln_g, ln_b, w_out0, gate0,
               g1, s1p1, sh1, w_in1, kv_col_block):
    b, l, d = ctx.shape
    c = conv_w.shape[1]
    row = lambda bi: (bi, 0, 0)
    const2 = lambda bi: (0, 0)
    vec = pl.BlockSpec((1, d), const2)
    per_b = pl.BlockSpec((1, 1, d), row)
    cvec = pl.BlockSpec((1, c), const2)
    return pl.pallas_call(
        _ctx_layer_body,
        grid=(b,),
        in_specs=[pl.BlockSpec((1, l, d), row), vec, per_b, per_b,
                  pl.BlockSpec(w_in0.shape, const2),
                  pl.BlockSpec((1, 512), const2), pl.BlockSpec((1, 128), const2),
                  pl.BlockSpec((512, 512), const2), pl.BlockSpec((128, 128), const2),
                  pl.BlockSpec((CONV_WIDTH, SUBLANES, c), lambda bi: (0, 0, 0)), cvec, cvec, cvec,
                  pl.BlockSpec(w_out0.shape, const2), per_b,
                  vec, per_b, per_b,
                  pl.BlockSpec((d, 1024), lambda bi: (0, kv_col_block))],
        out_specs=[pl.BlockSpec((1, l, 256), row), pl.BlockSpec((1, l, 512), row),
                   pl.BlockSpec((1, l, 512), row), pl.BlockSpec((1, l, 1024), row)],
        out_shape=[jax.ShapeDtypeStruct((b, l, 256), BF16), jax.ShapeDtypeStruct((b, l, 512), BF16),
                   jax.ShapeDtypeStruct((b, l, 512), BF16), jax.ShapeDtypeStruct((b, l, 1024), BF16)],
        scratch_shapes=[pltpu.VMEM((l + 2 * HALO, c), F32),
                        pltpu.VMEM((SUBLANES, l + 2 * HALO - SUBLANES, c), F32),
                        pltpu.VMEM((l, c), F32)],
        compiler_params=_params(("arbitrary",)),
        name="ctx_layer",
    )(ctx, g0, s1p0, sh0, w_in0, qg, kg, mq, mk,
      jnp.broadcast_to(conv_w[:, None, :], (CONV_WIDTH, SUBLANES, c)), conv_b.reshape(1, c), ln_g.reshape(1, c),
      ln_b.reshape(1, c), w_out0, gate0, g1, s1p1, sh1, w_in1)


def _outproj_final_body(ya_ref, yb_ref, sz_ref, w_ref, x_ref, gate_ref, perm_ref, fg_ref, o_ref):
    sub = perm_ref.shape[0]
    per = sub // DFT_RADIX
    for h in range(x_ref.shape[1] // sub):
        rows = slice(h * sub, (h + 1) * sub)
        stacked = jnp.concatenate([ya_ref[0, j, h * per:(h + 1) * per, :] for j in range(DFT_RADIX)], axis=0)
        ya = _dot(perm_ref[...], stacked)
        y = jnp.concatenate([ya, yb_ref[0, rows, :].astype(F32)], axis=-1) * sz_ref[0, rows, :].astype(F32)
        xn = x_ref[0, rows, :] + gate_ref[0] * _dot(y.astype(BF16), w_ref[...])
        o_ref[0, rows, :] = xn * lax.rsqrt(jnp.mean(xn * xn, axis=-1, keepdims=True) + EPS) * fg_ref[...]


def _outproj_final(ya, yb, sz, w, x, gate, final_g, tm):
    b, s, d = x.shape
    row = lambda bi, i: (bi, i, 0)
    const = lambda bi, i: (0, 0)
    return pl.pallas_call(
        _outproj_final_body,
        grid=(b, s // tm),
        in_specs=[pl.BlockSpec((1, DFT_RADIX, tm // DFT_RADIX, 512), lambda bi, i: (bi, 0, i, 0)),
                  pl.BlockSpec((1, tm, 512), row),
                  pl.BlockSpec((1, tm, 1024), row),
                  pl.BlockSpec((1024, d), const),
                  pl.BlockSpec((1, tm, d), row),
                  pl.BlockSpec((1, 1, d), lambda bi, i: (bi, 0, 0)),
                  pl.BlockSpec((OUT_PROJ_SUBTILE, OUT_PROJ_SUBTILE), const),
                  pl.BlockSpec((1, d), const)],
        out_specs=pl.BlockSpec((1, tm, d), row),
        out_shape=jax.ShapeDtypeStruct((b, s, d), F32),
        compiler_params=_params(("arbitrary", "arbitrary")),
        name="outproj_final",
    )(ya, yb, sz, w, x, gate, jnp.asarray(_class_row_permutation(OUT_PROJ_SUBTILE)).astype(BF16), final_g)


def _fourier_w_body(cc_ref, sc_ref, w_ref, o_ref):
    for g in range(w_ref.shape[0]):
        wc = jnp.dot(cc_ref[...], w_ref[g], preferred_element_type=F32, precision=HIGHEST)
        ws = jnp.dot(sc_ref[...], w_ref[g], preferred_element_type=F32, precision=HIGHEST)
        o_ref[g] = jnp.concatenate([wc, ws], axis=-1).astype(o_ref.dtype)


def _fourier_weights(cc, sc, w):
    g, c, _ = w.shape
    return pl.pallas_call(
        _fourier_w_body,
        out_shape=jax.ShapeDtypeStruct((g, c, 2 * c), BF16),
        name="fourier_weights",
    )(cc, sc, w)


def _fourier_tables(seq):
    quarter = seq // DFT_RADIX
    m = np.arange(quarter)
    tabs = []
    for j in range(DFT_RADIX):
        ang = 2.0 * np.pi * (((DFT_RADIX * m + j)[:, None] * m[None, :]) % seq) / seq
        tabs.append(np.concatenate([np.cos(ang), -np.sin(ang)], axis=1) / math.sqrt(seq))
    return np.stack(tabs).astype(np.float32)


def _class_row_permutation(rows):
    per = rows // DFT_RADIX
    p = np.zeros((rows, rows), np.float32)
    for j in range(DFT_RADIX):
        p[DFT_RADIX * np.arange(per) + j, j * per + np.arange(per)] = 1.0
    return p


def _fourier_body(f_ref, w1_ref, tab_ref, o_ref, yc_ref, ys_ref):
    groups, c, _ = w1_ref.shape
    quarter = f_ref.shape[1] // DFT_RADIX
    for g in range(groups):
        yg = _dot(f_ref[0, :, g * c:(g + 1) * c], w1_ref[g])
        yc_ref[:, g * c:(g + 1) * c] = yg[:, :c]
        ys_ref[:, g * c:(g + 1) * c] = yg[:, c:]
    for j in range(DFT_RADIX):
        pc, ps = [], []
        for q in range(DFT_RADIX):
            rows = slice(q * quarter, (q + 1) * quarter)
            a = (j * q) % 4
            if a == 0:
                pc.append((1, yc_ref, rows)), ps.append((1, ys_ref, rows))
            elif a == 1:
                pc.append((-1, ys_ref, rows)), ps.append((1, yc_ref, rows))
            elif a == 2:
                pc.append((-1, yc_ref, rows)), ps.append((-1, ys_ref, rows))
            else:
                pc.append((1, ys_ref, rows)), ps.append((-1, yc_ref, rows))
        folded = []
        for terms in (pc, ps):
            terms = sorted(terms, key=lambda t: -t[0])
            acc = terms[0][1][terms[0][2], :]
            for sign, ref, rows in terms[1:]:
                acc = acc + ref[rows, :] if sign > 0 else acc - ref[rows, :]
            folded.append(acc.astype(BF16))
        o_ref[0, j] = _dot(tab_ref[j], jnp.concatenate(folded, axis=0)).astype(o_ref.dtype)


def _fourier_mix(f, w1, tab):
    b, s, w = f.shape
    quarter = s // DFT_RADIX
    return pl.pallas_call(
        _fourier_body,
        grid=(b,),
        in_specs=[pl.BlockSpec((1, s, w), lambda bi: (bi, 0, 0)),
                  pl.BlockSpec(w1.shape, lambda bi: (0, 0, 0)),
                  pl.BlockSpec(tab.shape, lambda bi: (0, 0, 0))],
        out_specs=pl.BlockSpec((1, DFT_RADIX, quarter, w), lambda bi: (bi, 0, 0, 0)),
        out_shape=jax.ShapeDtypeStruct((b, DFT_RADIX, quarter, w), BF16),
        scratch_shapes=[pltpu.VMEM((s, w), F32), pltpu.VMEM((s, w), F32)],
        compiler_params=_params(("arbitrary",)),
        name="fourier_mix",
    )(f, w1, tab)


def _na_tile_geometry(rows_total):
    last_r0 = rows_total - NA_TILE_ROWS
    return ((0, 0), (2 * NA_TILE_ROWS, 2 * NA_TILE_ROWS - NA_ROWS // 2), (last_r0, rows_total - NA_WIN_ROWS))


def _na_bias_body(rows_total, rpb_ref, o_ref, t2_ref):
    h = pl.program_id(0)
    wq = lax.broadcasted_iota(jnp.int32, (GRID_W, LANES), 0)
    wk = lax.broadcasted_iota(jnp.int32, (GRID_W, LANES), 1) & (GRID_W - 1)
    cs = jnp.clip(wq - NA_COLS // 2, 0, GRID_W - NA_COLS)
    col_ok = (wk >= cs) & (wk < cs + NA_COLS)
    dc = wk - wq + (NA_COLS - 1)
    n_dr, n_dc = 2 * NA_ROWS - 1, 2 * NA_COLS - 1
    neg = jnp.full((GRID_W, LANES), NEG_INF, F32)
    for dr in range(n_dr):
        val = neg
        for c in range(n_dc):
            val = jnp.where(col_ok & (dc == c), rpb_ref[h * (n_dr * n_dc) + dr * n_dc + c] * LOG2E, val)
        t2_ref[dr] = val
    lane = lax.broadcasted_iota(jnp.int32, (GRID_W, LANES), 1)
    left = lane < GRID_W
    for ty, (r0, ws) in enumerate(_na_tile_geometry(rows_total)):
        for i in range(NA_TILE_ROWS):
            r = r0 + i
            rs = min(max(r - NA_ROWS // 2, 0), rows_total - NA_ROWS)
            for jp in range(NA_WIN_ROWS // 2):
                halves = []
                for kr in (ws + 2 * jp, ws + 2 * jp + 1):
                    halves.append(t2_ref[kr - r + NA_ROWS - 1] if rs <= kr < rs + NA_ROWS else neg)
                o_ref[0, ty, i * GRID_W:(i + 1) * GRID_W, jp * LANES:(jp + 1) * LANES] = jnp.where(left, halves[0], halves[1])


def _na_bias_tables(rpb, rows_total):
    h = rpb.shape[0]
    q_rows, k_cols = NA_TILE_ROWS * GRID_W, NA_WIN_ROWS * GRID_W
    return pl.pallas_call(
        functools.partial(_na_bias_body, rows_total),
        grid=(h,),
        in_specs=[pl.BlockSpec(memory_space=pltpu.SMEM)],
        out_specs=pl.BlockSpec((1, 3, q_rows, k_cols), lambda hi: (hi, 0, 0, 0)),
        out_shape=jax.ShapeDtypeStruct((h, 3, q_rows, k_cols), F32),
        scratch_shapes=[pltpu.VMEM((2 * NA_ROWS - 1, GRID_W, LANES), F32)],
        compiler_params=_params(("arbitrary",)),
        name="na_bias_tables",
    )(rpb.reshape(-1))


def _na_body(rows_total, q_ref, k_ref, v_ref, kc_ref, vc_ref, bias_ref, o_ref):
    t = pl.program_id(2)
    n_t = pl.num_programs(2)
    ty = jnp.where(t == 0, 0, jnp.where(t == n_t - 1, 2, 1))
    ws = jnp.clip(t * NA_TILE_ROWS - NA_ROWS // 2, 0, rows_total - NA_WIN_ROWS)
    off = pl.multiple_of(ws * GRID_W, GRID_W)
    n_win = NA_WIN_ROWS * GRID_W
    tq = q_ref.shape[1]
    part = tq // NA_QUERY_SPLIT
    for bb in range(q_ref.shape[0]):
        for qs in range(NA_QUERY_SPLIT):
            rows = slice(qs * part, (qs + 1) * part)
            bias = jnp.concatenate([bias_ref[0, ty, rows, :], bias_ref[1, ty, rows, :]], axis=0)
            chunks = [(kc_ref[bb], vc_ref[bb], None),
                      (k_ref[bb, pl.ds(off, n_win), :], v_ref[bb, pl.ds(off, n_win), :], bias)]
            o = _online_softmax_pv(_split_heads(q_ref[bb, rows, :]), chunks, online=False)
            o_ref[bb, rows, :] = o.astype(o_ref.dtype)


def _neighborhood_attention(q, k, v, kc, vc, bias):
    b, s, w = q.shape
    l = kc.shape[1]
    rows_total = s // GRID_W
    tq = NA_TILE_ROWS * GRID_W
    nb = NA_BATCH_PER_STEP
    full = lambda j, bi, t: (bi, 0, j)
    return pl.pallas_call(
        functools.partial(_na_body, rows_total),
        grid=(w // LANES, b // nb, s // tq),
        in_specs=[pl.BlockSpec((nb, tq, LANES), lambda j, bi, t: (bi, t, j)),
                  pl.BlockSpec((nb, s, LANES), full),
                  pl.BlockSpec((nb, s, 2 * LANES), full),
                  pl.BlockSpec((nb, l, LANES), full),
                  pl.BlockSpec((nb, l, 2 * LANES), full),
                  pl.BlockSpec((2,) + bias.shape[1:], lambda j, bi, t: (j, 0, 0, 0))],
        out_specs=pl.BlockSpec((nb, tq, LANES), lambda j, bi, t: (bi, t, j)),
        out_shape=jax.ShapeDtypeStruct((b, s, w), BF16),
        compiler_params=_params(("arbitrary", "arbitrary", "arbitrary")),
        name="neighborhood_attention",
    )(q, k, v, kc, vc, bias)


def _modulation(m, rows, batch):
    d = m.shape[1] // 3
    pick = lambda lo: jnp.broadcast_to(m[rows, lo:lo + d].reshape(-1, 1, d), (batch, 1, d))
    return pick(0), 1.0 + pick(d), pick(2 * d)


def kernel(x, c, ctx, c_ctx, ab_w_ada, ab_b_ada, ab_norm_g, ab_w_in, ab_conv_w, ab_conv_b, ab_ln_g, ab_ln_b, ab_q_norm_g, ab_k_norm_g, ab_w_out, cd_w_ada, cd_b_ada, cd_norm_g, cd_w_in, cd_w_fourier, cd_rpb, cd_w_out, final_norm_g):
    b, s, d = x.shape
    l = ctx.shape[1]
    assert d == D_MODEL and s % (NA_TILE_ROWS * GRID_W) == 0 and ab_w_ada.shape[0] == 1 and cd_w_ada.shape[0] == 1

    cond = jnp.zeros((16, d), F32).at[:b].set(c).at[b].set(c_ctx)
    lat = slice(0, b)
    cx = slice(b, b + 1)

    m0 = _adaln(cond, ab_w_ada[0], ab_b_ada[0])
    m1 = _adaln(cond, cd_w_ada[0], cd_b_ada[0])
    sh, s1p, gate = _modulation(m0, lat, b)
    csh, cs1p, cgate = _modulation(m0, cx, b)
    sh1, s1p1, gate1 = _modulation(m1, lat, b)
    csh1, cs1p1, _ = _modulation(m1, cx, b)
    ng1 = cd_norm_g[0].reshape(1, d)
    w_in1 = cd_w_in[0].astype(BF16)

    w_in0 = ab_w_in[0].astype(BF16)
    w_out0 = ab_w_out[0].astype(BF16)

    cos, sin = _rope_tables(s)
    cos, sin = jnp.asarray(cos), jnp.asarray(sin)
    mq = jnp.asarray(_block_ones(512)).astype(BF16)
    mk = jnp.asarray(_block_ones(128)).astype(BF16)
    qg = jnp.tile(ab_q_norm_g[0], 512 // HEAD_DIM).reshape(1, 512)
    kg = jnp.tile(ab_k_norm_g[0], 128 // HEAD_DIM).reshape(1, 128)
    ng0 = ab_norm_g[0].reshape(1, d)

    ck, cv, kc, vc = _ctx_layer(ctx, ng0, cs1p, csh, w_in0, qg, kg, mq, mk, ab_conv_w[0], ab_conv_b[0], ab_ln_g[0],
                                ab_ln_b[0], w_out0, cgate, ng1, cs1p1, csh1, w_in1, 1)

    glu, q, k, v, sz = _inproj_ab(x, ng0, s1p, sh, w_in0, qg, kg, cos, sin, mq, mk, 1024)
    y_b = _gqa_attention(q, [ck, k], [cv, v], 1024, 1, chunk=128)
    y_a = _conformer_conv(glu, ab_conv_w[0], ab_conv_b[0], ab_ln_g[0], ab_ln_b[0], 1024)

    x1, f, q, k, v, sz = _outproj_inproj_cd(y_a, y_b, sz, w_out0, x, gate, ng1, s1p1, sh1, w_in1, 1024)
    gate = gate1

    c_c, s_c = _dft_tables(cd_w_fourier.shape[-1])
    w1 = _fourier_weights(jnp.asarray(c_c), jnp.asarray(s_c), cd_w_fourier[0])
    y_c = _fourier_mix(f, w1, jnp.asarray(_fourier_tables(s)).astype(BF16))

    bias = _na_bias_tables(cd_rpb[0], s // GRID_W)
    y_d = _neighborhood_attention(q, k, v, kc, vc, bias)

    return _outproj_final(y_c, y_d, sz, cd_w_out[0].astype(BF16), x1, gate, final_norm_g.reshape(1, d), 1024)
```

```python
import functools
import math

import numpy as np
import jax
import jax.numpy as jnp
from jax import lax
from jax.experimental import pallas as pl
from jax.experimental.pallas import tpu as pltpu

F32 = jnp.float32
BF16 = jnp.bfloat16
HIGHEST = lax.Precision.HIGHEST

D_MODEL = 1024
HEAD_DIM = 64
GRID_W = 64
CONV_WIDTH = 31
NA_ROWS = 8
NA_COLS = 16
ROPE_THETA = 10000.0
EPS = 1e-6
NEG_INF = -1e30
LOG2E = math.log2(math.e)
Q_SCALE = HEAD_DIM ** -0.5 * LOG2E
LANES = 128
SUBLANES = 8
HALO = 16
VMEM_LIMIT = 48 * 1024 * 1024

NA_TILE_ROWS = 4
NA_WIN_ROWS = NA_TILE_ROWS + NA_ROWS
NA_BATCH_PER_STEP = 8
NA_QUERY_SPLIT = 2
IN_PROJ_SUBTILE = 128
MID_SUBTILE = 1024
OUT_PROJ_SUBTILE = 512
DFT_RADIX = 4
V_ROWS = 80


def _params(sem):
    return pltpu.CompilerParams(dimension_semantics=sem, vmem_limit_bytes=VMEM_LIMIT)


def _silu(t):
    return t * jax.nn.sigmoid(t)


def _dot(a, b):
    return jnp.dot(a, b, preferred_element_type=F32)


def _dot_nt(a, b):
    return lax.dot_general(a, b, (((1,), (1,)), ((), ())), preferred_element_type=F32)


def _rope_tables(seq):
    t = np.arange(seq)
    row = (t // GRID_W).astype(np.float64)
    col = (t % GRID_W).astype(np.float64)
    nf = HEAD_DIM // 4
    inv = np.float32(ROPE_THETA) ** (-np.arange(nf, dtype=np.float32) / nf)
    ang = np.concatenate([row[:, None] * inv, col[:, None] * inv], axis=-1)
    cos = np.repeat(np.cos(ang), 2, axis=-1)
    sin = np.repeat(np.sin(ang), 2, axis=-1)
    sign = np.tile(np.array([-1.0, 1.0]), HEAD_DIM // 2)
    cos = np.tile(cos, (1, LANES // HEAD_DIM))
    sin = np.tile(sin * sign, (1, LANES // HEAD_DIM))
    return cos.astype(np.float32), sin.astype(np.float32)


def _dft_tables(n):
    k = np.arange(n)
    ang = 2.0 * np.pi * ((k[:, None] * k[None, :]) % n) / n
    s = 1.0 / math.sqrt(n)
    return (np.cos(ang) * s).astype(np.float32), (np.sin(ang) * s).astype(np.float32)


def _block_ones(width):
    h = np.arange(width) // HEAD_DIM
    return (h[:, None] == h[None, :]).astype(np.float32)


def _adaln_body(c_ref, w_ref, b_ref, o_ref):
    a = _silu(c_ref[...]).astype(BF16)
    o_ref[...] = _dot(a, w_ref[...].astype(BF16)) + b_ref[...]


def _adaln(cond, w, b):
    r, d = cond.shape
    n = w.shape[1]
    tn = 1024
    return pl.pallas_call(
        _adaln_body,
        grid=(n // tn,),
        in_specs=[pl.BlockSpec((r, d), lambda j: (0, 0)),
                  pl.BlockSpec((d, tn), lambda j: (0, j)),
                  pl.BlockSpec((1, tn), lambda j: (0, j))],
        out_specs=pl.BlockSpec((r, tn), lambda j: (0, j)),
        out_shape=jax.ShapeDtypeStruct((r, n), F32),
        compiler_params=_params(("arbitrary",)),
        name="adaln",
    )(cond, w, b.reshape(1, n))


def _head_rms(t, ones_ref, gain):
    ms = _dot((t * t).astype(BF16), ones_ref[...]) * (1.0 / HEAD_DIM)
    return t * lax.rsqrt(ms + EPS) * gain


def _rope(t, cos, sin_signed):
    rows = t.shape[0]
    lane = lax.broadcasted_iota(jnp.int32, (rows, LANES), 1)
    even = (lane & 1) == 0
    outs = []
    for c in range(t.shape[1] // LANES):
        xc = t[:, c * LANES:(c + 1) * LANES]
        swapped = jnp.where(even, pltpu.roll(xc, LANES - 1, 1), pltpu.roll(xc, 1, 1))
        outs.append(xc * cos + swapped * sin_signed)
    return outs[0] if len(outs) == 1 else jnp.concatenate(outs, axis=1)


def _modulate(x, g_ref, s1p_ref, sh_ref):
    y = x * lax.rsqrt(jnp.mean(x * x, axis=-1, keepdims=True) + EPS) * g_ref[...]
    return (y * s1p_ref[0] + sh_ref[0]).astype(BF16)


def _row_subtiles(tm, sub):
    sub = min(tm, sub)
    return [slice(r, r + sub) for r in range(0, tm, sub)]


def _inproj_ab_body(x_ref, g_ref, s1p_ref, sh_ref, w_ref, qg_ref, kg_ref, cos_ref, sin_ref, mq_ref, mk_ref,
                    glu_ref, q_ref, k_ref, v_ref, sz_ref):
    for rows in _row_subtiles(x_ref.shape[1], IN_PROJ_SUBTILE):
        u = _dot(_modulate(x_ref[0, rows, :], g_ref, s1p_ref, sh_ref), w_ref[...])
        glu, q, k, v, sz = _ab_epilogue(u, mq_ref, mk_ref, qg_ref, kg_ref, (cos_ref[rows, :], sin_ref[rows, :]))
        glu_ref[0, rows, :] = glu
        q_ref[0, rows, :] = q
        k_ref[0, rows, :] = k
        v_ref[0, rows, :] = v
        sz_ref[0, rows, :] = sz


def _ab_epilogue(u, mq_ref, mk_ref, qg_ref, kg_ref, rope_tabs):
    glu = u[:, 0:512] * jax.nn.sigmoid(u[:, 512:1024])
    q = _head_rms(u[:, 1024:1536], mq_ref, qg_ref[...])
    k = _head_rms(u[:, 1536:1664], mk_ref, kg_ref[...])
    if rope_tabs is not None:
        q = _rope(q, *rope_tabs)
        k = _rope(k, *rope_tabs)
    v = u[:, 1664:1792]
    k_sw, v_sw = pltpu.roll(k, HEAD_DIM, 1), pltpu.roll(v, HEAD_DIM, 1)
    left = lax.broadcasted_iota(jnp.int32, k.shape, 1) < HEAD_DIM
    one = jnp.ones_like(v)
    k_dup = jnp.concatenate([jnp.where(left, k, k_sw), jnp.where(left, k_sw, k)], axis=1)
    v_dup = jnp.concatenate([jnp.where(left, v, one), jnp.where(left, one, v_sw),
                             jnp.where(left, v_sw, one), jnp.where(left, one, v)], axis=1)
    return (glu, (q * Q_SCALE).astype(BF16), k_dup.astype(BF16), v_dup.astype(BF16),
            _silu(u[:, 1792:2816]).astype(BF16))


def _inproj_ab(x, g, s1p, sh, w, qg, kg, cos, sin, mq, mk, tm):
    b, s, d = x.shape
    n = w.shape[1]
    row = lambda bi, i: (bi, i, 0)
    const = lambda bi, i: (0, 0)
    per_b = lambda bi, i: (bi, 0, 0)
    tab = lambda bi, i: (i, 0)
    return pl.pallas_call(
        _inproj_ab_body,
        grid=(b, s // tm),
        in_specs=[pl.BlockSpec((1, tm, d), row),
                  pl.BlockSpec((1, d), const),
                  pl.BlockSpec((1, 1, d), per_b),
                  pl.BlockSpec((1, 1, d), per_b),
                  pl.BlockSpec((d, n), const),
                  pl.BlockSpec((1, 512), const),
                  pl.BlockSpec((1, 128), const),
                  pl.BlockSpec((tm, LANES), tab),
                  pl.BlockSpec((tm, LANES), tab),
                  pl.BlockSpec((512, 512), const),
                  pl.BlockSpec((128, 128), const)],
        out_specs=[pl.BlockSpec((1, tm, 512), row),
                   pl.BlockSpec((1, tm, 512), row),
                   pl.BlockSpec((1, tm, 256), row),
                   pl.BlockSpec((1, tm, 512), row),
                   pl.BlockSpec((1, tm, 1024), row)],
        out_shape=[jax.ShapeDtypeStruct((b, s, 512), F32),
                   jax.ShapeDtypeStruct((b, s, 512), BF16),
                   jax.ShapeDtypeStruct((b, s, 256), BF16),
                   jax.ShapeDtypeStruct((b, s, 512), BF16),
                   jax.ShapeDtypeStruct((b, s, 1024), BF16)],
        compiler_params=_params(("arbitrary", "arbitrary")),
        name="inproj_ab",
    )(x, g, s1p, sh, w, qg, kg, cos, sin, mq, mk)


def _outproj_inproj_cd_body(ya_ref, yb_ref, sz0_ref, wo_ref, x_ref, gate_ref, g_ref, s1p_ref, sh_ref, wi_ref,
                            x1_ref, f_ref, q_ref, k_ref, v_ref, sz_ref):
    for rows in _row_subtiles(x_ref.shape[1], MID_SUBTILE):
        y = (jnp.concatenate([ya_ref[0, rows, :], yb_ref[0, rows, :]], axis=-1).astype(F32)
             * sz0_ref[0, rows, :].astype(F32))
        xn = x_ref[0, rows, :] + gate_ref[0] * _dot(y.astype(BF16), wo_ref[...])
        x1_ref[0, rows, :] = xn
        u = _dot(_modulate(xn, g_ref, s1p_ref, sh_ref), wi_ref[...])
        f_ref[0, rows, :] = u[:, 0:512].astype(BF16)
        q_ref[0, rows, :] = (u[:, 512:1024] * Q_SCALE).astype(BF16)
        k_ref[0, rows, :] = u[:, 1024:1536].astype(BF16)
        v_ref[0, rows, :] = _pairs_with_ones(u[:, 1536:2048]).astype(BF16)
        sz_ref[0, rows, :] = _silu(u[:, 2048:3072]).astype(BF16)


def _outproj_inproj_cd(ya, yb, sz0, w_out, x, gate, g, s1p, sh, w_in, tm):
    b, s, d = x.shape
    n = w_in.shape[1]
    row = lambda bi, i: (bi, i, 0)
    const = lambda bi, i: (0, 0)
    per_b = lambda bi, i: (bi, 0, 0)
    return pl.pallas_call(
        _outproj_inproj_cd_body,
        grid=(b, s // tm),
        in_specs=[pl.BlockSpec((1, tm, 512), row),
                  pl.BlockSpec((1, tm, 512), row),
                  pl.BlockSpec((1, tm, 1024), row),
                  pl.BlockSpec((1024, d), const, pipeline_mode=pl.Buffered(1)),
                  pl.BlockSpec((1, tm, d), row),
                  pl.BlockSpec((1, 1, d), per_b),
                  pl.BlockSpec((1, d), const),
                  pl.BlockSpec((1, 1, d), per_b),
                  pl.BlockSpec((1, 1, d), per_b),
                  pl.BlockSpec((d, n), const, pipeline_mode=pl.Buffered(1))],
        out_specs=[pl.BlockSpec((1, tm, d), row)] + [pl.BlockSpec((1, tm, 512), row)] * 3
        + [pl.BlockSpec((1, tm, 1024), row)] * 2,
        out_shape=[jax.ShapeDtypeStruct((b, s, d), F32)] + [jax.ShapeDtypeStruct((b, s, 512), BF16)] * 3
        + [jax.ShapeDtypeStruct((b, s, 1024), BF16)] * 2,
        compiler_params=_params(("arbitrary", "arbitrary")),
        name="outproj_inproj_cd",
    )(ya, yb, sz0, w_out, x, gate, g, s1p, sh, w_in)


def _split_heads(q):
    lane = lax.broadcasted_iota(jnp.int32, q.shape, 1)
    zero = jnp.zeros_like(q)
    return jnp.concatenate([jnp.where(lane < HEAD_DIM, q, zero), jnp.where(lane >= HEAD_DIM, q, zero)], axis=0)


def _with_ones(v):
    lane = lax.broadcasted_iota(jnp.int32, v.shape, 1)
    one = jnp.ones_like(v)
    return jnp.concatenate([jnp.where(lane < HEAD_DIM, v, one), jnp.where(lane < HEAD_DIM, one, v)], axis=1)


def _pairs_with_ones(v):
    return jnp.concatenate([_with_ones(v[:, c:c + LANES]) for c in range(0, v.shape[1], LANES)], axis=1)


def _pv(p, v2, rows):
    pv = _dot(p.astype(BF16), v2)
    return jnp.concatenate([pv[:rows, :LANES], pv[rows:, LANES:]], axis=0)


def _softmax_pv(q2, chunks):
    rows = q2.shape[0] // 2
    scores = [_dot_nt(q2, k) if bias is None else _dot_nt(q2, k) + bias for k, _, bias in chunks]
    m = functools.reduce(jnp.maximum, [jnp.max(s, axis=-1, keepdims=True) for s in scores])
    acc = functools.reduce(jnp.add, [_pv(jnp.exp2(s - m), v2, rows) for s, (_, v2, _) in zip(scores, chunks)])
    left = lax.broadcasted_iota(jnp.int32, (rows, LANES), 1) < HEAD_DIM
    top, bot = acc[:rows], acc[rows:]
    return jnp.where(left, top, bot) / pltpu.roll(jnp.where(left, bot, top), HEAD_DIM, 1)


def _softmax_pv_keys_major(q2, chunks):
    half = q2.shape[0] // 2
    scores = [_dot_nt(k, q2) for k, _ in chunks]
    m = acc = None
    for s, (_, v_ones) in zip(scores, chunks):
        mc = jnp.max(s, axis=0, keepdims=True)
        m_new = mc if m is None else jnp.maximum(m, mc)
        pv = _dot(v_ones.T[:V_ROWS], jnp.exp2(s - m_new).astype(BF16))
        acc = pv if m is None else jnp.exp2(m - m_new) * acc + pv
        m = m_new
    o = acc[:HEAD_DIM] / acc[HEAD_DIM:HEAD_DIM + 1]
    return jnp.concatenate([o[:, :half], o[:, half:]], axis=0).T


def _key_chunks(k_ref, v_ref, kv, chunk):
    n = k_ref.shape[1]
    kl, vl = slice(kv * LANES, (kv + 1) * LANES), slice(kv * 2 * LANES, (kv + 1) * 2 * LANES)
    return [(k_ref[0, c:c + chunk, kl], v_ref[0, c:c + chunk, vl], None) for c in range(0, n, chunk)]


def _gqa_body(n_src, chunk, q_ref, *refs):
    k_refs, v_refs, o_ref = refs[:n_src], refs[n_src:2 * n_src], refs[2 * n_src]
    pairs = q_ref.shape[2] // LANES
    kv_heads = k_refs[0].shape[2] // LANES
    for p in range(pairs):
        lanes = slice(p * LANES, (p + 1) * LANES)
        kv = p * kv_heads // pairs
        chunks = [c for k, v in zip(k_refs, v_refs) for c in _key_chunks(k, v, kv, min(chunk, k.shape[1]))]
        chunks = [(k, v2[:, :LANES]) for k, v2, _ in chunks]
        o_ref[0, :, lanes] = _softmax_pv_keys_major(_split_heads(q_ref[0, :, lanes]), chunks).astype(o_ref.dtype)


def _gqa_attention(q, ks, vs, tq, pairs, chunk=512):
    b, lq, w = q.shape
    n_src = len(ks)
    n_pairs = w // LANES
    kv_heads = ks[0].shape[2] // LANES
    kv_blk = max(1, pairs * kv_heads // n_pairs)
    kv_idx = lambda bi, i, j: (bi, 0, j * pairs * kv_heads // n_pairs // kv_blk)
    k_specs = [pl.BlockSpec((1, k.shape[1], kv_blk * LANES), kv_idx) for k in ks]
    v_specs = [pl.BlockSpec((1, v.shape[1], kv_blk * 2 * LANES), kv_idx) for v in vs]
    return pl.pallas_call(
        functools.partial(_gqa_body, n_src, chunk),
        grid=(b, lq // tq, n_pairs // pairs),
        in_specs=[pl.BlockSpec((1, tq, pairs * LANES), lambda bi, i, j: (bi, i, j))] + k_specs + v_specs,
        out_specs=pl.BlockSpec((1, tq, pairs * LANES), lambda bi, i, j: (bi, i, j)),
        out_shape=jax.ShapeDtypeStruct((b, lq, w), BF16),
        compiler_params=_params(("arbitrary", "arbitrary", "arbitrary")),
        name=f"gqa_attention_{n_src}src",
    )(q, *ks, *vs)


def _conv_core(rows, pad_ref, shift_ref, w_ref, cb_ref, lg_ref, lb_ref, store):
    span = shift_ref.shape[1]
    for r in range(1, SUBLANES):
        shift_ref[r] = pad_ref[r:r + span, :]
    sub = 32
    base = HALO - CONV_WIDTH // 2
    for r0 in range(0, rows, sub):
        acc = jnp.zeros((sub, pad_ref.shape[1]), F32) + cb_ref[...]
        for k in range(CONV_WIDTH):
            a, r = divmod(base + k, SUBLANES)
            lo = a * SUBLANES + r0
            tap = pad_ref[lo:lo + sub, :] if r == 0 else shift_ref[r, lo:lo + sub, :]
            acc = acc + tap * jnp.concatenate([w_ref[k]] * (sub // SUBLANES), axis=0)
        mu = jnp.mean(acc, axis=-1, keepdims=True)
        xc = acc - mu
        y = xc * lax.rsqrt(jnp.mean(xc * xc, axis=-1, keepdims=True) + EPS) * lg_ref[...] + lb_ref[...]
        store(r0, sub, _silu(y))


def _conv_body(rows, x_ref, prev_ref, next_ref, w_ref, cb_ref, lg_ref, lb_ref, o_ref, pad_ref, shift_ref):
    i = pl.program_id(1)
    last = pl.num_programs(1) - 1
    pad_ref[0:HALO, :] = jnp.where(i > 0, prev_ref[0], 0.0)
    pad_ref[HALO:HALO + rows, :] = x_ref[0]
    pad_ref[HALO + rows:HALO + rows + HALO, :] = jnp.where(i < last, next_ref[0], 0.0)

    def store(r0, sub, y):
        o_ref[0, r0:r0 + sub, :] = y.astype(o_ref.dtype)

    _conv_core(rows, pad_ref, shift_ref, w_ref, cb_ref, lg_ref, lb_ref, store)


def _conformer_conv(u, conv_w, conv_b, ln_g, ln_b, rows):
    b, l, c = u.shape
    hb = rows // HALO
    n_halo = l // HALO
    return pl.pallas_call(
        functools.partial(_conv_body, rows),
        grid=(b, l // rows),
        in_specs=[pl.BlockSpec((1, rows, c), lambda bi, i: (bi, i, 0)),
                  pl.BlockSpec((1, HALO, c), lambda bi, i: (bi, jnp.maximum(i * hb - 1, 0), 0)),
                  pl.BlockSpec((1, HALO, c), lambda bi, i: (bi, jnp.minimum((i + 1) * hb, n_halo - 1), 0)),
                  pl.BlockSpec((CONV_WIDTH, SUBLANES, c), lambda bi, i: (0, 0, 0)),
                  pl.BlockSpec((1, c), lambda bi, i: (0, 0)),
                  pl.BlockSpec((1, c), lambda bi, i: (0, 0)),
                  pl.BlockSpec((1, c), lambda bi, i: (0, 0))],
        out_specs=pl.BlockSpec((1, rows, c), lambda bi, i: (bi, i, 0)),
        out_shape=jax.ShapeDtypeStruct((b, l, c), BF16),
        scratch_shapes=[pltpu.VMEM((rows + 2 * HALO, c), F32),
                        pltpu.VMEM((SUBLANES, rows + 2 * HALO - SUBLANES, c), F32)],
        compiler_params=_params(("arbitrary", "arbitrary")),
        name="conformer_conv",
    )(u, u, u, jnp.broadcast_to(conv_w[:, None, :], (CONV_WIDTH, SUBLANES, c)),
      conv_b.reshape(1, c), ln_g.reshape(1, c), ln_b.reshape(1, c))


def _ctx_layer_body(x_ref, g0_ref, s1p0_ref, sh0_ref, wi0_ref, qg_ref, kg_ref, mq_ref, mk_ref, cw_ref, cb_ref,
                    lg_ref, lb_ref, wo0_ref, gate0_ref, g1_ref, s1p1_ref, sh1_ref, wkv1_ref,
                    ck_ref, cv_ref, kc_ref, vc_ref, pad_ref, shift_ref, ya_ref):
    rows = x_ref.shape[1]
    x = x_ref[0]
    u = _dot(_modulate(x, g0_ref, s1p0_ref, sh0_ref), wi0_ref[...])
    glu, q, k_dup, v_dup, sz = _ab_epilogue(u, mq_ref, mk_ref, qg_ref, kg_ref, None)
    ck_ref[0] = k_dup
    cv_ref[0] = v_dup
    pairs, kv_heads = q.shape[1] // LANES, k_dup.shape[1] // LANES
    yb = []
    for p in range(pairs):
        kv = p * kv_heads // pairs
        keys = [(k_dup[:, kv * LANES:(kv + 1) * LANES], v_dup[:, kv * 2 * LANES:(kv + 1) * 2 * LANES], None)]
        yb.append(_softmax_pv(_split_heads(q[:, p * LANES:(p + 1) * LANES]), keys))
    zeros = jnp.zeros((HALO, glu.shape[1]), F32)
    pad_ref[0:HALO, :] = zeros
    pad_ref[HALO:HALO + rows, :] = glu
    pad_ref[HALO + rows:HALO + rows + HALO, :] = zeros

    def store(r0, sub, y):
        ya_ref[r0:r0 + sub, :] = y

    _conv_core(rows, pad_ref, shift_ref, cw_ref, cb_ref, lg_ref, lb_ref, store)
    y = jnp.concatenate([ya_ref[...]] + yb, axis=-1) * sz.astype(F32)
    x1 = x + gate0_ref[0] * _dot(y.astype(BF16), wo0_ref[...])
    u1 = _dot(_modulate(x1, g1_ref, s1p1_ref, sh1_ref), wkv1_ref[...])
    kc_ref[0] = u1[:, 0:512].astype(BF16)
    vc_ref[0] = _pairs_with_ones(u1[:, 512:1024]).astype(BF16)


def _ctx_layer(ctx, g0, s1p0, sh0, w_in0, qg, kg, mq, mk, conv_w, conv_b, ln_g, ln_b, w_out0, gate0,
               g1, s1p1, sh1, w_in1, kv_col_block):
    b, l, d = ctx.shape
    c = conv_w.shape[1]
    row = lambda bi: (bi, 0, 0)
    const2 = lambda bi: (0, 0)
    vec = pl.BlockSpec((1, d), const2)
    per_b = pl.BlockSpec((1, 1, d), row)
    cvec = pl.BlockSpec((1, c), const2)
    return pl.pallas_call(
        _ctx_layer_body,
        grid=(b,),
        in_specs=[pl.BlockSpec((1, l, d), row), vec, per_b, per_b,
                  pl.BlockSpec(w_in0.shape, const2),
                  pl.BlockSpec((1, 512), const2), pl.BlockSpec((1, 128), const2),
                  pl.BlockSpec((512, 512), const2), pl.BlockSpec((128, 128), const2),
                  pl.BlockSpec((CONV_WIDTH, SUBLANES, c), lambda bi: (0, 0, 0)), cvec, cvec, cvec,
                  pl.BlockSpec(w_out0.shape, const2), per_b,
                  vec, per_b, per_b,
                  pl.BlockSpec((d, 1024), lambda bi: (0, kv_col_block))],
        out_specs=[pl.BlockSpec((1, l, 256), row), pl.BlockSpec((1, l, 512), row),
                   pl.BlockSpec((1, l, 512), row), pl.BlockSpec((1, l, 1024), row)],
        out_shape=[jax.ShapeDtypeStruct((b, l, 256), BF16), jax.ShapeDtypeStruct((b, l, 512), BF16),
                   jax.ShapeDtypeStruct((b, l, 512), BF16), jax.ShapeDtypeStruct((b, l, 1024), BF16)],
        scratch_shapes=[pltpu.VMEM((l + 2 * HALO, c), F32),
                        pltpu.VMEM((SUBLANES, l + 2 * HALO - SUBLANES, c), F32),
                        pltpu.VMEM((l, c), F32)],
        compiler_params=_params(("arbitrary",)),
        name="ctx_layer",
    )(ctx, g0, s1p0, sh0, w_in0, qg, kg, mq, mk,
      jnp.broadcast_to(conv_w[:, None, :], (CONV_WIDTH, SUBLANES, c)), conv_b.reshape(1, c), ln_g.reshape(1, c),
      ln_b.reshape(1, c), w_out0, gate0, g1, s1p1, sh1, w_in1)


def _outproj_final_body(ya_ref, yb_ref, sz_ref, w_ref, x_ref, gate_ref, perm_ref, fg_ref, o_ref):
    sub = perm_ref.shape[0]
    per = sub // DFT_RADIX
    for h in range(x_ref.shape[1] // sub):
        rows = slice(h * sub, (h + 1) * sub)
        stacked = jnp.concatenate([ya_ref[0, j, h * per:(h + 1) * per, :] for j in range(DFT_RADIX)], axis=0)
        ya = _dot(perm_ref[...], stacked)
        y = jnp.concatenate([ya, yb_ref[0, rows, :].astype(F32)], axis=-1) * sz_ref[0, rows, :].astype(F32)
        xn = x_ref[0, rows, :] + gate_ref[0] * _dot(y.astype(BF16), w_ref[...])
        o_ref[0, rows, :] = xn * lax.rsqrt(jnp.mean(xn * xn, axis=-1, keepdims=True) + EPS) * fg_ref[...]


def _outproj_final(ya, yb, sz, w, x, gate, final_g, tm):
    b, s, d = x.shape
    row = lambda bi, i: (bi, i, 0)
    const = lambda bi, i: (0, 0)
    return pl.pallas_call(
        _outproj_final_body,
        grid=(b, s // tm),
        in_specs=[pl.BlockSpec((1, DFT_RADIX, tm // DFT_RADIX, 512), lambda bi, i: (bi, 0, i, 0)),
                  pl.BlockSpec((1, tm, 512), row),
                  pl.BlockSpec((1, tm, 1024), row),
                  pl.BlockSpec((1024, d), const),
                  pl.BlockSpec((1, tm, d), row),
                  pl.BlockSpec((1, 1, d), lambda bi, i: (bi, 0, 0)),
                  pl.BlockSpec((OUT_PROJ_SUBTILE, OUT_PROJ_SUBTILE), const),
                  pl.BlockSpec((1, d), const)],
        out_specs=pl.BlockSpec((1, tm, d), row),
        out_shape=jax.ShapeDtypeStruct((b, s, d), F32),
        compiler_params=_params(("arbitrary", "arbitrary")),
        name="outproj_final",
    )(ya, yb, sz, w, x, gate, jnp.asarray(_class_row_permutation(OUT_PROJ_SUBTILE)).astype(BF16), final_g)


def _fourier_w_body(cc_ref, sc_ref, w_ref, o_ref):
    for g in range(w_ref.shape[0]):
        wc = jnp.dot(cc_ref[...], w_ref[g], preferred_element_type=F32, precision=HIGHEST)
        ws = jnp.dot(sc_ref[...], w_ref[g], preferred_element_type=F32, precision=HIGHEST)
        o_ref[g] = jnp.concatenate([wc, ws], axis=-1).astype(o_ref.dtype)


def _fourier_weights(cc, sc, w):
    g, c, _ = w.shape
    return pl.pallas_call(
        _fourier_w_body,
        out_shape=jax.ShapeDtypeStruct((g, c, 2 * c), BF16),
        name="fourier_weights",
    )(cc, sc, w)


def _fourier_tables(seq):
    quarter = seq // DFT_RADIX
    m = np.arange(quarter)
    tabs = []
    for j in range(DFT_RADIX):
        ang = 2.0 * np.pi * (((DFT_RADIX * m + j)[:, None] * m[None, :]) % seq) / seq
        tabs.append(np.concatenate([np.cos(ang), -np.sin(ang)], axis=1) / math.sqrt(seq))
    return np.stack(tabs).astype(np.float32)


def _class_row_permutation(rows):
    per = rows // DFT_RADIX
    p = np.zeros((rows, rows), np.float32)
    for j in range(DFT_RADIX):
        p[DFT_RADIX * np.arange(per) + j, j * per + np.arange(per)] = 1.0
    return p


def _fourier_body(f_ref, w1_ref, tab_ref, o_ref, yc_ref, ys_ref):
    groups, c, _ = w1_ref.shape
    quarter = f_ref.shape[1] // DFT_RADIX
    for g in range(groups):
        yg = _dot(f_ref[0, :, g * c:(g + 1) * c], w1_ref[g])
        yc_ref[:, g * c:(g + 1) * c] = yg[:, :c]
        ys_ref[:, g * c:(g + 1) * c] = yg[:, c:]
    for j in range(DFT_RADIX):
        pc, ps = [], []
        for q in range(DFT_RADIX):
            rows = slice(q * quarter, (q + 1) * quarter)
            a = (j * q) % 4
            if a == 0:
                pc.append((1, yc_ref, rows)), ps.append((1, ys_ref, rows))
            elif a == 1:
                pc.append((-1, ys_ref, rows)), ps.append((1, yc_ref, rows))
            elif a == 2:
                pc.append((-1, yc_ref, rows)), ps.append((-1, ys_ref, rows))
            else:
                pc.append((1, ys_ref, rows)), ps.append((-1, yc_ref, rows))
        folded = []
        for terms in (pc, ps):
            terms = sorted(terms, key=lambda t: -t[0])
            acc = terms[0][1][terms[0][2], :]
            for sign, ref, rows in terms[1:]:
                acc = acc + ref[rows, :] if sign > 0 else acc - ref[rows, :]
            folded.append(acc.astype(BF16))
        o_ref[0, j] = _dot(tab_ref[j], jnp.concatenate(folded, axis=0)).astype(o_ref.dtype)


def _fourier_mix(f, w1, tab):
    b, s, w = f.shape
    quarter = s // DFT_RADIX
    return pl.pallas_call(
        _fourier_body,
        grid=(b,),
        in_specs=[pl.BlockSpec((1, s, w), lambda bi: (bi, 0, 0)),
                  pl.BlockSpec(w1.shape, lambda bi: (0, 0, 0)),
                  pl.BlockSpec(tab.shape, lambda bi: (0, 0, 0))],
        out_specs=pl.BlockSpec((1, DFT_RADIX, quarter, w), lambda bi: (bi, 0, 0, 0)),
        out_shape=jax.ShapeDtypeStruct((b, DFT_RADIX, quarter, w), BF16),
        scratch_shapes=[pltpu.VMEM((s, w), F32), pltpu.VMEM((s, w), F32)],
        compiler_params=_params(("arbitrary",)),
        name="fourier_mix",
    )(f, w1, tab)


def _na_tile_geometry(rows_total):
    last_r0 = rows_total - NA_TILE_ROWS
    return ((0, 0), (2 * NA_TILE_ROWS, 2 * NA_TILE_ROWS - NA_ROWS // 2), (last_r0, rows_total - NA_WIN_ROWS))


def _na_bias_body(rows_total, rpb_ref, o_ref, t2_ref):
    h = pl.program_id(0)
    wq = lax.broadcasted_iota(jnp.int32, (GRID_W, LANES), 0)
    wk = lax.broadcasted_iota(jnp.int32, (GRID_W, LANES), 1) & (GRID_W - 1)
    cs = jnp.clip(wq - NA_COLS // 2, 0, GRID_W - NA_COLS)
    col_ok = (wk >= cs) & (wk < cs + NA_COLS)
    dc = wk - wq + (NA_COLS - 1)
    n_dr, n_dc = 2 * NA_ROWS - 1, 2 * NA_COLS - 1
    neg = jnp.full((GRID_W, LANES), NEG_INF, F32)
    for dr in range(n_dr):
        t2_ref[dr] = neg
    for c in range(n_dc):
        hit = col_ok & (dc == c)
        for dr in range(n_dr):
            t2_ref[dr] = jnp.where(hit, rpb_ref[h * (n_dr * n_dc) + dr * n_dc + c] * LOG2E, t2_ref[dr])
    lane = lax.broadcasted_iota(jnp.int32, (GRID_W, LANES), 1)
    left = lane < GRID_W
    for ty, (r0, ws) in enumerate(_na_tile_geometry(rows_total)):
        for i in range(NA_TILE_ROWS):
            r = r0 + i
            rs = min(max(r - NA_ROWS // 2, 0), rows_total - NA_ROWS)
            for jp in range(NA_WIN_ROWS // 2):
                halves = []
                for kr in (ws + 2 * jp, ws + 2 * jp + 1):
                    halves.append(t2_ref[kr - r + NA_ROWS - 1] if rs <= kr < rs + NA_ROWS else neg)
                o_ref[0, ty, i * GRID_W:(i + 1) * GRID_W, jp * LANES:(jp + 1) * LANES] = jnp.where(left, halves[0], halves[1])


def _na_bias_tables(rpb, rows_total):
    h = rpb.shape[0]
    q_rows, k_cols = NA_TILE_ROWS * GRID_W, NA_WIN_ROWS * GRID_W
    return pl.pallas_call(
        functools.partial(_na_bias_body, rows_total),
        grid=(h,),
        in_specs=[pl.BlockSpec(memory_space=pltpu.SMEM)],
        out_specs=pl.BlockSpec((1, 3, q_rows, k_cols), lambda hi: (hi, 0, 0, 0)),
        out_shape=jax.ShapeDtypeStruct((h, 3, q_rows, k_cols), F32),
        scratch_shapes=[pltpu.VMEM((2 * NA_ROWS - 1, GRID_W, LANES), F32)],
        compiler_params=_params(("arbitrary",)),
        name="na_bias_tables",
    )(rpb.reshape(-1))


def _na_body(rows_total, q_ref, k_ref, v_ref, kc_ref, vc_ref, bias_ref, o_ref):
    t = pl.program_id(2)
    n_t = pl.num_programs(2)
    ty = jnp.where(t == 0, 0, jnp.where(t == n_t - 1, 2, 1))
    ws = jnp.clip(t * NA_TILE_ROWS - NA_ROWS // 2, 0, rows_total - NA_WIN_ROWS)
    off = pl.multiple_of(ws * GRID_W, GRID_W)
    n_win = NA_WIN_ROWS * GRID_W
    tq = q_ref.shape[1]
    part = tq // NA_QUERY_SPLIT
    for bb in range(q_ref.shape[0]):
        for qs in range(NA_QUERY_SPLIT):
            rows = slice(qs * part, (qs + 1) * part)
            bias = jnp.concatenate([bias_ref[0, ty, rows, :], bias_ref[1, ty, rows, :]], axis=0)
            chunks = [(kc_ref[bb], vc_ref[bb], None),
                      (k_ref[bb, pl.ds(off, n_win), :], v_ref[bb, pl.ds(off, n_win), :], bias)]
            o = _softmax_pv(_split_heads(q_ref[bb, rows, :]), chunks)
            o_ref[bb, rows, :] = o.astype(o_ref.dtype)


def _neighborhood_attention(q, k, v, kc, vc, bias):
    b, s, w = q.shape
    l = kc.shape[1]
    rows_total = s // GRID_W
    tq = NA_TILE_ROWS * GRID_W
    nb = NA_BATCH_PER_STEP
    full = lambda j, bi, t: (bi, 0, j)
    return pl.pallas_call(
        functools.partial(_na_body, rows_total),
        grid=(w // LANES, b // nb, s // tq),
        in_specs=[pl.BlockSpec((nb, tq, LANES), lambda j, bi, t: (bi, t, j)),
                  pl.BlockSpec((nb, s, LANES), full),
                  pl.BlockSpec((nb, s, 2 * LANES), full),
                  pl.BlockSpec((nb, l, LANES), full),
                  pl.BlockSpec((nb, l, 2 * LANES), full),
                  pl.BlockSpec((2,) + bias.shape[1:], lambda j, bi, t: (j, 0, 0, 0))],
        out_specs=pl.BlockSpec((nb, tq, LANES), lambda j, bi, t: (bi, t, j)),
        out_shape=jax.ShapeDtypeStruct((b, s, w), BF16),
        compiler_params=_params(("arbitrary", "arbitrary", "arbitrary")),
        name="neighborhood_attention",
    )(q, k, v, kc, vc, bias)


def _modulation(m, rows, batch):
    d = m.shape[1] // 3
    pick = lambda lo: jnp.broadcast_to(m[rows, lo:lo + d].reshape(-1, 1, d), (batch, 1, d))
    return pick(0), 1.0 + pick(d), pick(2 * d)


def kernel(x, c, ctx, c_ctx, ab_w_ada, ab_b_ada, ab_norm_g, ab_w_in, ab_conv_w, ab_conv_b, ab_ln_g, ab_ln_b, ab_q_norm_g, ab_k_norm_g, ab_w_out, cd_w_ada, cd_b_ada, cd_norm_g, cd_w_in, cd_w_fourier, cd_rpb, cd_w_out, final_norm_g):
    b, s, d = x.shape
    l = ctx.shape[1]
    assert d == D_MODEL and s % (NA_TILE_ROWS * GRID_W) == 0 and ab_w_ada.shape[0] == 1 and cd_w_ada.shape[0] == 1

    cond = jnp.zeros((16, d), F32).at[:b].set(c).at[b].set(c_ctx)
    lat = slice(0, b)
    cx = slice(b, b + 1)

    m0 = _adaln(cond, ab_w_ada[0], ab_b_ada[0])
    m1 = _adaln(cond, cd_w_ada[0], cd_b_ada[0])
    sh, s1p, gate = _modulation(m0, lat, b)
    csh, cs1p, cgate = _modulation(m0, cx, b)
    sh1, s1p1, gate1 = _modulation(m1, lat, b)
    csh1, cs1p1, _ = _modulation(m1, cx, b)
    ng1 = cd_norm_g[0].reshape(1, d)
    w_in1 = cd_w_in[0].astype(BF16)

    w_in0 = ab_w_in[0].astype(BF16)
    w_out0 = ab_w_out[0].astype(BF16)

    cos, sin = _rope_tables(s)
    cos, sin = jnp.asarray(cos), jnp.asarray(sin)
    mq = jnp.asarray(_block_ones(512)).astype(BF16)
    mk = jnp.asarray(_block_ones(128)).astype(BF16)
    qg = jnp.tile(ab_q_norm_g[0], 512 // HEAD_DIM).reshape(1, 512)
    kg = jnp.tile(ab_k_norm_g[0], 128 // HEAD_DIM).reshape(1, 128)
    ng0 = ab_norm_g[0].reshape(1, d)

    ck, cv, kc, vc = _ctx_layer(ctx, ng0, cs1p, csh, w_in0, qg, kg, mq, mk, ab_conv_w[0], ab_conv_b[0], ab_ln_g[0],
                                ab_ln_b[0], w_out0, cgate, ng1, cs1p1, csh1, w_in1, 1)

    glu, q, k, v, sz = _inproj_ab(x, ng0, s1p, sh, w_in0, qg, kg, cos, sin, mq, mk, 1024)
    y_b = _gqa_attention(q, [ck, k], [cv, v], 1024, 1, chunk=128)
    y_a = _conformer_conv(glu, ab_conv_w[0], ab_conv_b[0], ab_ln_g[0], ab_ln_b[0], 1024)

    x1, f, q, k, v, sz = _outproj_inproj_cd(y_a, y_b, sz, w_out0, x, gate, ng1, s1p1, sh1, w_in1, 1024)
    gate = gate1

    c_c, s_c = _dft_tables(cd_w_fourier.shape[-1])
    w1 = _fourier_weights(jnp.asarray(c_c), jnp.asarray(s_c), cd_w_fourier[0])
    y_c = _fourier_mix(f, w1, jnp.asarray(_fourier_tables(s)).astype(BF16))

    bias = _na_bias_tables(cd_rpb[0], s // GRID_W)
    y_d = _neighborhood_attention(q, k, v, kc, vc, bias)

    return _outproj_final(y_c, y_d, sz, cd_w_out[0].astype(BF16), x1, gate, final_norm_g.reshape(1, d), 1024)
```

```python
import functools
import math

import numpy as np
import jax
import jax.numpy as jnp
from jax import lax
from jax.experimental import pallas as pl
from jax.experimental.pallas import tpu as pltpu

F32 = jnp.float32
BF16 = jnp.bfloat16
HIGHEST = lax.Precision.HIGHEST

D_MODEL = 1024
HEAD_DIM = 64
GRID_W = 64
CONV_WIDTH = 31
NA_ROWS = 8
NA_COLS = 16
ROPE_THETA = 10000.0
EPS = 1e-6
NEG_INF = -1e30
LOG2E = math.log2(math.e)
Q_SCALE = HEAD_DIM ** -0.5 * LOG2E
LANES = 128
SUBLANES = 8
HALO = 16
VMEM_LIMIT = 48 * 1024 * 1024

NA_TILE_ROWS = 4
NA_WIN_ROWS = NA_TILE_ROWS + NA_ROWS
NA_BATCH_PER_STEP = 8
NA_QUERY_SPLIT = 2
IN_PROJ_SUBTILE = 256
MID_SUBTILE = 1024
OUT_PROJ_SUBTILE = 512
DFT_RADIX = 4
V_ROWS = 80


def _params(sem):
    return pltpu.CompilerParams(dimension_semantics=sem, vmem_limit_bytes=VMEM_LIMIT)


def _silu(t):
    return t * jax.nn.sigmoid(t)


def _dot(a, b):
    return jnp.dot(a, b, preferred_element_type=F32)


def _dot_nt(a, b):
    return lax.dot_general(a, b, (((1,), (1,)), ((), ())), preferred_element_type=F32)


def _rope_tables(seq):
    t = np.arange(seq)
    row = (t // GRID_W).astype(np.float64)
    col = (t % GRID_W).astype(np.float64)
    nf = HEAD_DIM // 4
    inv = np.float32(ROPE_THETA) ** (-np.arange(nf, dtype=np.float32) / nf)
    ang = np.concatenate([row[:, None] * inv, col[:, None] * inv], axis=-1)
    cos = np.repeat(np.cos(ang), 2, axis=-1)
    sin = np.repeat(np.sin(ang), 2, axis=-1)
    sign = np.tile(np.array([-1.0, 1.0]), HEAD_DIM // 2)
    cos = np.tile(cos, (1, LANES // HEAD_DIM))
    sin = np.tile(sin * sign, (1, LANES // HEAD_DIM))
    return cos.astype(np.float32), sin.astype(np.float32)


def _dft_tables(n):
    k = np.arange(n)
    ang = 2.0 * np.pi * ((k[:, None] * k[None, :]) % n) / n
    s = 1.0 / math.sqrt(n)
    return (np.cos(ang) * s).astype(np.float32), (np.sin(ang) * s).astype(np.float32)


def _block_ones(width):
    h = np.arange(width) // HEAD_DIM
    return (h[:, None] == h[None, :]).astype(np.float32)


def _adaln_body(c_ref, w_ref, b_ref, o_ref):
    a = _silu(c_ref[...]).astype(BF16)
    o_ref[...] = _dot(a, w_ref[...].astype(BF16)) + b_ref[...]


def _adaln(cond, w, b):
    r, d = cond.shape
    n = w.shape[1]
    tn = 1024
    return pl.pallas_call(
        _adaln_body,
        grid=(n // tn,),
        in_specs=[pl.BlockSpec((r, d), lambda j: (0, 0)),
                  pl.BlockSpec((d, tn), lambda j: (0, j)),
                  pl.BlockSpec((1, tn), lambda j: (0, j))],
        out_specs=pl.BlockSpec((r, tn), lambda j: (0, j)),
        out_shape=jax.ShapeDtypeStruct((r, n), F32),
        compiler_params=_params(("arbitrary",)),
        name="adaln",
    )(cond, w, b.reshape(1, n))


def _head_rms(t, ones_ref, gain):
    ms = _dot((t * t).astype(BF16), ones_ref[...]) * (1.0 / HEAD_DIM)
    return t * lax.rsqrt(ms + EPS) * gain


def _rope(t, cos, sin_signed):
    rows = t.shape[0]
    lane = lax.broadcasted_iota(jnp.int32, (rows, LANES), 1)
    even = (lane & 1) == 0
    outs = []
    for c in range(t.shape[1] // LANES):
        xc = t[:, c * LANES:(c + 1) * LANES]
        swapped = jnp.where(even, pltpu.roll(xc, LANES - 1, 1), pltpu.roll(xc, 1, 1))
        outs.append(xc * cos + swapped * sin_signed)
    return outs[0] if len(outs) == 1 else jnp.concatenate(outs, axis=1)


def _modulate(x, g_ref, s1p_ref, sh_ref):
    y = x * lax.rsqrt(jnp.mean(x * x, axis=-1, keepdims=True) + EPS) * g_ref[...]
    return (y * s1p_ref[0] + sh_ref[0]).astype(BF16)


def _row_subtiles(tm, sub):
    sub = min(tm, sub)
    return [slice(r, r + sub) for r in range(0, tm, sub)]


def _inproj_ab_body(x_ref, g_ref, s1p_ref, sh_ref, w_ref, qg_ref, kg_ref, cos_ref, sin_ref, mq_ref, mk_ref,
                    glu_ref, q_ref, k_ref, v_ref, sz_ref):
    for rows in _row_subtiles(x_ref.shape[1], IN_PROJ_SUBTILE):
        u = _dot(_modulate(x_ref[0, rows, :], g_ref, s1p_ref, sh_ref), w_ref[...])
        glu, q, k, v, sz = _ab_epilogue(u, mq_ref, mk_ref, qg_ref, kg_ref, (cos_ref[rows, :], sin_ref[rows, :]))
        glu_ref[0, rows, :] = glu
        q_ref[0, rows, :] = q
        k_ref[0, rows, :] = k
        v_ref[0, rows, :] = v
        sz_ref[0, rows, :] = sz


def _ab_epilogue(u, mq_ref, mk_ref, qg_ref, kg_ref, rope_tabs):
    glu = u[:, 0:512] * jax.nn.sigmoid(u[:, 512:1024])
    q = _head_rms(u[:, 1024:1536], mq_ref, qg_ref[...])
    k = _head_rms(u[:, 1536:1664], mk_ref, kg_ref[...])
    if rope_tabs is not None:
        q = _rope(q, *rope_tabs)
        k = _rope(k, *rope_tabs)
    v = u[:, 1664:1792]
    k_sw, v_sw = pltpu.roll(k, HEAD_DIM, 1), pltpu.roll(v, HEAD_DIM, 1)
    left = lax.broadcasted_iota(jnp.int32, k.shape, 1) < HEAD_DIM
    one = jnp.ones_like(v)
    k_dup = jnp.concatenate([jnp.where(left, k, k_sw), jnp.where(left, k_sw, k)], axis=1)
    v_dup = jnp.concatenate([jnp.where(left, v, one), jnp.where(left, one, v_sw),
                             jnp.where(left, v_sw, one), jnp.where(left, one, v)], axis=1)
    return (glu, (q * Q_SCALE).astype(BF16), k_dup.astype(BF16), v_dup.astype(BF16),
            _silu(u[:, 1792:2816]).astype(BF16))


def _inproj_ab(x, g, s1p, sh, w, qg, kg, cos, sin, mq, mk, tm):
    b, s, d = x.shape
    n = w.shape[1]
    row = lambda bi, i: (bi, i, 0)
    const = lambda bi, i: (0, 0)
    per_b = lambda bi, i: (bi, 0, 0)
    tab = lambda bi, i: (i, 0)
    return pl.pallas_call(
        _inproj_ab_body,
        grid=(b, s // tm),
        in_specs=[pl.BlockSpec((1, tm, d), row),
                  pl.BlockSpec((1, d), const),
                  pl.BlockSpec((1, 1, d), per_b),
                  pl.BlockSpec((1, 1, d), per_b),
                  pl.BlockSpec((d, n), const),
                  pl.BlockSpec((1, 512), const),
                  pl.BlockSpec((1, 128), const),
                  pl.BlockSpec((tm, LANES), tab),
                  pl.BlockSpec((tm, LANES), tab),
                  pl.BlockSpec((512, 512), const),
                  pl.BlockSpec((128, 128), const)],
        out_specs=[pl.BlockSpec((1, tm, 512), row),
                   pl.BlockSpec((1, tm, 512), row),
                   pl.BlockSpec((1, tm, 256), row),
                   pl.BlockSpec((1, tm, 512), row),
                   pl.BlockSpec((1, tm, 1024), row)],
        out_shape=[jax.ShapeDtypeStruct((b, s, 512), F32),
                   jax.ShapeDtypeStruct((b, s, 512), BF16),
                   jax.ShapeDtypeStruct((b, s, 256), BF16),
                   jax.ShapeDtypeStruct((b, s, 512), BF16),
                   jax.ShapeDtypeStruct((b, s, 1024), BF16)],
        compiler_params=_params(("arbitrary", "arbitrary")),
        name="inproj_ab",
    )(x, g, s1p, sh, w, qg, kg, cos, sin, mq, mk)


def _outproj_inproj_cd_body(ya_ref, yb_ref, sz0_ref, wo_ref, x_ref, gate_ref, g_ref, s1p_ref, sh_ref, wi_ref,
                            x1_ref, f_ref, q_ref, k_ref, v_ref, sz_ref):
    for rows in _row_subtiles(x_ref.shape[1], MID_SUBTILE):
        y = (jnp.concatenate([ya_ref[0, rows, :], yb_ref[0, rows, :]], axis=-1).astype(F32)
             * sz0_ref[0, rows, :].astype(F32))
        xn = x_ref[0, rows, :] + gate_ref[0] * _dot(y.astype(BF16), wo_ref[...])
        x1_ref[0, rows, :] = xn
        u = _dot(_modulate(xn, g_ref, s1p_ref, sh_ref), wi_ref[...])
        f_ref[0, rows, :] = u[:, 0:512].astype(BF16)
        q_ref[0, rows, :] = (u[:, 512:1024] * Q_SCALE).astype(BF16)
        k_ref[0, rows, :] = u[:, 1024:1536].astype(BF16)
        v_ref[0, rows, :] = _pairs_with_ones(u[:, 1536:2048]).astype(BF16)
        sz_ref[0, rows, :] = _silu(u[:, 2048:3072]).astype(BF16)


def _outproj_inproj_cd(ya, yb, sz0, w_out, x, gate, g, s1p, sh, w_in, tm):
    b, s, d = x.shape
    n = w_in.shape[1]
    row = lambda bi, i: (bi, i, 0)
    const = lambda bi, i: (0, 0)
    per_b = lambda bi, i: (bi, 0, 0)
    return pl.pallas_call(
        _outproj_inproj_cd_body,
        grid=(b, s // tm),
        in_specs=[pl.BlockSpec((1, tm, 512), row),
                  pl.BlockSpec((1, tm, 512), row),
                  pl.BlockSpec((1, tm, 1024), row),
                  pl.BlockSpec((1024, d), const, pipeline_mode=pl.Buffered(1)),
                  pl.BlockSpec((1, tm, d), row),
                  pl.BlockSpec((1, 1, d), per_b),
                  pl.BlockSpec((1, d), const),
                  pl.BlockSpec((1, 1, d), per_b),
                  pl.BlockSpec((1, 1, d), per_b),
                  pl.BlockSpec((d, n), const, pipeline_mode=pl.Buffered(1))],
        out_specs=[pl.BlockSpec((1, tm, d), row)] + [pl.BlockSpec((1, tm, 512), row)] * 3
        + [pl.BlockSpec((1, tm, 1024), row)] * 2,
        out_shape=[jax.ShapeDtypeStruct((b, s, d), F32)] + [jax.ShapeDtypeStruct((b, s, 512), BF16)] * 3
        + [jax.ShapeDtypeStruct((b, s, 1024), BF16)] * 2,
        compiler_params=_params(("arbitrary", "arbitrary")),
        name="outproj_inproj_cd",
    )(ya, yb, sz0, w_out, x, gate, g, s1p, sh, w_in)


def _split_heads(q):
    lane = lax.broadcasted_iota(jnp.int32, q.shape, 1)
    zero = jnp.zeros_like(q)
    return jnp.concatenate([jnp.where(lane < HEAD_DIM, q, zero), jnp.where(lane >= HEAD_DIM, q, zero)], axis=0)


def _with_ones(v):
    lane = lax.broadcasted_iota(jnp.int32, v.shape, 1)
    one = jnp.ones_like(v)
    return jnp.concatenate([jnp.where(lane < HEAD_DIM, v, one), jnp.where(lane < HEAD_DIM, one, v)], axis=1)


def _pairs_with_ones(v):
    return jnp.concatenate([_with_ones(v[:, c:c + LANES]) for c in range(0, v.shape[1], LANES)], axis=1)


def _pv(p, v2, rows):
    pv = _dot(p.astype(BF16), v2)
    return jnp.concatenate([pv[:rows, :LANES], pv[rows:, LANES:]], axis=0)


def _softmax_pv(q2, chunks):
    rows = q2.shape[0] // 2
    scores = [_dot_nt(q2, k) if bias is None else _dot_nt(q2, k) + bias for k, _, bias in chunks]
    m = functools.reduce(jnp.maximum, [jnp.max(s, axis=-1, keepdims=True) for s in scores])
    acc = functools.reduce(jnp.add, [_pv(jnp.exp2(s - m), v2, rows) for s, (_, v2, _) in zip(scores, chunks)])
    left = lax.broadcasted_iota(jnp.int32, (rows, LANES), 1) < HEAD_DIM
    top, bot = acc[:rows], acc[rows:]
    return jnp.where(left, top, bot) / pltpu.roll(jnp.where(left, bot, top), HEAD_DIM, 1)


def _softmax_pv_keys_major(q2, chunks):
    half = q2.shape[0] // 2
    scores = [_dot_nt(k, q2) for k, _ in chunks]
    m = acc = None
    for s, (_, v_ones) in zip(scores, chunks):
        mc = jnp.max(s, axis=0, keepdims=True)
        m_new = mc if m is None else jnp.maximum(m, mc)
        pv = _dot(v_ones.T[:V_ROWS], jnp.exp2(s - m_new).astype(BF16))
        acc = pv if m is None else jnp.exp2(m - m_new) * acc + pv
        m = m_new
    o = acc[:HEAD_DIM] / acc[HEAD_DIM:HEAD_DIM + 1]
    return jnp.concatenate([o[:, :half], o[:, half:]], axis=0).T


def _key_chunks(k_ref, v_ref, kv, chunk):
    n = k_ref.shape[1]
    kl, vl = slice(kv * LANES, (kv + 1) * LANES), slice(kv * 2 * LANES, (kv + 1) * 2 * LANES)
    return [(k_ref[0, c:c + chunk, kl], v_ref[0, c:c + chunk, vl], None) for c in range(0, n, chunk)]


def _gqa_body(n_src, chunk, q_ref, *refs):
    k_refs, v_refs, o_ref = refs[:n_src], refs[n_src:2 * n_src], refs[2 * n_src]
    pairs = q_ref.shape[2] // LANES
    kv_heads = k_refs[0].shape[2] // LANES
    for p in range(pairs):
        lanes = slice(p * LANES, (p + 1) * LANES)
        kv = p * kv_heads // pairs
        chunks = [c for k, v in zip(k_refs, v_refs) for c in _key_chunks(k, v, kv, min(chunk, k.shape[1]))]
        chunks = [(k, v2[:, :LANES]) for k, v2, _ in chunks]
        o_ref[0, :, lanes] = _softmax_pv_keys_major(_split_heads(q_ref[0, :, lanes]), chunks).astype(o_ref.dtype)


def _gqa_attention(q, ks, vs, tq, pairs, chunk=512):
    b, lq, w = q.shape
    n_src = len(ks)
    n_pairs = w // LANES
    kv_heads = ks[0].shape[2] // LANES
    kv_blk = max(1, pairs * kv_heads // n_pairs)
    kv_idx = lambda bi, i, j: (bi, 0, j * pairs * kv_heads // n_pairs // kv_blk)
    k_specs = [pl.BlockSpec((1, k.shape[1], kv_blk * LANES), kv_idx) for k in ks]
    v_specs = [pl.BlockSpec((1, v.shape[1], kv_blk * 2 * LANES), kv_idx) for v in vs]
    return pl.pallas_call(
        functools.partial(_gqa_body, n_src, chunk),
        grid=(b, lq // tq, n_pairs // pairs),
        in_specs=[pl.BlockSpec((1, tq, pairs * LANES), lambda bi, i, j: (bi, i, j))] + k_specs + v_specs,
        out_specs=pl.BlockSpec((1, tq, pairs * LANES), lambda bi, i, j: (bi, i, j)),
        out_shape=jax.ShapeDtypeStruct((b, lq, w), BF16),
        compiler_params=_params(("arbitrary", "arbitrary", "arbitrary")),
        name=f"gqa_attention_{n_src}src",
    )(q, *ks, *vs)


def _conv_core(rows, pad_ref, shift_ref, w_ref, cb_ref, lg_ref, lb_ref, store):
    span = shift_ref.shape[1]
    for r in range(1, SUBLANES):
        shift_ref[r] = pad_ref[r:r + span, :]
    sub = 32
    base = HALO - CONV_WIDTH // 2
    for r0 in range(0, rows, sub):
        acc = jnp.zeros((sub, pad_ref.shape[1]), F32) + cb_ref[...]
        for k in range(CONV_WIDTH):
            a, r = divmod(base + k, SUBLANES)
            lo = a * SUBLANES + r0
            tap = pad_ref[lo:lo + sub, :] if r == 0 else shift_ref[r, lo:lo + sub, :]
            acc = acc + tap * jnp.concatenate([w_ref[k]] * (sub // SUBLANES), axis=0)
        mu = jnp.mean(acc, axis=-1, keepdims=True)
        xc = acc - mu
        y = xc * lax.rsqrt(jnp.mean(xc * xc, axis=-1, keepdims=True) + EPS) * lg_ref[...] + lb_ref[...]
        store(r0, sub, _silu(y))


def _conv_body(rows, x_ref, prev_ref, next_ref, w_ref, cb_ref, lg_ref, lb_ref, o_ref, pad_ref, shift_ref):
    i = pl.program_id(1)
    last = pl.num_programs(1) - 1
    pad_ref[0:HALO, :] = jnp.where(i > 0, prev_ref[0], 0.0)
    pad_ref[HALO:HALO + rows, :] = x_ref[0]
    pad_ref[HALO + rows:HALO + rows + HALO, :] = jnp.where(i < last, next_ref[0], 0.0)

    def store(r0, sub, y):
        o_ref[0, r0:r0 + sub, :] = y.astype(o_ref.dtype)

    _conv_core(rows, pad_ref, shift_ref, w_ref, cb_ref, lg_ref, lb_ref, store)


def _conformer_conv(u, conv_w, conv_b, ln_g, ln_b, rows):
    b, l, c = u.shape
    hb = rows // HALO
    n_halo = l // HALO
    return pl.pallas_call(
        functools.partial(_conv_body, rows),
        grid=(b, l // rows),
        in_specs=[pl.BlockSpec((1, rows, c), lambda bi, i: (bi, i, 0)),
                  pl.BlockSpec((1, HALO, c), lambda bi, i: (bi, jnp.maximum(i * hb - 1, 0), 0)),
                  pl.BlockSpec((1, HALO, c), lambda bi, i: (bi, jnp.minimum((i + 1) * hb, n_halo - 1), 0)),
                  pl.BlockSpec((CONV_WIDTH, SUBLANES, c), lambda bi, i: (0, 0, 0)),
                  pl.BlockSpec((1, c), lambda bi, i: (0, 0)),
                  pl.BlockSpec((1, c), lambda bi, i: (0, 0)),
                  pl.BlockSpec((1, c), lambda bi, i: (0, 0))],
        out_specs=pl.BlockSpec((1, rows, c), lambda bi, i: (bi, i, 0)),
        out_shape=jax.ShapeDtypeStruct((b, l, c), BF16),
        scratch_shapes=[pltpu.VMEM((rows + 2 * HALO, c), F32),
                        pltpu.VMEM((SUBLANES, rows + 2 * HALO - SUBLANES, c), F32)],
        compiler_params=_params(("arbitrary", "arbitrary")),
        name="conformer_conv",
    )(u, u, u, jnp.broadcast_to(conv_w[:, None, :], (CONV_WIDTH, SUBLANES, c)),
      conv_b.reshape(1, c), ln_g.reshape(1, c), ln_b.reshape(1, c))


def _ctx_layer_body(x_ref, g0_ref, s1p0_ref, sh0_ref, wi0_ref, qg_ref, kg_ref, mq_ref, mk_ref, cw_ref, cb_ref,
                    lg_ref, lb_ref, wo0_ref, gate0_ref, g1_ref, s1p1_ref, sh1_ref, wkv1_ref,
                    ck_ref, cv_ref, kc_ref, vc_ref, pad_ref, shift_ref, ya_ref):
    rows = x_ref.shape[1]
    x = x_ref[0]
    u = _dot(_modulate(x, g0_ref, s1p0_ref, sh0_ref), wi0_ref[...])
    glu, q, k_dup, v_dup, sz = _ab_epilogue(u, mq_ref, mk_ref, qg_ref, kg_ref, None)
    ck_ref[0] = k_dup
    cv_ref[0] = v_dup
    pairs, kv_heads = q.shape[1] // LANES, k_dup.shape[1] // LANES
    yb = []
    for p in range(pairs):
        kv = p * kv_heads // pairs
        keys = [(k_dup[:, kv * LANES:(kv + 1) * LANES], v_dup[:, kv * 2 * LANES:(kv + 1) * 2 * LANES], None)]
        yb.append(_softmax_pv(_split_heads(q[:, p * LANES:(p + 1) * LANES]), keys))
    zeros = jnp.zeros((HALO, glu.shape[1]), F32)
    pad_ref[0:HALO, :] = zeros
    pad_ref[HALO:HALO + rows, :] = glu
    pad_ref[HALO + rows:HALO + rows + HALO, :] = zeros

    def store(r0, sub, y):
        ya_ref[r0:r0 + sub, :] = y

    _conv_core(rows, pad_ref, shift_ref, cw_ref, cb_ref, lg_ref, lb_ref, store)
    y = jnp.concatenate([ya_ref[...]] + yb, axis=-1) * sz.astype(F32)
    x1 = x + gate0_ref[0] * _dot(y.astype(BF16), wo0_ref[...])
    u1 = _dot(_modulate(x1, g1_ref, s1p1_ref, sh1_ref), wkv1_ref[...])
    kc_ref[0] = u1[:, 0:512].astype(BF16)
    vc_ref[0] = _pairs_with_ones(u1[:, 512:1024]).astype(BF16)


def _ctx_layer(ctx, g0, s1p0, sh0, w_in0, qg, kg, mq, mk, conv_w, conv_b, ln_g, ln_b, w_out0, gate0,
               g1, s1p1, sh1, w_in1, kv_col_block):
    b, l, d = ctx.shape
    c = conv_w.shape[1]
    row = lambda bi: (bi, 0, 0)
    const2 = lambda bi: (0, 0)
    vec = pl.BlockSpec((1, d), const2)
    per_b = pl.BlockSpec((1, 1, d), row)
    cvec = pl.BlockSpec((1, c), const2)
    return pl.pallas_call(
        _ctx_layer_body,
        grid=(b,),
        in_specs=[pl.BlockSpec((1, l, d), row), vec, per_b, per_b,
                  pl.BlockSpec(w_in0.shape, const2),
                  pl.BlockSpec((1, 512), const2), pl.BlockSpec((1, 128), const2),
                  pl.BlockSpec((512, 512), const2), pl.BlockSpec((128, 128), const2),
                  pl.BlockSpec((CONV_WIDTH, SUBLANES, c), lambda bi: (0, 0, 0)), cvec, cvec, cvec,
                  pl.BlockSpec(w_out0.shape, const2), per_b,
                  vec, per_b, per_b,
                  pl.BlockSpec((d, 1024), lambda bi: (0, kv_col_block))],
        out_specs=[pl.BlockSpec((1, l, 256), row), pl.BlockSpec((1, l, 512), row),
                   pl.BlockSpec((1, l, 512), row), pl.BlockSpec((1, l, 1024), row)],
        out_shape=[jax.ShapeDtypeStruct((b, l, 256), BF16), jax.ShapeDtypeStruct((b, l, 512), BF16),
                   jax.ShapeDtypeStruct((b, l, 512), BF16), jax.ShapeDtypeStruct((b, l, 1024), BF16)],
        scratch_shapes=[pltpu.VMEM((l + 2 * HALO, c), F32),
                        pltpu.VMEM((SUBLANES, l + 2 * HALO - SUBLANES, c), F32),
                        pltpu.VMEM((l, c), F32)],
        compiler_params=_params(("arbitrary",)),
        name="ctx_layer",
    )(ctx, g0, s1p0, sh0, w_in0, qg, kg, mq, mk,
      jnp.broadcast_to(conv_w[:, None, :], (CONV_WIDTH, SUBLANES, c)), conv_b.reshape(1, c), ln_g.reshape(1, c),
      ln_b.reshape(1, c), w_out0, gate0, g1, s1p1, sh1, w_in1)


def _outproj_final_body(ya_ref, yb_ref, sz_ref, w_ref, x_ref, gate_ref, perm_ref, fg_ref, o_ref):
    sub = perm_ref.shape[0]
    per = sub // DFT_RADIX
    for h in range(x_ref.shape[1] // sub):
        rows = slice(h * sub, (h + 1) * sub)
        stacked = jnp.concatenate([ya_ref[0, j, h * per:(h + 1) * per, :] for j in range(DFT_RADIX)], axis=0)
        ya = _dot(perm_ref[...], stacked)
        y = jnp.concatenate([ya, yb_ref[0, rows, :].astype(F32)], axis=-1) * sz_ref[0, rows, :].astype(F32)
        xn = x_ref[0, rows, :] + gate_ref[0] * _dot(y.astype(BF16), w_ref[...])
        o_ref[0, rows, :] = xn * lax.rsqrt(jnp.mean(xn * xn, axis=-1, keepdims=True) + EPS) * fg_ref[...]


def _outproj_final(ya, yb, sz, w, x, gate, final_g, tm):
    b, s, d = x.shape
    row = lambda bi, i: (bi, i, 0)
    const = lambda bi, i: (0, 0)
    return pl.pallas_call(
        _outproj_final_body,
        grid=(b, s // tm),
        in_specs=[pl.BlockSpec((1, DFT_RADIX, tm // DFT_RADIX, 512), lambda bi, i: (bi, 0, i, 0)),
                  pl.BlockSpec((1, tm, 512), row),
                  pl.BlockSpec((1, tm, 1024), row),
                  pl.BlockSpec((1024, d), const),
                  pl.BlockSpec((1, tm, d), row),
                  pl.BlockSpec((1, 1, d), lambda bi, i: (bi, 0, 0)),
                  pl.BlockSpec((OUT_PROJ_SUBTILE, OUT_PROJ_SUBTILE), const),
                  pl.BlockSpec((1, d), const)],
        out_specs=pl.BlockSpec((1, tm, d), row),
        out_shape=jax.ShapeDtypeStruct((b, s, d), F32),
        compiler_params=_params(("arbitrary", "arbitrary")),
        name="outproj_final",
    )(ya, yb, sz, w, x, gate, jnp.asarray(_class_row_permutation(OUT_PROJ_SUBTILE)).astype(BF16), final_g)


def _fourier_w_body(cc_ref, sc_ref, w_ref, o_ref):
    for g in range(w_ref.shape[0]):
        wc = jnp.dot(cc_ref[...], w_ref[g], preferred_element_type=F32, precision=HIGHEST)
        ws = jnp.dot(sc_ref[...], w_ref[g], preferred_element_type=F32, precision=HIGHEST)
        o_ref[g] = jnp.concatenate([wc, ws], axis=-1).astype(o_ref.dtype)


def _fourier_weights(cc, sc, w):
    g, c, _ = w.shape
    return pl.pallas_call(
        _fourier_w_body,
        out_shape=jax.ShapeDtypeStruct((g, c, 2 * c), BF16),
        name="fourier_weights",
    )(cc, sc, w)


def _fourier_tables(seq):
    quarter = seq // DFT_RADIX
    m = np.arange(quarter)
    tabs = []
    for j in range(DFT_RADIX):
        ang = 2.0 * np.pi * (((DFT_RADIX * m + j)[:, None] * m[None, :]) % seq) / seq
        tabs.append(np.concatenate([np.cos(ang), -np.sin(ang)], axis=1) / math.sqrt(seq))
    return np.stack(tabs).astype(np.float32)


def _class_row_permutation(rows):
    per = rows // DFT_RADIX
    p = np.zeros((rows, rows), np.float32)
    for j in range(DFT_RADIX):
        p[DFT_RADIX * np.arange(per) + j, j * per + np.arange(per)] = 1.0
    return p


def _fourier_body(f_ref, w1_ref, tab_ref, o_ref, yc_ref, ys_ref):
    groups, c, _ = w1_ref.shape
    quarter = f_ref.shape[1] // DFT_RADIX
    for g in range(groups):
        yg = _dot(f_ref[0, :, g * c:(g + 1) * c], w1_ref[g])
        yc_ref[:, g * c:(g + 1) * c] = yg[:, :c]
        ys_ref[:, g * c:(g + 1) * c] = yg[:, c:]
    for j in range(DFT_RADIX):
        pc, ps = [], []
        for q in range(DFT_RADIX):
            rows = slice(q * quarter, (q + 1) * quarter)
            a = (j * q) % 4
            if a == 0:
                pc.append((1, yc_ref, rows)), ps.append((1, ys_ref, rows))
            elif a == 1:
                pc.append((-1, ys_ref, rows)), ps.append((1, yc_ref, rows))
            elif a == 2:
                pc.append((-1, yc_ref, rows)), ps.append((-1, ys_ref, rows))
            else:
                pc.append((1, ys_ref, rows)), ps.append((-1, yc_ref, rows))
        folded = []
        for terms in (pc, ps):
            terms = sorted(terms, key=lambda t: -t[0])
            acc = terms[0][1][terms[0][2], :]
            for sign, ref, rows in terms[1:]:
                acc = acc + ref[rows, :] if sign > 0 else acc - ref[rows, :]
            folded.append(acc.astype(BF16))
        o_ref[0, j] = _dot(tab_ref[j], jnp.concatenate(folded, axis=0)).astype(o_ref.dtype)


def _fourier_mix(f, w1, tab):
    b, s, w = f.shape
    quarter = s // DFT_RADIX
    return pl.pallas_call(
        _fourier_body,
        grid=(b,),
        in_specs=[pl.BlockSpec((1, s, w), lambda bi: (bi, 0, 0)),
                  pl.BlockSpec(w1.shape, lambda bi: (0, 0, 0)),
                  pl.BlockSpec(tab.shape, lambda bi: (0, 0, 0))],
        out_specs=pl.BlockSpec((1, DFT_RADIX, quarter, w), lambda bi: (bi, 0, 0, 0)),
        out_shape=jax.ShapeDtypeStruct((b, DFT_RADIX, quarter, w), BF16),
        scratch_shapes=[pltpu.VMEM((s, w), F32), pltpu.VMEM((s, w), F32)],
        compiler_params=_params(("arbitrary",)),
        name="fourier_mix",
    )(f, w1, tab)


def _na_tile_geometry(rows_total):
    last_r0 = rows_total - NA_TILE_ROWS
    return ((0, 0), (2 * NA_TILE_ROWS, 2 * NA_TILE_ROWS - NA_ROWS // 2), (last_r0, rows_total - NA_WIN_ROWS))


def _na_bias_body(rows_total, rpb_ref, o_ref, t2_ref):
    h = pl.program_id(0)
    wq = lax.broadcasted_iota(jnp.int32, (GRID_W, LANES), 0)
    wk = lax.broadcasted_iota(jnp.int32, (GRID_W, LANES), 1) & (GRID_W - 1)
    cs = jnp.clip(wq - NA_COLS // 2, 0, GRID_W - NA_COLS)
    col_ok = (wk >= cs) & (wk < cs + NA_COLS)
    dc = wk - wq + (NA_COLS - 1)
    n_dr, n_dc = 2 * NA_ROWS - 1, 2 * NA_COLS - 1
    neg = jnp.full((GRID_W, LANES), NEG_INF, F32)
    for dr in range(n_dr):
        t2_ref[dr] = neg
    for c in range(n_dc):
        hit = col_ok & (dc == c)
        for dr in range(n_dr):
            t2_ref[dr] = jnp.where(hit, rpb_ref[h * (n_dr * n_dc) + dr * n_dc + c] * LOG2E, t2_ref[dr])
    lane = lax.broadcasted_iota(jnp.int32, (GRID_W, LANES), 1)
    left = lane < GRID_W
    for ty, (r0, ws) in enumerate(_na_tile_geometry(rows_total)):
        for i in range(NA_TILE_ROWS):
            r = r0 + i
            rs = min(max(r - NA_ROWS // 2, 0), rows_total - NA_ROWS)
            for jp in range(NA_WIN_ROWS // 2):
                halves = []
                for kr in (ws + 2 * jp, ws + 2 * jp + 1):
                    halves.append(t2_ref[kr - r + NA_ROWS - 1] if rs <= kr < rs + NA_ROWS else neg)
                o_ref[0, ty, i * GRID_W:(i + 1) * GRID_W, jp * LANES:(jp + 1) * LANES] = jnp.where(left, halves[0], halves[1])


def _na_bias_tables(rpb, rows_total):
    h = rpb.shape[0]
    q_rows, k_cols = NA_TILE_ROWS * GRID_W, NA_WIN_ROWS * GRID_W
    return pl.pallas_call(
        functools.partial(_na_bias_body, rows_total),
        grid=(h,),
        in_specs=[pl.BlockSpec(memory_space=pltpu.SMEM)],
        out_specs=pl.BlockSpec((1, 3, q_rows, k_cols), lambda hi: (hi, 0, 0, 0)),
        out_shape=jax.ShapeDtypeStruct((h, 3, q_rows, k_cols), F32),
        scratch_shapes=[pltpu.VMEM((2 * NA_ROWS - 1, GRID_W, LANES), F32)],
        compiler_params=_params(("arbitrary",)),
        name="na_bias_tables",
    )(rpb.reshape(-1))


def _na_body(rows_total, q_ref, k_ref, v_ref, kc_ref, vc_ref, bias_ref, o_ref):
    t = pl.program_id(2)
    n_t = pl.num_programs(2)
    ty = jnp.where(t == 0, 0, jnp.where(t == n_t - 1, 2, 1))
    ws = jnp.clip(t * NA_TILE_ROWS - NA_ROWS // 2, 0, rows_total - NA_WIN_ROWS)
    off = pl.multiple_of(ws * GRID_W, GRID_W)
    n_win = NA_WIN_ROWS * GRID_W
    tq = q_ref.shape[1]
    part = tq // NA_QUERY_SPLIT
    for bb in range(q_ref.shape[0]):
        for qs in range(NA_QUERY_SPLIT):
            rows = slice(qs * part, (qs + 1) * part)
            bias = jnp.concatenate([bias_ref[0, ty, rows, :], bias_ref[1, ty, rows, :]], axis=0)
            chunks = [(kc_ref[bb], vc_ref[bb], None),
                      (k_ref[bb, pl.ds(off, n_win), :], v_ref[bb, pl.ds(off, n_win), :], bias)]
            o = _softmax_pv(_split_heads(q_ref[bb, rows, :]), chunks)
            o_ref[bb, rows, :] = o.astype(o_ref.dtype)


def _neighborhood_attention(q, k, v, kc, vc, bias):
    b, s, w = q.shape
    l = kc.shape[1]
    rows_total = s // GRID_W
    tq = NA_TILE_ROWS * GRID_W
    nb = NA_BATCH_PER_STEP
    full = lambda j, bi, t: (bi, 0, j)
    return pl.pallas_call(
        functools.partial(_na_body, rows_total),
        grid=(w // LANES, b // nb, s // tq),
        in_specs=[pl.BlockSpec((nb, tq, LANES), lambda j, bi, t: (bi, t, j)),
                  pl.BlockSpec((nb, s, LANES), full),
                  pl.BlockSpec((nb, s, 2 * LANES), full),
                  pl.BlockSpec((nb, l, LANES), full),
                  pl.BlockSpec((nb, l, 2 * LANES), full),
                  pl.BlockSpec((2,) + bias.shape[1:], lambda j, bi, t: (j, 0, 0, 0))],
        out_specs=pl.BlockSpec((nb, tq, LANES), lambda j, bi, t: (bi, t, j)),
        out_shape=jax.ShapeDtypeStruct((b, s, w), BF16),
        compiler_params=_params(("arbitrary", "arbitrary", "arbitrary")),
        name="neighborhood_attention",
    )(q, k, v, kc, vc, bias)


def _modulation(m, rows, batch):
    d = m.shape[1] // 3
    pick = lambda lo: jnp.broadcast_to(m[rows, lo:lo + d].reshape(-1, 1, d), (batch, 1, d))
    return pick(0), 1.0 + pick(d), pick(2 * d)


def kernel(x, c, ctx, c_ctx, ab_w_ada, ab_b_ada, ab_norm_g, ab_w_in, ab_conv_w, ab_conv_b, ab_ln_g, ab_ln_b, ab_q_norm_g, ab_k_norm_g, ab_w_out, cd_w_ada, cd_b_ada, cd_norm_g, cd_w_in, cd_w_fourier, cd_rpb, cd_w_out, final_norm_g):
    b, s, d = x.shape
    l = ctx.shape[1]
    assert d == D_MODEL and s % (NA_TILE_ROWS * GRID_W) == 0 and ab_w_ada.shape[0] == 1 and cd_w_ada.shape[0] == 1

    cond = jnp.zeros((16, d), F32).at[:b].set(c).at[b].set(c_ctx)
    lat = slice(0, b)
    cx = slice(b, b + 1)

    m0 = _adaln(cond, ab_w_ada[0], ab_b_ada[0])
    m1 = _adaln(cond, cd_w_ada[0], cd_b_ada[0])
    sh, s1p, gate = _modulation(m0, lat, b)
    csh, cs1p, cgate = _modulation(m0, cx, b)
    sh1, s1p1, gate1 = _modulation(m1, lat, b)
    csh1, cs1p1, _ = _modulation(m1, cx, b)
    ng1 = cd_norm_g[0].reshape(1, d)
    w_in1 = cd_w_in[0].astype(BF16)

    w_in0 = ab_w_in[0].astype(BF16)
    w_out0 = ab_w_out[0].astype(BF16)

    cos, sin = _rope_tables(s)
    cos, sin = jnp.asarray(cos), jnp.asarray(sin)
    mq = jnp.asarray(_block_ones(512)).astype(BF16)
    mk = jnp.asarray(_block_ones(128)).astype(BF16)
    qg = jnp.tile(ab_q_norm_g[0], 512 // HEAD_DIM).reshape(1, 512)
    kg = jnp.tile(ab_k_norm_g[0], 128 // HEAD_DIM).reshape(1, 128)
    ng0 = ab_norm_g[0].reshape(1, d)

    ck, cv, kc, vc = _ctx_layer(ctx, ng0, cs1p, csh, w_in0, qg, kg, mq, mk, ab_conv_w[0], ab_conv_b[0], ab_ln_g[0],
                                ab_ln_b[0], w_out0, cgate, ng1, cs1p1, csh1, w_in1, 1)

    glu, q, k, v, sz = _inproj_ab(x, ng0, s1p, sh, w_in0, qg, kg, cos, sin, mq, mk, 1024)
    y_b = _gqa_attention(q, [ck, k], [cv, v], 1024, 1, chunk=128)
    y_a = _conformer_conv(glu, ab_conv_w[0], ab_conv_b[0], ab_ln_g[0], ab_ln_b[0], 1024)

    x1, f, q, k, v, sz = _outproj_inproj_cd(y_a, y_b, sz, w_out0, x, gate, ng1, s1p1, sh1, w_in1, 1024)
    gate = gate1

    c_c, s_c = _dft_tables(cd_w_fourier.shape[-1])
    w1 = _fourier_weights(jnp.asarray(c_c), jnp.asarray(s_c), cd_w_fourier[0])
    y_c = _fourier_mix(f, w1, jnp.asarray(_fourier_tables(s)).astype(BF16))

    bias = _na_bias_tables(cd_rpb[0], s // GRID_W)
    y_d = _neighborhood_attention(q, k, v, kc, vc, bias)

    return _outproj_final(y_c, y_d, sz, cd_w_out[0].astype(BF16), x1, gate, final_norm_g.reshape(1, d), 1024)
```

```python
import functools
import math

import numpy as np
import jax
import jax.numpy as jnp
from jax import lax
from jax.experimental import pallas as pl
from jax.experimental.pallas import tpu as pltpu

F32 = jnp.float32
BF16 = jnp.bfloat16
HIGHEST = lax.Precision.HIGHEST

D_MODEL = 1024
HEAD_DIM = 64
GRID_W = 64
CONV_WIDTH = 31
NA_ROWS = 8
NA_COLS = 16
ROPE_THETA = 10000.0
EPS = 1e-6
NEG_INF = -1e30
LOG2E = math.log2(math.e)
Q_SCALE = HEAD_DIM ** -0.5 * LOG2E
LANES = 128
SUBLANES = 8
HALO = 16
VMEM_LIMIT = 48 * 1024 * 1024

NA_TILE_ROWS = 4
NA_WIN_ROWS = NA_TILE_ROWS + NA_ROWS
NA_BATCH_PER_STEP = 8
NA_QUERY_SPLIT = 2
IN_PROJ_SUBTILE = 256
MID_SUBTILE = 1024
OUT_PROJ_SUBTILE = 512
DFT_RADIX = 4
V_ROWS = 80


def _params(sem, n_inputs=0, fused_inputs=()):
    fuse = [i in fused_inputs for i in range(n_inputs)] if fused_inputs else None
    return pltpu.CompilerParams(dimension_semantics=sem, vmem_limit_bytes=VMEM_LIMIT, allow_input_fusion=fuse)


def _silu(t):
    return t * jax.nn.sigmoid(t)


def _dot(a, b):
    return jnp.dot(a, b, preferred_element_type=F32)


def _dot_nt(a, b):
    return lax.dot_general(a, b, (((1,), (1,)), ((), ())), preferred_element_type=F32)


def _rope_tables(seq):
    t = np.arange(seq)
    row = (t // GRID_W).astype(np.float64)
    col = (t % GRID_W).astype(np.float64)
    nf = HEAD_DIM // 4
    inv = np.float32(ROPE_THETA) ** (-np.arange(nf, dtype=np.float32) / nf)
    ang = np.concatenate([row[:, None] * inv, col[:, None] * inv], axis=-1)
    cos = np.repeat(np.cos(ang), 2, axis=-1)
    sin = np.repeat(np.sin(ang), 2, axis=-1)
    sign = np.tile(np.array([-1.0, 1.0]), HEAD_DIM // 2)
    cos = np.tile(cos, (1, LANES // HEAD_DIM))
    sin = np.tile(sin * sign, (1, LANES // HEAD_DIM))
    return cos.astype(np.float32), sin.astype(np.float32)


def _dft_tables(n):
    k = np.arange(n)
    ang = 2.0 * np.pi * ((k[:, None] * k[None, :]) % n) / n
    s = 1.0 / math.sqrt(n)
    return (np.cos(ang) * s).astype(np.float32), (np.sin(ang) * s).astype(np.float32)


def _block_ones(width):
    h = np.arange(width) // HEAD_DIM
    return (h[:, None] == h[None, :]).astype(np.float32)


def _adaln_body(c_ref, w_ref, b_ref, o_ref):
    a = _silu(c_ref[...]).astype(BF16)
    o_ref[...] = _dot(a, w_ref[...].astype(BF16)) + b_ref[...]


def _adaln(cond, w, b):
    r, d = cond.shape
    n = w.shape[1]
    tn = 1024
    return pl.pallas_call(
        _adaln_body,
        grid=(n // tn,),
        in_specs=[pl.BlockSpec((r, d), lambda j: (0, 0)),
                  pl.BlockSpec((d, tn), lambda j: (0, j)),
                  pl.BlockSpec((1, tn), lambda j: (0, j))],
        out_specs=pl.BlockSpec((r, tn), lambda j: (0, j)),
        out_shape=jax.ShapeDtypeStruct((r, n), F32),
        compiler_params=_params(("arbitrary",)),
        name="adaln",
    )(cond, w, b.reshape(1, n))


def _head_rms(t, ones_ref, gain):
    ms = _dot((t * t).astype(BF16), ones_ref[...]) * (1.0 / HEAD_DIM)
    return t * lax.rsqrt(ms + EPS) * gain


def _rope(t, cos, sin_signed):
    rows = t.shape[0]
    lane = lax.broadcasted_iota(jnp.int32, (rows, LANES), 1)
    even = (lane & 1) == 0
    outs = []
    for c in range(t.shape[1] // LANES):
        xc = t[:, c * LANES:(c + 1) * LANES]
        swapped = jnp.where(even, pltpu.roll(xc, LANES - 1, 1), pltpu.roll(xc, 1, 1))
        outs.append(xc * cos + swapped * sin_signed)
    return outs[0] if len(outs) == 1 else jnp.concatenate(outs, axis=1)


def _modulate(x, g_ref, s1p_ref, sh_ref):
    y = x * lax.rsqrt(jnp.mean(x * x, axis=-1, keepdims=True) + EPS) * g_ref[...]
    return (y * s1p_ref[0] + sh_ref[0]).astype(BF16)


def _row_subtiles(tm, sub):
    sub = min(tm, sub)
    return [slice(r, r + sub) for r in range(0, tm, sub)]


def _inproj_ab_body(x_ref, g_ref, s1p_ref, sh_ref, w_ref, qg_ref, kg_ref, cos_ref, sin_ref, mq_ref, mk_ref,
                    glu_ref, q_ref, k_ref, v_ref, sz_ref):
    for rows in _row_subtiles(x_ref.shape[1], IN_PROJ_SUBTILE):
        u = _dot(_modulate(x_ref[0, rows, :], g_ref, s1p_ref, sh_ref), w_ref[...])
        glu, q, k, v, sz = _ab_epilogue(u, mq_ref, mk_ref, qg_ref, kg_ref, (cos_ref[rows, :], sin_ref[rows, :]))
        glu_ref[0, rows, :] = glu
        q_ref[0, rows, :] = q
        k_ref[0, rows, :] = k
        v_ref[0, rows, :] = v
        sz_ref[0, rows, :] = sz


def _ab_epilogue(u, mq_ref, mk_ref, qg_ref, kg_ref, rope_tabs):
    glu = u[:, 0:512] * jax.nn.sigmoid(u[:, 512:1024])
    q = _head_rms(u[:, 1024:1536], mq_ref, qg_ref[...])
    k = _head_rms(u[:, 1536:1664], mk_ref, kg_ref[...])
    if rope_tabs is not None:
        q = _rope(q, *rope_tabs)
        k = _rope(k, *rope_tabs)
    v = u[:, 1664:1792]
    k_sw, v_sw = pltpu.roll(k, HEAD_DIM, 1), pltpu.roll(v, HEAD_DIM, 1)
    left = lax.broadcasted_iota(jnp.int32, k.shape, 1) < HEAD_DIM
    one = jnp.ones_like(v)
    k_dup = jnp.concatenate([jnp.where(left, k, k_sw), jnp.where(left, k_sw, k)], axis=1)
    v_dup = jnp.concatenate([jnp.where(left, v, one), jnp.where(left, one, v_sw),
                             jnp.where(left, v_sw, one), jnp.where(left, one, v)], axis=1)
    return (glu, (q * Q_SCALE).astype(BF16), k_dup.astype(BF16), v_dup.astype(BF16),
            _silu(u[:, 1792:2816]).astype(BF16))


def _inproj_ab(x, g, s1p, sh, w, qg, kg, cos, sin, mq, mk, tm):
    b, s, d = x.shape
    n = w.shape[1]
    row = lambda bi, i: (bi, i, 0)
    const = lambda bi, i: (0, 0)
    per_b = lambda bi, i: (bi, 0, 0)
    tab = lambda bi, i: (i, 0)
    return pl.pallas_call(
        _inproj_ab_body,
        grid=(b, s // tm),
        in_specs=[pl.BlockSpec((1, tm, d), row),
                  pl.BlockSpec((1, d), const),
                  pl.BlockSpec((1, 1, d), per_b),
                  pl.BlockSpec((1, 1, d), per_b),
                  pl.BlockSpec((d, n), const),
                  pl.BlockSpec((1, 512), const),
                  pl.BlockSpec((1, 128), const),
                  pl.BlockSpec((tm, LANES), tab),
                  pl.BlockSpec((tm, LANES), tab),
                  pl.BlockSpec((512, 512), const),
                  pl.BlockSpec((128, 128), const)],
        out_specs=[pl.BlockSpec((1, tm, 512), row),
                   pl.BlockSpec((1, tm, 512), row),
                   pl.BlockSpec((1, tm, 256), row),
                   pl.BlockSpec((1, tm, 512), row),
                   pl.BlockSpec((1, tm, 1024), row)],
        out_shape=[jax.ShapeDtypeStruct((b, s, 512), F32),
                   jax.ShapeDtypeStruct((b, s, 512), BF16),
                   jax.ShapeDtypeStruct((b, s, 256), BF16),
                   jax.ShapeDtypeStruct((b, s, 512), BF16),
                   jax.ShapeDtypeStruct((b, s, 1024), BF16)],
        compiler_params=_params(("arbitrary", "arbitrary"), 11, (4,)),
        name="inproj_ab",
    )(x, g, s1p, sh, w, qg, kg, cos, sin, mq, mk)


def _outproj_inproj_cd_body(ya_ref, yb_ref, sz0_ref, wo_ref, x_ref, gate_ref, g_ref, s1p_ref, sh_ref, wi_ref,
                            x1_ref, f_ref, q_ref, k_ref, v_ref, sz_ref):
    for rows in _row_subtiles(x_ref.shape[1], MID_SUBTILE):
        y = (jnp.concatenate([ya_ref[0, rows, :], yb_ref[0, rows, :]], axis=-1).astype(F32)
             * sz0_ref[0, rows, :].astype(F32))
        xn = x_ref[0, rows, :] + gate_ref[0] * _dot(y.astype(BF16), wo_ref[...])
        x1_ref[0, rows, :] = xn
        u = _dot(_modulate(xn, g_ref, s1p_ref, sh_ref), wi_ref[...])
        f_ref[0, rows, :] = u[:, 0:512].astype(BF16)
        q_ref[0, rows, :] = (u[:, 512:1024] * Q_SCALE).astype(BF16)
        k_ref[0, rows, :] = u[:, 1024:1536].astype(BF16)
        v_ref[0, rows, :] = _pairs_with_ones(u[:, 1536:2048]).astype(BF16)
        sz_ref[0, rows, :] = _silu(u[:, 2048:3072]).astype(BF16)


def _outproj_inproj_cd(ya, yb, sz0, w_out, x, gate, g, s1p, sh, w_in, tm):
    b, s, d = x.shape
    n = w_in.shape[1]
    row = lambda bi, i: (bi, i, 0)
    const = lambda bi, i: (0, 0)
    per_b = lambda bi, i: (bi, 0, 0)
    return pl.pallas_call(
        _outproj_inproj_cd_body,
        grid=(b, s // tm),
        in_specs=[pl.BlockSpec((1, tm, 512), row),
                  pl.BlockSpec((1, tm, 512), row),
                  pl.BlockSpec((1, tm, 1024), row),
                  pl.BlockSpec((1024, d), const, pipeline_mode=pl.Buffered(1)),
                  pl.BlockSpec((1, tm, d), row),
                  pl.BlockSpec((1, 1, d), per_b),
                  pl.BlockSpec((1, d), const),
                  pl.BlockSpec((1, 1, d), per_b),
                  pl.BlockSpec((1, 1, d), per_b),
                  pl.BlockSpec((d, n), const, pipeline_mode=pl.Buffered(1))],
        out_specs=[pl.BlockSpec((1, tm, d), row)] + [pl.BlockSpec((1, tm, 512), row)] * 3
        + [pl.BlockSpec((1, tm, 1024), row)] * 2,
        out_shape=[jax.ShapeDtypeStruct((b, s, d), F32)] + [jax.ShapeDtypeStruct((b, s, 512), BF16)] * 3
        + [jax.ShapeDtypeStruct((b, s, 1024), BF16)] * 2,
        compiler_params=_params(("arbitrary", "arbitrary"), 10, (3, 9)),
        name="outproj_inproj_cd",
    )(ya, yb, sz0, w_out, x, gate, g, s1p, sh, w_in)


def _split_heads(q):
    lane = lax.broadcasted_iota(jnp.int32, q.shape, 1)
    zero = jnp.zeros_like(q)
    return jnp.concatenate([jnp.where(lane < HEAD_DIM, q, zero), jnp.where(lane >= HEAD_DIM, q, zero)], axis=0)


def _with_ones(v):
    lane = lax.broadcasted_iota(jnp.int32, v.shape, 1)
    one = jnp.ones_like(v)
    return jnp.concatenate([jnp.where(lane < HEAD_DIM, v, one), jnp.where(lane < HEAD_DIM, one, v)], axis=1)


def _pairs_with_ones(v):
    return jnp.concatenate([_with_ones(v[:, c:c + LANES]) for c in range(0, v.shape[1], LANES)], axis=1)


def _pv(p, v2, rows):
    pv = _dot(p.astype(BF16), v2)
    return jnp.concatenate([pv[:rows, :LANES], pv[rows:, LANES:]], axis=0)


def _softmax_pv(q2, chunks):
    rows = q2.shape[0] // 2
    scores = [_dot_nt(q2, k) if bias is None else _dot_nt(q2, k) + bias for k, _, bias in chunks]
    m = functools.reduce(jnp.maximum, [jnp.max(s, axis=-1, keepdims=True) for s in scores])
    acc = functools.reduce(jnp.add, [_pv(jnp.exp2(s - m), v2, rows) for s, (_, v2, _) in zip(scores, chunks)])
    left = lax.broadcasted_iota(jnp.int32, (rows, LANES), 1) < HEAD_DIM
    top, bot = acc[:rows], acc[rows:]
    return jnp.where(left, top, bot) / pltpu.roll(jnp.where(left, bot, top), HEAD_DIM, 1)


def _softmax_pv_keys_major(q2, chunks):
    half = q2.shape[0] // 2
    scores = [_dot_nt(k, q2) for k, _ in chunks]
    m = acc = None
    for s, (_, v_ones) in zip(scores, chunks):
        mc = jnp.max(s, axis=0, keepdims=True)
        m_new = mc if m is None else jnp.maximum(m, mc)
        pv = _dot(v_ones.T[:V_ROWS], jnp.exp2(s - m_new).astype(BF16))
        acc = pv if m is None else jnp.exp2(m - m_new) * acc + pv
        m = m_new
    o = acc[:HEAD_DIM] / acc[HEAD_DIM:HEAD_DIM + 1]
    return jnp.concatenate([o[:, :half], o[:, half:]], axis=0).T


def _key_chunks(k_ref, v_ref, kv, chunk):
    n = k_ref.shape[1]
    kl, vl = slice(kv * LANES, (kv + 1) * LANES), slice(kv * 2 * LANES, (kv + 1) * 2 * LANES)
    return [(k_ref[0, c:c + chunk, kl], v_ref[0, c:c + chunk, vl], None) for c in range(0, n, chunk)]


def _gqa_body(n_src, chunk, q_ref, *refs):
    k_refs, v_refs, o_ref = refs[:n_src], refs[n_src:2 * n_src], refs[2 * n_src]
    pairs = q_ref.shape[2] // LANES
    kv_heads = k_refs[0].shape[2] // LANES
    for p in range(pairs):
        lanes = slice(p * LANES, (p + 1) * LANES)
        kv = p * kv_heads // pairs
        chunks = [c for k, v in zip(k_refs, v_refs) for c in _key_chunks(k, v, kv, min(chunk, k.shape[1]))]
        chunks = [(k, v2[:, :LANES]) for k, v2, _ in chunks]
        o_ref[0, :, lanes] = _softmax_pv_keys_major(_split_heads(q_ref[0, :, lanes]), chunks).astype(o_ref.dtype)


def _gqa_attention(q, ks, vs, tq, pairs, chunk=512):
    b, lq, w = q.shape
    n_src = len(ks)
    n_pairs = w // LANES
    kv_heads = ks[0].shape[2] // LANES
    kv_blk = max(1, pairs * kv_heads // n_pairs)
    kv_idx = lambda bi, i, j: (bi, 0, j * pairs * kv_heads // n_pairs // kv_blk)
    k_specs = [pl.BlockSpec((1, k.shape[1], kv_blk * LANES), kv_idx) for k in ks]
    v_specs = [pl.BlockSpec((1, v.shape[1], kv_blk * 2 * LANES), kv_idx) for v in vs]
    return pl.pallas_call(
        functools.partial(_gqa_body, n_src, chunk),
        grid=(b, lq // tq, n_pairs // pairs),
        in_specs=[pl.BlockSpec((1, tq, pairs * LANES), lambda bi, i, j: (bi, i, j))] + k_specs + v_specs,
        out_specs=pl.BlockSpec((1, tq, pairs * LANES), lambda bi, i, j: (bi, i, j)),
        out_shape=jax.ShapeDtypeStruct((b, lq, w), BF16),
        compiler_params=_params(("arbitrary", "arbitrary", "arbitrary")),
        name=f"gqa_attention_{n_src}src",
    )(q, *ks, *vs)


def _conv_core(rows, pad_ref, shift_ref, w_ref, cb_ref, lg_ref, lb_ref, store):
    span = shift_ref.shape[1]
    for r in range(1, SUBLANES):
        shift_ref[r] = pad_ref[r:r + span, :]
    sub = 32
    base = HALO - CONV_WIDTH // 2
    for r0 in range(0, rows, sub):
        acc = jnp.zeros((sub, pad_ref.shape[1]), F32) + cb_ref[...]
        for k in range(CONV_WIDTH):
            a, r = divmod(base + k, SUBLANES)
            lo = a * SUBLANES + r0
            tap = pad_ref[lo:lo + sub, :] if r == 0 else shift_ref[r, lo:lo + sub, :]
            acc = acc + tap * jnp.concatenate([w_ref[k]] * (sub // SUBLANES), axis=0)
        mu = jnp.mean(acc, axis=-1, keepdims=True)
        xc = acc - mu
        y = xc * lax.rsqrt(jnp.mean(xc * xc, axis=-1, keepdims=True) + EPS) * lg_ref[...] + lb_ref[...]
        store(r0, sub, _silu(y))


def _conv_body(rows, x_ref, prev_ref, next_ref, w_ref, cb_ref, lg_ref, lb_ref, o_ref, pad_ref, shift_ref):
    i = pl.program_id(1)
    last = pl.num_programs(1) - 1
    pad_ref[0:HALO, :] = jnp.where(i > 0, prev_ref[0], 0.0)
    pad_ref[HALO:HALO + rows, :] = x_ref[0]
    pad_ref[HALO + rows:HALO + rows + HALO, :] = jnp.where(i < last, next_ref[0], 0.0)

    def store(r0, sub, y):
        o_ref[0, r0:r0 + sub, :] = y.astype(o_ref.dtype)

    _conv_core(rows, pad_ref, shift_ref, w_ref, cb_ref, lg_ref, lb_ref, store)


def _conformer_conv(u, conv_w, conv_b, ln_g, ln_b, rows):
    b, l, c = u.shape
    hb = rows // HALO
    n_halo = l // HALO
    return pl.pallas_call(
        functools.partial(_conv_body, rows),
        grid=(b, l // rows),
        in_specs=[pl.BlockSpec((1, rows, c), lambda bi, i: (bi, i, 0)),
                  pl.BlockSpec((1, HALO, c), lambda bi, i: (bi, jnp.maximum(i * hb - 1, 0), 0)),
                  pl.BlockSpec((1, HALO, c), lambda bi, i: (bi, jnp.minimum((i + 1) * hb, n_halo - 1), 0)),
                  pl.BlockSpec((CONV_WIDTH, SUBLANES, c), lambda bi, i: (0, 0, 0)),
                  pl.BlockSpec((1, c), lambda bi, i: (0, 0)),
                  pl.BlockSpec((1, c), lambda bi, i: (0, 0)),
                  pl.BlockSpec((1, c), lambda bi, i: (0, 0))],
        out_specs=pl.BlockSpec((1, rows, c), lambda bi, i: (bi, i, 0)),
        out_shape=jax.ShapeDtypeStruct((b, l, c), BF16),
        scratch_shapes=[pltpu.VMEM((rows + 2 * HALO, c), F32),
                        pltpu.VMEM((SUBLANES, rows + 2 * HALO - SUBLANES, c), F32)],
        compiler_params=_params(("arbitrary", "arbitrary")),
        name="conformer_conv",
    )(u, u, u, jnp.broadcast_to(conv_w[:, None, :], (CONV_WIDTH, SUBLANES, c)),
      conv_b.reshape(1, c), ln_g.reshape(1, c), ln_b.reshape(1, c))


def _ctx_layer_body(x_ref, g0_ref, s1p0_ref, sh0_ref, wi0_ref, qg_ref, kg_ref, mq_ref, mk_ref, cw_ref, cb_ref,
                    lg_ref, lb_ref, wo0_ref, gate0_ref, g1_ref, s1p1_ref, sh1_ref, wkv1_ref,
                    ck_ref, cv_ref, kc_ref, vc_ref, pad_ref, shift_ref, ya_ref):
    rows = x_ref.shape[1]
    x = x_ref[0]
    u = _dot(_modulate(x, g0_ref, s1p0_ref, sh0_ref), wi0_ref[...])
    glu, q, k_dup, v_dup, sz = _ab_epilogue(u, mq_ref, mk_ref, qg_ref, kg_ref, None)
    ck_ref[0] = k_dup
    cv_ref[0] = v_dup
    pairs, kv_heads = q.shape[1] // LANES, k_dup.shape[1] // LANES
    yb = []
    for p in range(pairs):
        kv = p * kv_heads // pairs
        keys = [(k_dup[:, kv * LANES:(kv + 1) * LANES], v_dup[:, kv * 2 * LANES:(kv + 1) * 2 * LANES], None)]
        yb.append(_softmax_pv(_split_heads(q[:, p * LANES:(p + 1) * LANES]), keys))
    zeros = jnp.zeros((HALO, glu.shape[1]), F32)
    pad_ref[0:HALO, :] = zeros
    pad_ref[HALO:HALO + rows, :] = glu
    pad_ref[HALO + rows:HALO + rows + HALO, :] = zeros

    def store(r0, sub, y):
        ya_ref[r0:r0 + sub, :] = y

    _conv_core(rows, pad_ref, shift_ref, cw_ref, cb_ref, lg_ref, lb_ref, store)
    y = jnp.concatenate([ya_ref[...]] + yb, axis=-1) * sz.astype(F32)
    x1 = x + gate0_ref[0] * _dot(y.astype(BF16), wo0_ref[...])
    u1 = _dot(_modulate(x1, g1_ref, s1p1_ref, sh1_ref), wkv1_ref[...])
    kc_ref[0] = u1[:, 0:512].astype(BF16)
    vc_ref[0] = _pairs_with_ones(u1[:, 512:1024]).astype(BF16)


def _ctx_layer(ctx, g0, s1p0, sh0, w_in0, qg, kg, mq, mk, conv_w, conv_b, ln_g, ln_b, w_out0, gate0,
               g1, s1p1, sh1, w_in1, kv_col_block):
    b, l, d = ctx.shape
    c = conv_w.shape[1]
    row = lambda bi: (bi, 0, 0)
    const2 = lambda bi: (0, 0)
    vec = pl.BlockSpec((1, d), const2)
    per_b = pl.BlockSpec((1, 1, d), row)
    cvec = pl.BlockSpec((1, c), const2)
    return pl.pallas_call(
        _ctx_layer_body,
        grid=(b,),
        in_specs=[pl.BlockSpec((1, l, d), row), vec, per_b, per_b,
                  pl.BlockSpec(w_in0.shape, const2),
                  pl.BlockSpec((1, 512), const2), pl.BlockSpec((1, 128), const2),
                  pl.BlockSpec((512, 512), const2), pl.BlockSpec((128, 128), const2),
                  pl.BlockSpec((CONV_WIDTH, SUBLANES, c), lambda bi: (0, 0, 0)), cvec, cvec, cvec,
                  pl.BlockSpec(w_out0.shape, const2), per_b,
                  vec, per_b, per_b,
                  pl.BlockSpec((d, 1024), lambda bi: (0, kv_col_block))],
        out_specs=[pl.BlockSpec((1, l, 256), row), pl.BlockSpec((1, l, 512), row),
                   pl.BlockSpec((1, l, 512), row), pl.BlockSpec((1, l, 1024), row)],
        out_shape=[jax.ShapeDtypeStruct((b, l, 256), BF16), jax.ShapeDtypeStruct((b, l, 512), BF16),
                   jax.ShapeDtypeStruct((b, l, 512), BF16), jax.ShapeDtypeStruct((b, l, 1024), BF16)],
        scratch_shapes=[pltpu.VMEM((l + 2 * HALO, c), F32),
                        pltpu.VMEM((SUBLANES, l + 2 * HALO - SUBLANES, c), F32),
                        pltpu.VMEM((l, c), F32)],
        compiler_params=_params(("arbitrary",), 19, (4, 13, 18)),
        name="ctx_layer",
    )(ctx, g0, s1p0, sh0, w_in0, qg, kg, mq, mk,
      jnp.broadcast_to(conv_w[:, None, :], (CONV_WIDTH, SUBLANES, c)), conv_b.reshape(1, c), ln_g.reshape(1, c),
      ln_b.reshape(1, c), w_out0, gate0, g1, s1p1, sh1, w_in1)


def _outproj_final_body(ya_ref, yb_ref, sz_ref, w_ref, x_ref, gate_ref, perm_ref, fg_ref, o_ref):
    sub = perm_ref.shape[0]
    per = sub // DFT_RADIX
    for h in range(x_ref.shape[1] // sub):
        rows = slice(h * sub, (h + 1) * sub)
        stacked = jnp.concatenate([ya_ref[0, j, h * per:(h + 1) * per, :] for j in range(DFT_RADIX)], axis=0)
        ya = _dot(perm_ref[...], stacked)
        y = jnp.concatenate([ya, yb_ref[0, rows, :].astype(F32)], axis=-1) * sz_ref[0, rows, :].astype(F32)
        xn = x_ref[0, rows, :] + gate_ref[0] * _dot(y.astype(BF16), w_ref[...])
        o_ref[0, rows, :] = xn * lax.rsqrt(jnp.mean(xn * xn, axis=-1, keepdims=True) + EPS) * fg_ref[...]


def _outproj_final(ya, yb, sz, w, x, gate, final_g, tm):
    b, s, d = x.shape
    row = lambda bi, i: (bi, i, 0)
    const = lambda bi, i: (0, 0)
    return pl.pallas_call(
        _outproj_final_body,
        grid=(b, s // tm),
        in_specs=[pl.BlockSpec((1, DFT_RADIX, tm // DFT_RADIX, 512), lambda bi, i: (bi, 0, i, 0)),
                  pl.BlockSpec((1, tm, 512), row),
                  pl.BlockSpec((1, tm, 1024), row),
                  pl.BlockSpec((1024, d), const),
                  pl.BlockSpec((1, tm, d), row),
                  pl.BlockSpec((1, 1, d), lambda bi, i: (bi, 0, 0)),
                  pl.BlockSpec((OUT_PROJ_SUBTILE, OUT_PROJ_SUBTILE), const),
                  pl.BlockSpec((1, d), const)],
        out_specs=pl.BlockSpec((1, tm, d), row),
        out_shape=jax.ShapeDtypeStruct((b, s, d), F32),
        compiler_params=_params(("arbitrary", "arbitrary"), 8, (3,)),
        name="outproj_final",
    )(ya, yb, sz, w, x, gate, jnp.asarray(_class_row_permutation(OUT_PROJ_SUBTILE)).astype(BF16), final_g)


def _fourier_w_body(cc_ref, sc_ref, w_ref, o_ref):
    for g in range(w_ref.shape[0]):
        wc = jnp.dot(cc_ref[...], w_ref[g], preferred_element_type=F32, precision=HIGHEST)
        ws = jnp.dot(sc_ref[...], w_ref[g], preferred_element_type=F32, precision=HIGHEST)
        o_ref[g] = jnp.concatenate([wc, ws], axis=-1).astype(o_ref.dtype)


def _fourier_weights(cc, sc, w):
    g, c, _ = w.shape
    return pl.pallas_call(
        _fourier_w_body,
        out_shape=jax.ShapeDtypeStruct((g, c, 2 * c), BF16),
        name="fourier_weights",
    )(cc, sc, w)


def _fourier_tables(seq):
    quarter = seq // DFT_RADIX
    m = np.arange(quarter)
    tabs = []
    for j in range(DFT_RADIX):
        ang = 2.0 * np.pi * (((DFT_RADIX * m + j)[:, None] * m[None, :]) % seq) / seq
        tabs.append(np.concatenate([np.cos(ang), -np.sin(ang)], axis=1) / math.sqrt(seq))
    return np.stack(tabs).astype(np.float32)


def _class_row_permutation(rows):
    per = rows // DFT_RADIX
    p = np.zeros((rows, rows), np.float32)
    for j in range(DFT_RADIX):
        p[DFT_RADIX * np.arange(per) + j, j * per + np.arange(per)] = 1.0
    return p


def _fourier_body(f_ref, w1_ref, tab_ref, o_ref, yc_ref, ys_ref):
    groups, c, _ = w1_ref.shape
    quarter = f_ref.shape[1] // DFT_RADIX
    for g in range(groups):
        yg = _dot(f_ref[0, :, g * c:(g + 1) * c], w1_ref[g])
        yc_ref[:, g * c:(g + 1) * c] = yg[:, :c]
        ys_ref[:, g * c:(g + 1) * c] = yg[:, c:]
    for j in range(DFT_RADIX):
        pc, ps = [], []
        for q in range(DFT_RADIX):
            rows = slice(q * quarter, (q + 1) * quarter)
            a = (j * q) % 4
            if a == 0:
                pc.append((1, yc_ref, rows)), ps.append((1, ys_ref, rows))
            elif a == 1:
                pc.append((-1, ys_ref, rows)), ps.append((1, yc_ref, rows))
            elif a == 2:
                pc.append((-1, yc_ref, rows)), ps.append((-1, ys_ref, rows))
            else:
                pc.append((1, ys_ref, rows)), ps.append((-1, yc_ref, rows))
        folded = []
        for terms in (pc, ps):
            terms = sorted(terms, key=lambda t: -t[0])
            acc = terms[0][1][terms[0][2], :]
            for sign, ref, rows in terms[1:]:
                acc = acc + ref[rows, :] if sign > 0 else acc - ref[rows, :]
            folded.append(acc.astype(BF16))
        o_ref[0, j] = _dot(tab_ref[j], jnp.concatenate(folded, axis=0)).astype(o_ref.dtype)


def _fourier_mix(f, w1, tab):
    b, s, w = f.shape
    quarter = s // DFT_RADIX
    return pl.pallas_call(
        _fourier_body,
        grid=(b,),
        in_specs=[pl.BlockSpec((1, s, w), lambda bi: (bi, 0, 0)),
                  pl.BlockSpec(w1.shape, lambda bi: (0, 0, 0)),
                  pl.BlockSpec(tab.shape, lambda bi: (0, 0, 0))],
        out_specs=pl.BlockSpec((1, DFT_RADIX, quarter, w), lambda bi: (bi, 0, 0, 0)),
        out_shape=jax.ShapeDtypeStruct((b, DFT_RADIX, quarter, w), BF16),
        scratch_shapes=[pltpu.VMEM((s, w), F32), pltpu.VMEM((s, w), F32)],
        compiler_params=_params(("arbitrary",)),
        name="fourier_mix",
    )(f, w1, tab)


def _na_tile_geometry(rows_total):
    last_r0 = rows_total - NA_TILE_ROWS
    return ((0, 0), (2 * NA_TILE_ROWS, 2 * NA_TILE_ROWS - NA_ROWS // 2), (last_r0, rows_total - NA_WIN_ROWS))


def _na_bias_body(rows_total, rpb_ref, o_ref, t2_ref):
    h = pl.program_id(0)
    wq = lax.broadcasted_iota(jnp.int32, (GRID_W, LANES), 0)
    wk = lax.broadcasted_iota(jnp.int32, (GRID_W, LANES), 1) & (GRID_W - 1)
    cs = jnp.clip(wq - NA_COLS // 2, 0, GRID_W - NA_COLS)
    col_ok = (wk >= cs) & (wk < cs + NA_COLS)
    dc = wk - wq + (NA_COLS - 1)
    n_dr, n_dc = 2 * NA_ROWS - 1, 2 * NA_COLS - 1
    neg = jnp.full((GRID_W, LANES), NEG_INF, F32)
    for dr in range(n_dr):
        t2_ref[dr] = neg
    for c in range(n_dc):
        hit = col_ok & (dc == c)
        for dr in range(n_dr):
            t2_ref[dr] = jnp.where(hit, rpb_ref[h * (n_dr * n_dc) + dr * n_dc + c] * LOG2E, t2_ref[dr])
    lane = lax.broadcasted_iota(jnp.int32, (GRID_W, LANES), 1)
    left = lane < GRID_W
    for ty, (r0, ws) in enumerate(_na_tile_geometry(rows_total)):
        for i in range(NA_TILE_ROWS):
            r = r0 + i
            rs = min(max(r - NA_ROWS // 2, 0), rows_total - NA_ROWS)
            for jp in range(NA_WIN_ROWS // 2):
                halves = []
                for kr in (ws + 2 * jp, ws + 2 * jp + 1):
                    halves.append(t2_ref[kr - r + NA_ROWS - 1] if rs <= kr < rs + NA_ROWS else neg)
                o_ref[0, ty, i * GRID_W:(i + 1) * GRID_W, jp * LANES:(jp + 1) * LANES] = jnp.where(left, halves[0], halves[1])


def _na_bias_tables(rpb, rows_total):
    h = rpb.shape[0]
    q_rows, k_cols = NA_TILE_ROWS * GRID_W, NA_WIN_ROWS * GRID_W
    return pl.pallas_call(
        functools.partial(_na_bias_body, rows_total),
        grid=(h,),
        in_specs=[pl.BlockSpec(memory_space=pltpu.SMEM)],
        out_specs=pl.BlockSpec((1, 3, q_rows, k_cols), lambda hi: (hi, 0, 0, 0)),
        out_shape=jax.ShapeDtypeStruct((h, 3, q_rows, k_cols), F32),
        scratch_shapes=[pltpu.VMEM((2 * NA_ROWS - 1, GRID_W, LANES), F32)],
        compiler_params=_params(("arbitrary",)),
        name="na_bias_tables",
    )(rpb.reshape(-1))


def _na_body(rows_total, q_ref, k_ref, v_ref, kc_ref, vc_ref, bias_ref, o_ref):
    t = pl.program_id(2)
    n_t = pl.num_programs(2)
    ty = jnp.where(t == 0, 0, jnp.where(t == n_t - 1, 2, 1))
    ws = jnp.clip(t * NA_TILE_ROWS - NA_ROWS // 2, 0, rows_total - NA_WIN_ROWS)
    off = pl.multiple_of(ws * GRID_W, GRID_W)
    n_win = NA_WIN_ROWS * GRID_W
    tq = q_ref.shape[1]
    part = tq // NA_QUERY_SPLIT
    for bb in range(q_ref.shape[0]):
        for qs in range(NA_QUERY_SPLIT):
            rows = slice(qs * part, (qs + 1) * part)
            bias = jnp.concatenate([bias_ref[0, ty, rows, :], bias_ref[1, ty, rows, :]], axis=0)
            chunks = [(kc_ref[bb], vc_ref[bb], None),
                      (k_ref[bb, pl.ds(off, n_win), :], v_ref[bb, pl.ds(off, n_win), :], bias)]
            o = _softmax_pv(_split_heads(q_ref[bb, rows, :]), chunks)
            o_ref[bb, rows, :] = o.astype(o_ref.dtype)


def _neighborhood_attention(q, k, v, kc, vc, bias):
    b, s, w = q.shape
    l = kc.shape[1]
    rows_total = s // GRID_W
    tq = NA_TILE_ROWS * GRID_W
    nb = NA_BATCH_PER_STEP
    full = lambda j, bi, t: (bi, 0, j)
    return pl.pallas_call(
        functools.partial(_na_body, rows_total),
        grid=(w // LANES, b // nb, s // tq),
        in_specs=[pl.BlockSpec((nb, tq, LANES), lambda j, bi, t: (bi, t, j)),
                  pl.BlockSpec((nb, s, LANES), full),
                  pl.BlockSpec((nb, s, 2 * LANES), full),
                  pl.BlockSpec((nb, l, LANES), full),
                  pl.BlockSpec((nb, l, 2 * LANES), full),
                  pl.BlockSpec((2,) + bias.shape[1:], lambda j, bi, t: (j, 0, 0, 0))],
        out_specs=pl.BlockSpec((nb, tq, LANES), lambda j, bi, t: (bi, t, j)),
        out_shape=jax.ShapeDtypeStruct((b, s, w), BF16),
        compiler_params=_params(("arbitrary", "arbitrary", "arbitrary")),
        name="neighborhood_attention",
    )(q, k, v, kc, vc, bias)


def _modulation(m, rows, batch):
    d = m.shape[1] // 3
    pick = lambda lo: jnp.broadcast_to(m[rows, lo:lo + d].reshape(-1, 1, d), (batch, 1, d))
    return pick(0), 1.0 + pick(d), pick(2 * d)


def kernel(x, c, ctx, c_ctx, ab_w_ada, ab_b_ada, ab_norm_g, ab_w_in, ab_conv_w, ab_conv_b, ab_ln_g, ab_ln_b, ab_q_norm_g, ab_k_norm_g, ab_w_out, cd_w_ada, cd_b_ada, cd_norm_g, cd_w_in, cd_w_fourier, cd_rpb, cd_w_out, final_norm_g):
    b, s, d = x.shape
    l = ctx.shape[1]
    assert d == D_MODEL and s % (NA_TILE_ROWS * GRID_W) == 0 and ab_w_ada.shape[0] == 1 and cd_w_ada.shape[0] == 1

    cond = jnp.zeros((16, d), F32).at[:b].set(c).at[b].set(c_ctx)
    lat = slice(0, b)
    cx = slice(b, b + 1)

    m0 = _adaln(cond, ab_w_ada[0], ab_b_ada[0])
    m1 = _adaln(cond, cd_w_ada[0], cd_b_ada[0])
    sh, s1p, gate = _modulation(m0, lat, b)
    csh, cs1p, cgate = _modulation(m0, cx, b)
    sh1, s1p1, gate1 = _modulation(m1, lat, b)
    csh1, cs1p1, _ = _modulation(m1, cx, b)
    ng1 = cd_norm_g[0].reshape(1, d)
    w_in1 = cd_w_in[0].astype(BF16)

    w_in0 = ab_w_in[0].astype(BF16)
    w_out0 = ab_w_out[0].astype(BF16)

    cos, sin = _rope_tables(s)
    cos, sin = jnp.asarray(cos), jnp.asarray(sin)
    mq = jnp.asarray(_block_ones(512)).astype(BF16)
    mk = jnp.asarray(_block_ones(128)).astype(BF16)
    qg = jnp.tile(ab_q_norm_g[0], 512 // HEAD_DIM).reshape(1, 512)
    kg = jnp.tile(ab_k_norm_g[0], 128 // HEAD_DIM).reshape(1, 128)
    ng0 = ab_norm_g[0].reshape(1, d)

    ck, cv, kc, vc = _ctx_layer(ctx, ng0, cs1p, csh, w_in0, qg, kg, mq, mk, ab_conv_w[0], ab_conv_b[0], ab_ln_g[0],
                                ab_ln_b[0], w_out0, cgate, ng1, cs1p1, csh1, w_in1, 1)

    glu, q, k, v, sz = _inproj_ab(x, ng0, s1p, sh, w_in0, qg, kg, cos, sin, mq, mk, 1024)
    y_b = _gqa_attention(q, [ck, k], [cv, v], 1024, 1, chunk=128)
    y_a = _conformer_conv(glu, ab_conv_w[0], ab_conv_b[0], ab_ln_g[0], ab_ln_b[0], 1024)

    x1, f, q, k, v, sz = _outproj_inproj_cd(y_a, y_b, sz, w_out0, x, gate, ng1, s1p1, sh1, w_in1, 1024)
    gate = gate1

    c_c, s_c = _dft_tables(cd_w_fourier.shape[-1])
    w1 = _fourier_weights(jnp.asarray(c_c), jnp.asarray(s_c), cd_w_fourier[0])
    y_c = _fourier_mix(f, w1, jnp.asarray(_fourier_tables(s)).astype(BF16))

    bias = _na_bias_tables(cd_rpb[0], s // GRID_W)
    y_d = _neighborhood_attention(q, k, v, kc, vc, bias)

    return _outproj_final(y_c, y_d, sz, cd_w_out[0].astype(BF16), x1, gate, final_norm_g.reshape(1, d), 1024)
```

```python
import functools
import math

import numpy as np
import jax
import jax.numpy as jnp
from jax import lax
from jax.experimental import pallas as pl
from jax.experimental.pallas import tpu as pltpu

F32 = jnp.float32
BF16 = jnp.bfloat16
HIGHEST = lax.Precision.HIGHEST

D_MODEL = 1024
HEAD_DIM = 64
GRID_W = 64
CONV_WIDTH = 31
NA_ROWS = 8
NA_COLS = 16
ROPE_THETA = 10000.0
EPS = 1e-6
NEG_INF = -1e30
LOG2E = math.log2(math.e)
Q_SCALE = HEAD_DIM ** -0.5 * LOG2E
LANES = 128
SUBLANES = 8
HALO = 16
VMEM_LIMIT = 48 * 1024 * 1024

NA_TILE_ROWS = 4
NA_WIN_ROWS = NA_TILE_ROWS + NA_ROWS
NA_BATCH_PER_STEP = 8
NA_QUERY_SPLIT = 2
IN_PROJ_SUBTILE = 256
MID_SUBTILE = 1024
OUT_PROJ_SUBTILE = 512
DFT_RADIX = 4
V_ROWS = 80


def _params(sem, vmem_mib=None):
    limit = VMEM_LIMIT if vmem_mib is None else vmem_mib * 1024 * 1024
    return pltpu.CompilerParams(dimension_semantics=sem, vmem_limit_bytes=limit)


def _silu(t):
    return t * jax.nn.sigmoid(t)


def _dot(a, b):
    return jnp.dot(a, b, preferred_element_type=F32)


def _dot_nt(a, b):
    return lax.dot_general(a, b, (((1,), (1,)), ((), ())), preferred_element_type=F32)


def _rope_tables(seq):
    t = np.arange(seq)
    row = (t // GRID_W).astype(np.float64)
    col = (t % GRID_W).astype(np.float64)
    nf = HEAD_DIM // 4
    inv = np.float32(ROPE_THETA) ** (-np.arange(nf, dtype=np.float32) / nf)
    ang = np.concatenate([row[:, None] * inv, col[:, None] * inv], axis=-1)
    cos = np.repeat(np.cos(ang), 2, axis=-1)
    sin = np.repeat(np.sin(ang), 2, axis=-1)
    sign = np.tile(np.array([-1.0, 1.0]), HEAD_DIM // 2)
    cos = np.tile(cos, (1, LANES // HEAD_DIM))
    sin = np.tile(sin * sign, (1, LANES // HEAD_DIM))
    return cos.astype(np.float32), sin.astype(np.float32)


def _dft_tables(n):
    k = np.arange(n)
    ang = 2.0 * np.pi * ((k[:, None] * k[None, :]) % n) / n
    s = 1.0 / math.sqrt(n)
    return (np.cos(ang) * s).astype(np.float32), (np.sin(ang) * s).astype(np.float32)


def _block_ones(width):
    h = np.arange(width) // HEAD_DIM
    return (h[:, None] == h[None, :]).astype(np.float32)


def _adaln_body(c_ref, w_ref, b_ref, o_ref):
    a = _silu(c_ref[...]).astype(BF16)
    o_ref[...] = _dot(a, w_ref[...].astype(BF16)) + b_ref[...]


def _adaln(cond, w, b):
    r, d = cond.shape
    n = w.shape[1]
    tn = 1024
    return pl.pallas_call(
        _adaln_body,
        grid=(n // tn,),
        in_specs=[pl.BlockSpec((r, d), lambda j: (0, 0)),
                  pl.BlockSpec((d, tn), lambda j: (0, j)),
                  pl.BlockSpec((1, tn), lambda j: (0, j))],
        out_specs=pl.BlockSpec((r, tn), lambda j: (0, j)),
        out_shape=jax.ShapeDtypeStruct((r, n), F32),
        compiler_params=_params(("arbitrary",), 12),
        name="adaln",
    )(cond, w, b.reshape(1, n))


def _head_rms(t, ones_ref, gain):
    ms = _dot((t * t).astype(BF16), ones_ref[...]) * (1.0 / HEAD_DIM)
    return t * lax.rsqrt(ms + EPS) * gain


def _rope(t, cos, sin_signed):
    rows = t.shape[0]
    lane = lax.broadcasted_iota(jnp.int32, (rows, LANES), 1)
    even = (lane & 1) == 0
    outs = []
    for c in range(t.shape[1] // LANES):
        xc = t[:, c * LANES:(c + 1) * LANES]
        swapped = jnp.where(even, pltpu.roll(xc, LANES - 1, 1), pltpu.roll(xc, 1, 1))
        outs.append(xc * cos + swapped * sin_signed)
    return outs[0] if len(outs) == 1 else jnp.concatenate(outs, axis=1)


def _modulate(x, g_ref, s1p_ref, sh_ref):
    y = x * lax.rsqrt(jnp.mean(x * x, axis=-1, keepdims=True) + EPS) * g_ref[...]
    return (y * s1p_ref[0] + sh_ref[0]).astype(BF16)


def _row_subtiles(tm, sub):
    sub = min(tm, sub)
    return [slice(r, r + sub) for r in range(0, tm, sub)]


def _inproj_ab_body(x_ref, g_ref, s1p_ref, sh_ref, w_ref, qg_ref, kg_ref, cos_ref, sin_ref, mq_ref, mk_ref,
                    glu_ref, q_ref, k_ref, v_ref, sz_ref):
    for rows in _row_subtiles(x_ref.shape[1], IN_PROJ_SUBTILE):
        u = _dot(_modulate(x_ref[0, rows, :], g_ref, s1p_ref, sh_ref), w_ref[...])
        glu, q, k, v, sz = _ab_epilogue(u, mq_ref, mk_ref, qg_ref, kg_ref, (cos_ref[rows, :], sin_ref[rows, :]))
        glu_ref[0, rows, :] = glu
        q_ref[0, rows, :] = q
        k_ref[0, rows, :] = k
        v_ref[0, rows, :] = v
        sz_ref[0, rows, :] = sz


def _ab_epilogue(u, mq_ref, mk_ref, qg_ref, kg_ref, rope_tabs):
    glu = u[:, 0:512] * jax.nn.sigmoid(u[:, 512:1024])
    q = _head_rms(u[:, 1024:1536], mq_ref, qg_ref[...])
    k = _head_rms(u[:, 1536:1664], mk_ref, kg_ref[...])
    if rope_tabs is not None:
        q = _rope(q, *rope_tabs)
        k = _rope(k, *rope_tabs)
    v = u[:, 1664:1792]
    k_sw, v_sw = pltpu.roll(k, HEAD_DIM, 1), pltpu.roll(v, HEAD_DIM, 1)
    left = lax.broadcasted_iota(jnp.int32, k.shape, 1) < HEAD_DIM
    one = jnp.ones_like(v)
    k_dup = jnp.concatenate([jnp.where(left, k, k_sw), jnp.where(left, k_sw, k)], axis=1)
    v_dup = jnp.concatenate([jnp.where(left, v, one), jnp.where(left, one, v_sw),
                             jnp.where(left, v_sw, one), jnp.where(left, one, v)], axis=1)
    return (glu, (q * Q_SCALE).astype(BF16), k_dup.astype(BF16), v_dup.astype(BF16),
            _silu(u[:, 1792:2816]).astype(BF16))


def _inproj_ab(x, g, s1p, sh, w, qg, kg, cos, sin, mq, mk, tm):
    b, s, d = x.shape
    n = w.shape[1]
    row = lambda bi, i: (bi, i, 0)
    const = lambda bi, i: (0, 0)
    per_b = lambda bi, i: (bi, 0, 0)
    tab = lambda bi, i: (i, 0)
    return pl.pallas_call(
        _inproj_ab_body,
        grid=(b, s // tm),
        in_specs=[pl.BlockSpec((1, tm, d), row),
                  pl.BlockSpec((1, d), const),
                  pl.BlockSpec((1, 1, d), per_b),
                  pl.BlockSpec((1, 1, d), per_b),
                  pl.BlockSpec((d, n), const),
                  pl.BlockSpec((1, 512), const),
                  pl.BlockSpec((1, 128), const),
                  pl.BlockSpec((tm, LANES), tab),
                  pl.BlockSpec((tm, LANES), tab),
                  pl.BlockSpec((512, 512), const),
                  pl.BlockSpec((128, 128), const)],
        out_specs=[pl.BlockSpec((1, tm, 512), row),
                   pl.BlockSpec((1, tm, 512), row),
                   pl.BlockSpec((1, tm, 256), row),
                   pl.BlockSpec((1, tm, 512), row),
                   pl.BlockSpec((1, tm, 1024), row)],
        out_shape=[jax.ShapeDtypeStruct((b, s, 512), F32),
                   jax.ShapeDtypeStruct((b, s, 512), BF16),
                   jax.ShapeDtypeStruct((b, s, 256), BF16),
                   jax.ShapeDtypeStruct((b, s, 512), BF16),
                   jax.ShapeDtypeStruct((b, s, 1024), BF16)],
        compiler_params=_params(("arbitrary", "arbitrary"), 40),
        name="inproj_ab",
    )(x, g, s1p, sh, w, qg, kg, cos, sin, mq, mk)


def _outproj_inproj_cd_body(ya_ref, yb_ref, sz0_ref, wo_ref, x_ref, gate_ref, g_ref, s1p_ref, sh_ref, wi_ref,
                            x1_ref, f_ref, q_ref, k_ref, v_ref, sz_ref):
    for rows in _row_subtiles(x_ref.shape[1], MID_SUBTILE):
        y = (jnp.concatenate([ya_ref[0, rows, :], yb_ref[0, rows, :]], axis=-1).astype(F32)
             * sz0_ref[0, rows, :].astype(F32))
        xn = x_ref[0, rows, :] + gate_ref[0] * _dot(y.astype(BF16), wo_ref[...])
        x1_ref[0, rows, :] = xn
        u = _dot(_modulate(xn, g_ref, s1p_ref, sh_ref), wi_ref[...])
        f_ref[0, rows, :] = u[:, 0:512].astype(BF16)
        q_ref[0, rows, :] = (u[:, 512:1024] * Q_SCALE).astype(BF16)
        k_ref[0, rows, :] = u[:, 1024:1536].astype(BF16)
        v_ref[0, rows, :] = _pairs_with_ones(u[:, 1536:2048]).astype(BF16)
        sz_ref[0, rows, :] = _silu(u[:, 2048:3072]).astype(BF16)


def _outproj_inproj_cd(ya, yb, sz0, w_out, x, gate, g, s1p, sh, w_in, tm):
    b, s, d = x.shape
    n = w_in.shape[1]
    row = lambda bi, i: (bi, i, 0)
    const = lambda bi, i: (0, 0)
    per_b = lambda bi, i: (bi, 0, 0)
    return pl.pallas_call(
        _outproj_inproj_cd_body,
        grid=(b, s // tm),
        in_specs=[pl.BlockSpec((1, tm, 512), row),
                  pl.BlockSpec((1, tm, 512), row),
                  pl.BlockSpec((1, tm, 1024), row),
                  pl.BlockSpec((1024, d), const, pipeline_mode=pl.Buffered(1)),
                  pl.BlockSpec((1, tm, d), row),
                  pl.BlockSpec((1, 1, d), per_b),
                  pl.BlockSpec((1, d), const),
                  pl.BlockSpec((1, 1, d), per_b),
                  pl.BlockSpec((1, 1, d), per_b),
                  pl.BlockSpec((d, n), const, pipeline_mode=pl.Buffered(1))],
        out_specs=[pl.BlockSpec((1, tm, d), row)] + [pl.BlockSpec((1, tm, 512), row)] * 3
        + [pl.BlockSpec((1, tm, 1024), row)] * 2,
        out_shape=[jax.ShapeDtypeStruct((b, s, d), F32)] + [jax.ShapeDtypeStruct((b, s, 512), BF16)] * 3
        + [jax.ShapeDtypeStruct((b, s, 1024), BF16)] * 2,
        compiler_params=_params(("arbitrary", "arbitrary")),
        name="outproj_inproj_cd",
    )(ya, yb, sz0, w_out, x, gate, g, s1p, sh, w_in)


def _split_heads(q):
    lane = lax.broadcasted_iota(jnp.int32, q.shape, 1)
    zero = jnp.zeros_like(q)
    return jnp.concatenate([jnp.where(lane < HEAD_DIM, q, zero), jnp.where(lane >= HEAD_DIM, q, zero)], axis=0)


def _with_ones(v):
    lane = lax.broadcasted_iota(jnp.int32, v.shape, 1)
    one = jnp.ones_like(v)
    return jnp.concatenate([jnp.where(lane < HEAD_DIM, v, one), jnp.where(lane < HEAD_DIM, one, v)], axis=1)


def _pairs_with_ones(v):
    return jnp.concatenate([_with_ones(v[:, c:c + LANES]) for c in range(0, v.shape[1], LANES)], axis=1)


def _pv(p, v2, rows):
    pv = _dot(p.astype(BF16), v2)
    return jnp.concatenate([pv[:rows, :LANES], pv[rows:, LANES:]], axis=0)


def _softmax_pv(q2, chunks):
    rows = q2.shape[0] // 2
    scores = [_dot_nt(q2, k) if bias is None else _dot_nt(q2, k) + bias for k, _, bias in chunks]
    m = functools.reduce(jnp.maximum, [jnp.max(s, axis=-1, keepdims=True) for s in scores])
    acc = functools.reduce(jnp.add, [_pv(jnp.exp2(s - m), v2, rows) for s, (_, v2, _) in zip(scores, chunks)])
    left = lax.broadcasted_iota(jnp.int32, (rows, LANES), 1) < HEAD_DIM
    top, bot = acc[:rows], acc[rows:]
    return jnp.where(left, top, bot) / pltpu.roll(jnp.where(left, bot, top), HEAD_DIM, 1)


def _softmax_pv_keys_major(q2, chunks):
    half = q2.shape[0] // 2
    scores = [_dot_nt(k, q2) for k, _ in chunks]
    m = acc = None
    for s, (_, v_ones) in zip(scores, chunks):
        mc = jnp.max(s, axis=0, keepdims=True)
        m_new = mc if m is None else jnp.maximum(m, mc)
        pv = _dot(v_ones.T[:V_ROWS], jnp.exp2(s - m_new).astype(BF16))
        acc = pv if m is None else jnp.exp2(m - m_new) * acc + pv
        m = m_new
    o = acc[:HEAD_DIM] / acc[HEAD_DIM:HEAD_DIM + 1]
    return jnp.concatenate([o[:, :half], o[:, half:]], axis=0).T


def _key_chunks(k_ref, v_ref, kv, chunk):
    n = k_ref.shape[1]
    kl, vl = slice(kv * LANES, (kv + 1) * LANES), slice(kv * 2 * LANES, (kv + 1) * 2 * LANES)
    return [(k_ref[0, c:c + chunk, kl], v_ref[0, c:c + chunk, vl], None) for c in range(0, n, chunk)]


def _gqa_body(n_src, chunk, q_ref, *refs):
    k_refs, v_refs, o_ref = refs[:n_src], refs[n_src:2 * n_src], refs[2 * n_src]
    pairs = q_ref.shape[2] // LANES
    kv_heads = k_refs[0].shape[2] // LANES
    for p in range(pairs):
        lanes = slice(p * LANES, (p + 1) * LANES)
        kv = p * kv_heads // pairs
        chunks = [c for k, v in zip(k_refs, v_refs) for c in _key_chunks(k, v, kv, min(chunk, k.shape[1]))]
        chunks = [(k, v2[:, :LANES]) for k, v2, _ in chunks]
        o_ref[0, :, lanes] = _softmax_pv_keys_major(_split_heads(q_ref[0, :, lanes]), chunks).astype(o_ref.dtype)


def _gqa_attention(q, ks, vs, tq, pairs, chunk=512):
    b, lq, w = q.shape
    n_src = len(ks)
    n_pairs = w // LANES
    kv_heads = ks[0].shape[2] // LANES
    kv_blk = max(1, pairs * kv_heads // n_pairs)
    kv_idx = lambda bi, i, j: (bi, 0, j * pairs * kv_heads // n_pairs // kv_blk)
    k_specs = [pl.BlockSpec((1, k.shape[1], kv_blk * LANES), kv_idx) for k in ks]
    v_specs = [pl.BlockSpec((1, v.shape[1], kv_blk * 2 * LANES), kv_idx) for v in vs]
    return pl.pallas_call(
        functools.partial(_gqa_body, n_src, chunk),
        grid=(b, lq // tq, n_pairs // pairs),
        in_specs=[pl.BlockSpec((1, tq, pairs * LANES), lambda bi, i, j: (bi, i, j))] + k_specs + v_specs,
        out_specs=pl.BlockSpec((1, tq, pairs * LANES), lambda bi, i, j: (bi, i, j)),
        out_shape=jax.ShapeDtypeStruct((b, lq, w), BF16),
        compiler_params=_params(("arbitrary", "arbitrary", "arbitrary"), 32),
        name=f"gqa_attention_{n_src}src",
    )(q, *ks, *vs)


def _conv_core(rows, pad_ref, shift_ref, w_ref, cb_ref, lg_ref, lb_ref, store):
    span = shift_ref.shape[1]
    for r in range(1, SUBLANES):
        shift_ref[r] = pad_ref[r:r + span, :]
    sub = 32
    base = HALO - CONV_WIDTH // 2
    for r0 in range(0, rows, sub):
        acc = jnp.zeros((sub, pad_ref.shape[1]), F32) + cb_ref[...]
        for k in range(CONV_WIDTH):
            a, r = divmod(base + k, SUBLANES)
            lo = a * SUBLANES + r0
            tap = pad_ref[lo:lo + sub, :] if r == 0 else shift_ref[r, lo:lo + sub, :]
            acc = acc + tap * jnp.concatenate([w_ref[k]] * (sub // SUBLANES), axis=0)
        mu = jnp.mean(acc, axis=-1, keepdims=True)
        xc = acc - mu
        y = xc * lax.rsqrt(jnp.mean(xc * xc, axis=-1, keepdims=True) + EPS) * lg_ref[...] + lb_ref[...]
        store(r0, sub, _silu(y))


def _conv_body(rows, x_ref, prev_ref, next_ref, w_ref, cb_ref, lg_ref, lb_ref, o_ref, pad_ref, shift_ref):
    i = pl.program_id(1)
    last = pl.num_programs(1) - 1
    pad_ref[0:HALO, :] = jnp.where(i > 0, prev_ref[0], 0.0)
    pad_ref[HALO:HALO + rows, :] = x_ref[0]
    pad_ref[HALO + rows:HALO + rows + HALO, :] = jnp.where(i < last, next_ref[0], 0.0)

    def store(r0, sub, y):
        o_ref[0, r0:r0 + sub, :] = y.astype(o_ref.dtype)

    _conv_core(rows, pad_ref, shift_ref, w_ref, cb_ref, lg_ref, lb_ref, store)


def _conformer_conv(u, conv_w, conv_b, ln_g, ln_b, rows):
    b, l, c = u.shape
    hb = rows // HALO
    n_halo = l // HALO
    return pl.pallas_call(
        functools.partial(_conv_body, rows),
        grid=(b, l // rows),
        in_specs=[pl.BlockSpec((1, rows, c), lambda bi, i: (bi, i, 0)),
                  pl.BlockSpec((1, HALO, c), lambda bi, i: (bi, jnp.maximum(i * hb - 1, 0), 0)),
                  pl.BlockSpec((1, HALO, c), lambda bi, i: (bi, jnp.minimum((i + 1) * hb, n_halo - 1), 0)),
                  pl.BlockSpec((CONV_WIDTH, SUBLANES, c), lambda bi, i: (0, 0, 0)),
                  pl.BlockSpec((1, c), lambda bi, i: (0, 0)),
                  pl.BlockSpec((1, c), lambda bi, i: (0, 0)),
                  pl.BlockSpec((1, c), lambda bi, i: (0, 0))],
        out_specs=pl.BlockSpec((1, rows, c), lambda bi, i: (bi, i, 0)),
        out_shape=jax.ShapeDtypeStruct((b, l, c), BF16),
        scratch_shapes=[pltpu.VMEM((rows + 2 * HALO, c), F32),
                        pltpu.VMEM((SUBLANES, rows + 2 * HALO - SUBLANES, c), F32)],
        compiler_params=_params(("arbitrary", "arbitrary"), 40),
        name="conformer_conv",
    )(u, u, u, jnp.broadcast_to(conv_w[:, None, :], (CONV_WIDTH, SUBLANES, c)),
      conv_b.reshape(1, c), ln_g.reshape(1, c), ln_b.reshape(1, c))


def _ctx_layer_body(x_ref, g0_ref, s1p0_ref, sh0_ref, wi0_ref, qg_ref, kg_ref, mq_ref, mk_ref, cw_ref, cb_ref,
                    lg_ref, lb_ref, wo0_ref, gate0_ref, g1_ref, s1p1_ref, sh1_ref, wkv1_ref,
                    ck_ref, cv_ref, kc_ref, vc_ref, pad_ref, shift_ref, ya_ref):
    rows = x_ref.shape[1]
    x = x_ref[0]
    u = _dot(_modulate(x, g0_ref, s1p0_ref, sh0_ref), wi0_ref[...])
    glu, q, k_dup, v_dup, sz = _ab_epilogue(u, mq_ref, mk_ref, qg_ref, kg_ref, None)
    ck_ref[0] = k_dup
    cv_ref[0] = v_dup
    pairs, kv_heads = q.shape[1] // LANES, k_dup.shape[1] // LANES
    yb = []
    for p in range(pairs):
        kv = p * kv_heads // pairs
        keys = [(k_dup[:, kv * LANES:(kv + 1) * LANES], v_dup[:, kv * 2 * LANES:(kv + 1) * 2 * LANES], None)]
        yb.append(_softmax_pv(_split_heads(q[:, p * LANES:(p + 1) * LANES]), keys))
    zeros = jnp.zeros((HALO, glu.shape[1]), F32)
    pad_ref[0:HALO, :] = zeros
    pad_ref[HALO:HALO + rows, :] = glu
    pad_ref[HALO + rows:HALO + rows + HALO, :] = zeros

    def store(r0, sub, y):
        ya_ref[r0:r0 + sub, :] = y

    _conv_core(rows, pad_ref, shift_ref, cw_ref, cb_ref, lg_ref, lb_ref, store)
    y = jnp.concatenate([ya_ref[...]] + yb, axis=-1) * sz.astype(F32)
    x1 = x + gate0_ref[0] * _dot(y.astype(BF16), wo0_ref[...])
    u1 = _dot(_modulate(x1, g1_ref, s1p1_ref, sh1_ref), wkv1_ref[...])
    kc_ref[0] = u1[:, 0:512].astype(BF16)
    vc_ref[0] = _pairs_with_ones(u1[:, 512:1024]).astype(BF16)


def _ctx_layer(ctx, g0, s1p0, sh0, w_in0, qg, kg, mq, mk, conv_w, conv_b, ln_g, ln_b, w_out0, gate0,
               g1, s1p1, sh1, w_in1, kv_col_block):
    b, l, d = ctx.shape
    c = conv_w.shape[1]
    row = lambda bi: (bi, 0, 0)
    const2 = lambda bi: (0, 0)
    vec = pl.BlockSpec((1, d), const2)
    per_b = pl.BlockSpec((1, 1, d), row)
    cvec = pl.BlockSpec((1, c), const2)
    return pl.pallas_call(
        _ctx_layer_body,
        grid=(b,),
        in_specs=[pl.BlockSpec((1, l, d), row), vec, per_b, per_b,
                  pl.BlockSpec(w_in0.shape, const2),
                  pl.BlockSpec((1, 512), const2), pl.BlockSpec((1, 128), const2),
                  pl.BlockSpec((512, 512), const2), pl.BlockSpec((128, 128), const2),
                  pl.BlockSpec((CONV_WIDTH, SUBLANES, c), lambda bi: (0, 0, 0)), cvec, cvec, cvec,
                  pl.BlockSpec(w_out0.shape, const2), per_b,
                  vec, per_b, per_b,
                  pl.BlockSpec((d, 1024), lambda bi: (0, kv_col_block))],
        out_specs=[pl.BlockSpec((1, l, 256), row), pl.BlockSpec((1, l, 512), row),
                   pl.BlockSpec((1, l, 512), row), pl.BlockSpec((1, l, 1024), row)],
        out_shape=[jax.ShapeDtypeStruct((b, l, 256), BF16), jax.ShapeDtypeStruct((b, l, 512), BF16),
                   jax.ShapeDtypeStruct((b, l, 512), BF16), jax.ShapeDtypeStruct((b, l, 1024), BF16)],
        scratch_shapes=[pltpu.VMEM((l + 2 * HALO, c), F32),
                        pltpu.VMEM((SUBLANES, l + 2 * HALO - SUBLANES, c), F32),
                        pltpu.VMEM((l, c), F32)],
        compiler_params=_params(("arbitrary",), 24),
        name="ctx_layer",
    )(ctx, g0, s1p0, sh0, w_in0, qg, kg, mq, mk,
      jnp.broadcast_to(conv_w[:, None, :], (CONV_WIDTH, SUBLANES, c)), conv_b.reshape(1, c), ln_g.reshape(1, c),
      ln_b.reshape(1, c), w_out0, gate0, g1, s1p1, sh1, w_in1)


def _outproj_final_body(ya_ref, yb_ref, sz_ref, w_ref, x_ref, gate_ref, perm_ref, fg_ref, o_ref):
    sub = perm_ref.shape[0]
    per = sub // DFT_RADIX
    for h in range(x_ref.shape[1] // sub):
        rows = slice(h * sub, (h + 1) * sub)
        stacked = jnp.concatenate([ya_ref[0, j, h * per:(h + 1) * per, :] for j in range(DFT_RADIX)], axis=0)
        ya = _dot(perm_ref[...], stacked)
        y = jnp.concatenate([ya, yb_ref[0, rows, :].astype(F32)], axis=-1) * sz_ref[0, rows, :].astype(F32)
        xn = x_ref[0, rows, :] + gate_ref[0] * _dot(y.astype(BF16), w_ref[...])
        o_ref[0, rows, :] = xn * lax.rsqrt(jnp.mean(xn * xn, axis=-1, keepdims=True) + EPS) * fg_ref[...]


def _outproj_final(ya, yb, sz, w, x, gate, final_g, tm):
    b, s, d = x.shape
    row = lambda bi, i: (bi, i, 0)
    const = lambda bi, i: (0, 0)
    return pl.pallas_call(
        _outproj_final_body,
        grid=(b, s // tm),
        in_specs=[pl.BlockSpec((1, DFT_RADIX, tm // DFT_RADIX, 512), lambda bi, i: (bi, 0, i, 0)),
                  pl.BlockSpec((1, tm, 512), row),
                  pl.BlockSpec((1, tm, 1024), row),
                  pl.BlockSpec((1024, d), const),
                  pl.BlockSpec((1, tm, d), row),
                  pl.BlockSpec((1, 1, d), lambda bi, i: (bi, 0, 0)),
                  pl.BlockSpec((OUT_PROJ_SUBTILE, OUT_PROJ_SUBTILE), const),
                  pl.BlockSpec((1, d), const)],
        out_specs=pl.BlockSpec((1, tm, d), row),
        out_shape=jax.ShapeDtypeStruct((b, s, d), F32),
        compiler_params=_params(("arbitrary", "arbitrary"), 36),
        name="outproj_final",
    )(ya, yb, sz, w, x, gate, jnp.asarray(_class_row_permutation(OUT_PROJ_SUBTILE)).astype(BF16), final_g)


def _fourier_w_body(cc_ref, sc_ref, w_ref, o_ref):
    for g in range(w_ref.shape[0]):
        wc = jnp.dot(cc_ref[...], w_ref[g], preferred_element_type=F32, precision=HIGHEST)
        ws = jnp.dot(sc_ref[...], w_ref[g], preferred_element_type=F32, precision=HIGHEST)
        o_ref[g] = jnp.concatenate([wc, ws], axis=-1).astype(o_ref.dtype)


def _fourier_weights(cc, sc, w):
    g, c, _ = w.shape
    return pl.pallas_call(
        _fourier_w_body,
        out_shape=jax.ShapeDtypeStruct((g, c, 2 * c), BF16),
        name="fourier_weights",
    )(cc, sc, w)


def _fourier_tables(seq):
    quarter = seq // DFT_RADIX
    m = np.arange(quarter)
    tabs = []
    for j in range(DFT_RADIX):
        ang = 2.0 * np.pi * (((DFT_RADIX * m + j)[:, None] * m[None, :]) % seq) / seq
        tabs.append(np.concatenate([np.cos(ang), -np.sin(ang)], axis=1) / math.sqrt(seq))
    return np.stack(tabs).astype(np.float32)


def _class_row_permutation(rows):
    per = rows // DFT_RADIX
    p = np.zeros((rows, rows), np.float32)
    for j in range(DFT_RADIX):
        p[DFT_RADIX * np.arange(per) + j, j * per + np.arange(per)] = 1.0
    return p


def _fourier_body(f_ref, w1_ref, tab_ref, o_ref, yc_ref, ys_ref):
    groups, c, _ = w1_ref.shape
    quarter = f_ref.shape[1] // DFT_RADIX
    for g in range(groups):
        yg = _dot(f_ref[0, :, g * c:(g + 1) * c], w1_ref[g])
        yc_ref[:, g * c:(g + 1) * c] = yg[:, :c]
        ys_ref[:, g * c:(g + 1) * c] = yg[:, c:]
    for j in range(DFT_RADIX):
        pc, ps = [], []
        for q in range(DFT_RADIX):
            rows = slice(q * quarter, (q + 1) * quarter)
            a = (j * q) % 4
            if a == 0:
                pc.append((1, yc_ref, rows)), ps.append((1, ys_ref, rows))
            elif a == 1:
                pc.append((-1, ys_ref, rows)), ps.append((1, yc_ref, rows))
            elif a == 2:
                pc.append((-1, yc_ref, rows)), ps.append((-1, ys_ref, rows))
            else:
                pc.append((1, ys_ref, rows)), ps.append((-1, yc_ref, rows))
        folded = []
        for terms in (pc, ps):
            terms = sorted(terms, key=lambda t: -t[0])
            acc = terms[0][1][terms[0][2], :]
            for sign, ref, rows in terms[1:]:
                acc = acc + ref[rows, :] if sign > 0 else acc - ref[rows, :]
            folded.append(acc.astype(BF16))
        o_ref[0, j] = _dot(tab_ref[j], jnp.concatenate(folded, axis=0)).astype(o_ref.dtype)


def _fourier_mix(f, w1, tab):
    b, s, w = f.shape
    quarter = s // DFT_RADIX
    return pl.pallas_call(
        _fourier_body,
        grid=(b,),
        in_specs=[pl.BlockSpec((1, s, w), lambda bi: (bi, 0, 0)),
                  pl.BlockSpec(w1.shape, lambda bi: (0, 0, 0)),
                  pl.BlockSpec(tab.shape, lambda bi: (0, 0, 0))],
        out_specs=pl.BlockSpec((1, DFT_RADIX, quarter, w), lambda bi: (bi, 0, 0, 0)),
        out_shape=jax.ShapeDtypeStruct((b, DFT_RADIX, quarter, w), BF16),
        scratch_shapes=[pltpu.VMEM((s, w), F32), pltpu.VMEM((s, w), F32)],
        compiler_params=_params(("arbitrary",), 24),
        name="fourier_mix",
    )(f, w1, tab)


def _na_tile_geometry(rows_total):
    last_r0 = rows_total - NA_TILE_ROWS
    return ((0, 0), (2 * NA_TILE_ROWS, 2 * NA_TILE_ROWS - NA_ROWS // 2), (last_r0, rows_total - NA_WIN_ROWS))


def _na_bias_body(rows_total, rpb_ref, o_ref, t2_ref):
    h = pl.program_id(0)
    wq = lax.broadcasted_iota(jnp.int32, (GRID_W, LANES), 0)
    wk = lax.broadcasted_iota(jnp.int32, (GRID_W, LANES), 1) & (GRID_W - 1)
    cs = jnp.clip(wq - NA_COLS // 2, 0, GRID_W - NA_COLS)
    col_ok = (wk >= cs) & (wk < cs + NA_COLS)
    dc = wk - wq + (NA_COLS - 1)
    n_dr, n_dc = 2 * NA_ROWS - 1, 2 * NA_COLS - 1
    neg = jnp.full((GRID_W, LANES), NEG_INF, F32)
    for dr in range(n_dr):
        t2_ref[dr] = neg
    for c in range(n_dc):
        hit = col_ok & (dc == c)
        for dr in range(n_dr):
            t2_ref[dr] = jnp.where(hit, rpb_ref[h * (n_dr * n_dc) + dr * n_dc + c] * LOG2E, t2_ref[dr])
    lane = lax.broadcasted_iota(jnp.int32, (GRID_W, LANES), 1)
    left = lane < GRID_W
    for ty, (r0, ws) in enumerate(_na_tile_geometry(rows_total)):
        for i in range(NA_TILE_ROWS):
            r = r0 + i
            rs = min(max(r - NA_ROWS // 2, 0), rows_total - NA_ROWS)
            for jp in range(NA_WIN_ROWS // 2):
                halves = []
                for kr in (ws + 2 * jp, ws + 2 * jp + 1):
                    halves.append(t2_ref[kr - r + NA_ROWS - 1] if rs <= kr < rs + NA_ROWS else neg)
                o_ref[0, ty, i * GRID_W:(i + 1) * GRID_W, jp * LANES:(jp + 1) * LANES] = jnp.where(left, halves[0], halves[1])


def _na_bias_tables(rpb, rows_total):
    h = rpb.shape[0]
    q_rows, k_cols = NA_TILE_ROWS * GRID_W, NA_WIN_ROWS * GRID_W
    return pl.pallas_call(
        functools.partial(_na_bias_body, rows_total),
        grid=(h,),
        in_specs=[pl.BlockSpec(memory_space=pltpu.SMEM)],
        out_specs=pl.BlockSpec((1, 3, q_rows, k_cols), lambda hi: (hi, 0, 0, 0)),
        out_shape=jax.ShapeDtypeStruct((h, 3, q_rows, k_cols), F32),
        scratch_shapes=[pltpu.VMEM((2 * NA_ROWS - 1, GRID_W, LANES), F32)],
        compiler_params=_params(("arbitrary",), 12),
        name="na_bias_tables",
    )(rpb.reshape(-1))


def _na_body(rows_total, q_ref, k_ref, v_ref, kc_ref, vc_ref, bias_ref, o_ref):
    t = pl.program_id(2)
    n_t = pl.num_programs(2)
    ty = jnp.where(t == 0, 0, jnp.where(t == n_t - 1, 2, 1))
    ws = jnp.clip(t * NA_TILE_ROWS - NA_ROWS // 2, 0, rows_total - NA_WIN_ROWS)
    off = pl.multiple_of(ws * GRID_W, GRID_W)
    n_win = NA_WIN_ROWS * GRID_W
    tq = q_ref.shape[1]
    part = tq // NA_QUERY_SPLIT
    for bb in range(q_ref.shape[0]):
        for qs in range(NA_QUERY_SPLIT):
            rows = slice(qs * part, (qs + 1) * part)
            bias = jnp.concatenate([bias_ref[0, ty, rows, :], bias_ref[1, ty, rows, :]], axis=0)
            chunks = [(kc_ref[bb], vc_ref[bb], None),
                      (k_ref[bb, pl.ds(off, n_win), :], v_ref[bb, pl.ds(off, n_win), :], bias)]
            o = _softmax_pv(_split_heads(q_ref[bb, rows, :]), chunks)
            o_ref[bb, rows, :] = o.astype(o_ref.dtype)


def _neighborhood_attention(q, k, v, kc, vc, bias):
    b, s, w = q.shape
    l = kc.shape[1]
    rows_total = s // GRID_W
    tq = NA_TILE_ROWS * GRID_W
    nb = NA_BATCH_PER_STEP
    full = lambda j, bi, t: (bi, 0, j)
    return pl.pallas_call(
        functools.partial(_na_body, rows_total),
        grid=(w // LANES, b // nb, s // tq),
        in_specs=[pl.BlockSpec((nb, tq, LANES), lambda j, bi, t: (bi, t, j)),
                  pl.BlockSpec((nb, s, LANES), full),
                  pl.BlockSpec((nb, s, 2 * LANES), full),
                  pl.BlockSpec((nb, l, LANES), full),
                  pl.BlockSpec((nb, l, 2 * LANES), full),
                  pl.BlockSpec((2,) + bias.shape[1:], lambda j, bi, t: (j, 0, 0, 0))],
        out_specs=pl.BlockSpec((nb, tq, LANES), lambda j, bi, t: (bi, t, j)),
        out_shape=jax.ShapeDtypeStruct((b, s, w), BF16),
        compiler_params=_params(("arbitrary", "arbitrary", "arbitrary")),
        name="neighborhood_attention",
    )(q, k, v, kc, vc, bias)


def _modulation(m, rows, batch):
    d = m.shape[1] // 3
    pick = lambda lo: jnp.broadcast_to(m[rows, lo:lo + d].reshape(-1, 1, d), (batch, 1, d))
    return pick(0), 1.0 + pick(d), pick(2 * d)


def kernel(x, c, ctx, c_ctx, ab_w_ada, ab_b_ada, ab_norm_g, ab_w_in, ab_conv_w, ab_conv_b, ab_ln_g, ab_ln_b, ab_q_norm_g, ab_k_norm_g, ab_w_out, cd_w_ada, cd_b_ada, cd_norm_g, cd_w_in, cd_w_fourier, cd_rpb, cd_w_out, final_norm_g):
    b, s, d = x.shape
    l = ctx.shape[1]
    assert d == D_MODEL and s % (NA_TILE_ROWS * GRID_W) == 0 and ab_w_ada.shape[0] == 1 and cd_w_ada.shape[0] == 1

    cond = jnp.zeros((16, d), F32).at[:b].set(c).at[b].set(c_ctx)
    lat = slice(0, b)
    cx = slice(b, b + 1)

    m0 = _adaln(cond, ab_w_ada[0], ab_b_ada[0])
    m1 = _adaln(cond, cd_w_ada[0], cd_b_ada[0])
    sh, s1p, gate = _modulation(m0, lat, b)
    csh, cs1p, cgate = _modulation(m0, cx, b)
    sh1, s1p1, gate1 = _modulation(m1, lat, b)
    csh1, cs1p1, _ = _modulation(m1, cx, b)
    ng1 = cd_norm_g[0].reshape(1, d)
    w_in1 = cd_w_in[0].astype(BF16)

    w_in0 = ab_w_in[0].astype(BF16)
    w_out0 = ab_w_out[0].astype(BF16)

    cos, sin = _rope_tables(s)
    cos, sin = jnp.asarray(cos), jnp.asarray(sin)
    mq = jnp.asarray(_block_ones(512)).astype(BF16)
    mk = jnp.asarray(_block_ones(128)).astype(BF16)
    qg = jnp.tile(ab_q_norm_g[0], 512 // HEAD_DIM).reshape(1, 512)
    kg = jnp.tile(ab_k_norm_g[0], 128 // HEAD_DIM).reshape(1, 128)
    ng0 = ab_norm_g[0].reshape(1, d)

    ck, cv, kc, vc = _ctx_layer(ctx, ng0, cs1p, csh, w_in0, qg, kg, mq, mk, ab_conv_w[0], ab_conv_b[0], ab_ln_g[0],
                                ab_ln_b[0], w_out0, cgate, ng1, cs1p1, csh1, w_in1, 1)

    glu, q, k, v, sz = _inproj_ab(x, ng0, s1p, sh, w_in0, qg, kg, cos, sin, mq, mk, 1024)
    y_b = _gqa_attention(q, [ck, k], [cv, v], 1024, 1, chunk=128)
    y_a = _conformer_conv(glu, ab_conv_w[0], ab_conv_b[0], ab_ln_g[0], ab_ln_b[0], 1024)

    x1, f, q, k, v, sz = _outproj_inproj_cd(y_a, y_b, sz, w_out0, x, gate, ng1, s1p1, sh1, w_in1, 1024)
    gate = gate1

    c_c, s_c = _dft_tables(cd_w_fourier.shape[-1])
    w1 = _fourier_weights(jnp.asarray(c_c), jnp.asarray(s_c), cd_w_fourier[0])
    y_c = _fourier_mix(f, w1, jnp.asarray(_fourier_tables(s)).astype(BF16))

    bias = _na_bias_tables(cd_rpb[0], s // GRID_W)
    y_d = _neighborhood_attention(q, k, v, kc, vc, bias)

    return _outproj_final(y_c, y_d, sz, cd_w_out[0].astype(BF16), x1, gate, final_norm_g.reshape(1, d), 1024)
```
